```python
import math
import jax, jax.numpy as jnp
from jax import lax
import numpy as np

D_MODEL = 2048
BATCH = 8
SEQ = 2048
DEPTH = 2
DEC_BATCH = 128
DEC_SEQ = 8
PAST_LEN = 2048
PAGE_SIZE = 128

N_MIXERS = 2
N_A_LAYERS = (DEPTH + N_MIXERS - 1) // N_MIXERS
N_B_LAYERS = DEPTH // N_MIXERS
CHUNK = 128
A_WIDTH = D_MODEL
A_GROUPS = 16
A_GROUP_DIM = A_WIDTH // A_GROUPS
N_HEADS = 16
HEAD_DIM = D_MODEL // N_HEADS
N_KV_HEADS = 4
Q_PER_KV = N_HEADS // N_KV_HEADS
IDX_HEADS = 16
IDX_DIM = 64
TOPK_MAX = 256
Q_BLOCK = 128
ROPE_THETA = 10000.0
IDX_W_SCALE = (IDX_HEADS * IDX_DIM) ** -0.5
B_Q = N_HEADS * HEAD_DIM
B_KV = N_KV_HEADS * HEAD_DIM
B_IQ = IDX_HEADS * IDX_DIM
B_PROJ = B_Q + 2 * B_KV + B_IQ + IDX_DIM + IDX_HEADS
B_SPLITS = [B_Q, B_Q + B_KV, B_Q + 2 * B_KV, B_Q + 2 * B_KV + B_IQ, B_Q + 2 * B_KV + B_IQ + IDX_DIM]
D_FF = 5632
CONV_WIDTH = 3
EPS = 1e-6

kernel_name = "hybrid_chunkgmlp_dsa_convffn_step"


def _rmsnorm(x, g):
    xf = x.astype(jnp.float32)
    y = xf * lax.rsqrt(jnp.mean(xf * xf, axis=-1, keepdims=True) + EPS)
    return (y * g.astype(jnp.float32)).astype(x.dtype)


def _rope(x, pos):
    d = x.shape[-1]
    half = d // 2
    inv = ROPE_THETA ** (-jnp.arange(half, dtype=jnp.float32) * (2.0 / d))
    ang = pos.astype(jnp.float32)[:, None] * inv[None, :]
    cos = jnp.cos(ang)[:, None, :]
    sin = jnp.sin(ang)[:, None, :]
    xf = x.astype(jnp.float32)
    x1, x2 = xf[..., :half], xf[..., half:]
    return jnp.concatenate([x1 * cos - x2 * sin, x2 * cos + x1 * sin], axis=-1).astype(x.dtype)


def _chunk_mlp(h, w_in, v_gain, w_s, b_s, w_out):
    bsz, t, _ = h.shape
    c = min(t, CHUNK)
    n = t // c
    z = jax.nn.gelu(h @ w_in)
    u, v = z[..., :A_WIDTH], z[..., A_WIDTH:]
    v = _rmsnorm(v, v_gain)
    causal = jnp.tril(jnp.ones((c, c), dtype=bool))
    ws = jnp.where(causal[None], w_s[:, :c, :c], 0).astype(v.dtype)
    vg = v.reshape(bsz, n, c, A_GROUPS, A_GROUP_DIM)
    s = jnp.einsum('gts,bnsgd->bntgd', ws, vg) + b_s[:, :c].T[:, :, None]
    s = s.reshape(bsz, t, A_WIDTH)
    return (u * s) @ w_out, v


def _dsa_project(h, pos, w_in, q_gain, k_gain):
    bsz, t, _ = h.shape
    p = h @ w_in
    q, k, v, iq, ik, iw = jnp.split(p, B_SPLITS, axis=-1)
    q = _rope(_rmsnorm(q.reshape(bsz, t, N_HEADS, HEAD_DIM), q_gain), pos)
    k = _rope(_rmsnorm(k.reshape(bsz, t, N_KV_HEADS, HEAD_DIM), k_gain), pos)
    v = v.reshape(bsz, t, N_KV_HEADS, HEAD_DIM)
    iq = _rope(iq.reshape(bsz, t, IDX_HEADS, IDX_DIM), pos)
    ik = _rope(ik[:, :, None, :], pos)[:, :, 0, :]
    iw = iw * IDX_W_SCALE
    return q, k, v, iq, ik, iw


def _index_select(iq, iw, ik, qpos, kpos, topk):
    logits = jnp.einsum('bqhd,bsd->bqsh', iq, ik, preferred_element_type=jnp.float32)
    score = jnp.einsum('bqsh,bqh->bqs', jax.nn.relu(logits), iw.astype(jnp.float32))
    admissible = kpos[None, :] <= qpos[:, None]
    score = jnp.where(admissible[None], score, -jnp.inf)
    _, sel = lax.top_k(score, topk)
    valid = sel <= qpos[None, :, None]
    return sel, valid


def _sparse_attend(q, k_sel, v_sel, valid):
    bsz, t = q.shape[:2]
    qg = q.reshape(bsz, t, N_KV_HEADS, Q_PER_KV, HEAD_DIM)
    s = jnp.einsum('btkgd,btskd->btkgs', qg, k_sel, preferred_element_type=jnp.float32) * (HEAD_DIM ** -0.5)
    s = jnp.where(valid[:, :, None, None, :], s, -jnp.inf)
    p = jax.nn.softmax(s, axis=-1).astype(v_sel.dtype)
    o = jnp.einsum('btkgs,btskd->btkgd', p, v_sel)
    return o.reshape(bsz, t, N_HEADS * HEAD_DIM)


def _dsa_prompt(h, w_in, q_gain, k_gain, w_o):
    bsz, s_len, _ = h.shape
    pos = jnp.arange(s_len, dtype=jnp.int32)
    q, k, v, iq, ik, iw = _dsa_project(h, pos, w_in, q_gain, k_gain)
    topk = min(TOPK_MAX, s_len // 4)
    bidx = jnp.arange(bsz)[:, None, None]

    def block(start):
        qpos = start + jnp.arange(Q_BLOCK, dtype=jnp.int32)
        q_b = lax.dynamic_slice_in_dim(q, start, Q_BLOCK, axis=1)
        iq_b = lax.dynamic_slice_in_dim(iq, start, Q_BLOCK, axis=1)
        iw_b = lax.dynamic_slice_in_dim(iw, start, Q_BLOCK, axis=1)
        sel, valid = _index_select(iq_b, iw_b, ik, qpos, pos, topk)
        return _sparse_attend(q_b, k[bidx, sel], v[bidx, sel], valid)

    starts = jnp.arange(0, s_len, Q_BLOCK, dtype=jnp.int32)
    o = lax.map(block, starts)
    o = jnp.moveaxis(o, 0, 1).reshape(bsz, s_len, N_HEADS * HEAD_DIM)
    return o @ w_o, k, v, ik


def _gather_rows(pool, layer, page_table, new, sel, past):
    bidx = jnp.arange(sel.shape[0])[:, None, None]
    ps = jnp.minimum(sel, past - 1)
    old = pool[layer, page_table[bidx, ps // PAGE_SIZE], ps % PAGE_SIZE]
    cur = new[bidx, jnp.clip(sel - past, 0, new.shape[1] - 1)]
    return jnp.where((sel < past)[..., None, None], old, cur)


def _dsa_sample(h, layer, cache_k, cache_v, cache_idx_k, page_table, w_in, q_gain, k_gain, w_o):
    bsz, t, _ = h.shape
    past = page_table.shape[1] * PAGE_SIZE
    pos = past + jnp.arange(t, dtype=jnp.int32)
    q, k, v, iq, ik, iw = _dsa_project(h, pos, w_in, q_gain, k_gain)
    ik_past = cache_idx_k[layer, page_table].reshape(bsz, past, IDX_DIM)
    ik_all = jnp.concatenate([ik_past, ik.astype(ik_past.dtype)], axis=1)
    kpos = jnp.arange(past + t, dtype=jnp.int32)
    topk = min(TOPK_MAX, (past + t) // 4)
    sel, valid = _index_select(iq, iw, ik_all, pos, kpos, topk)
    k_sel = _gather_rows(cache_k, layer, page_table, k, sel, past)
    v_sel = _gather_rows(cache_v, layer, page_table, v, sel, past)
    o = _sparse_attend(q, k_sel, v_sel, valid)
    return o @ w_o, k, v, ik


def _conv_ffn(h, conv_buf, w_in, conv_w, conv_b, w_out):
    t = h.shape[1]
    a = h @ w_in
    ap = jnp.concatenate([conv_buf.astype(a.dtype), a], axis=1)
    c = conv_b
    for j in range(CONV_WIDTH):
        c = c + conv_w[j] * ap[:, j:j + t]
    g, up = c[..., :D_FF], c[..., D_FF:]
    return (jax.nn.silu(g) * up) @ w_out, ap[:, t:]


def setup_inputs(seed: int = 0) -> dict:
    key = jax.random.key(seed)
    ks = jax.random.split(key, 32)
    n_pages = PAST_LEN // PAGE_SIZE
    n_used = DEC_BATCH * n_pages
    n_pool = n_used + (n_used + 3) // 4

    def nrm(k, shape, scale=1.0):
        return jax.random.normal(k, shape, jnp.float32) * scale

    def gain(k, shape):
        return 1.0 + 0.02 * jax.random.normal(k, shape, jnp.float32)

    page_table = jax.random.permutation(ks[7], n_pool)[:n_used].reshape(DEC_BATCH, n_pages).astype(jnp.int32)
    return {
        'x_prompt': nrm(ks[0], (BATCH, SEQ, D_MODEL)),
        'x_sample': nrm(ks[1], (DEC_BATCH, DEC_SEQ, D_MODEL)),
        'cache_k': nrm(ks[2], (N_B_LAYERS, n_pool, PAGE_SIZE, N_KV_HEADS, HEAD_DIM)),
        'cache_v': nrm(ks[3], (N_B_LAYERS, n_pool, PAGE_SIZE, N_KV_HEADS, HEAD_DIM)),
        'cache_idx_k': nrm(ks[4], (N_B_LAYERS, n_pool, PAGE_SIZE, IDX_DIM)),
        'state_ffn_conv': nrm(ks[5], (DEPTH, DEC_BATCH, CONV_WIDTH - 1, 2 * D_FF)),
        'page_table': page_table,
        'a_norm': gain(ks[8], (N_A_LAYERS, D_MODEL)),
        'a_w_in': nrm(ks[9], (N_A_LAYERS, D_MODEL, 2 * A_WIDTH), D_MODEL ** -0.5),
        'a_v_norm': gain(ks[10], (N_A_LAYERS, A_WIDTH)),
        'a_w_s': nrm(ks[11], (N_A_LAYERS, A_GROUPS, CHUNK, CHUNK), CHUNK ** -0.5),
        'a_b_s': gain(ks[12], (N_A_LAYERS, A_GROUPS, CHUNK)),
        'a_w_out': nrm(ks[13], (N_A_LAYERS, A_WIDTH, D_MODEL), A_WIDTH ** -0.5),
        'b_norm': gain(ks[14], (N_B_LAYERS, D_MODEL)),
        'b_w_in': nrm(ks[15], (N_B_LAYERS, D_MODEL, B_PROJ), D_MODEL ** -0.5),
        'b_q_norm': gain(ks[16], (N_B_LAYERS, HEAD_DIM)),
        'b_k_norm': gain(ks[17], (N_B_LAYERS, HEAD_DIM)),
        'b_w_o': nrm(ks[18], (N_B_LAYERS, N_HEADS * HEAD_DIM, D_MODEL), (N_HEADS * HEAD_DIM) ** -0.5),
        'f_norm': gain(ks[19], (DEPTH, D_MODEL)),
        'f_w_in': nrm(ks[20], (DEPTH, D_MODEL, 2 * D_FF), D_MODEL ** -0.5),
        'f_conv_w': nrm(ks[21], (DEPTH, CONV_WIDTH, 2 * D_FF), CONV_WIDTH ** -0.5),
        'f_conv_b': nrm(ks[22], (DEPTH, 2 * D_FF), 0.01),
        'f_w_out': nrm(ks[23], (DEPTH, D_FF, D_MODEL), D_FF ** -0.5),
    }


def reference(x_prompt, x_sample, cache_k, cache_v, cache_idx_k, state_ffn_conv, page_table,
              a_norm, a_w_in, a_v_norm, a_w_s, a_b_s, a_w_out,
              b_norm, b_w_in, b_q_norm, b_k_norm, b_w_o,
              f_norm, f_w_in, f_conv_w, f_conv_b, f_w_out):
    xp, xs = x_prompt, x_sample
    bsz, seq = xp.shape[:2]
    n_seq_pages = seq // PAGE_SIZE
    k_p, v_p, ik_p, k_s, v_s, ik_s, chunk_v_s, conv_p, conv_s = [], [], [], [], [], [], [], [], []
    for layer in range(DEPTH):
        j = layer // N_MIXERS
        if layer % N_MIXERS == 0:
            op, _ = _chunk_mlp(_rmsnorm(xp, a_norm[j]), a_w_in[j], a_v_norm[j], a_w_s[j], a_b_s[j], a_w_out[j])
            os_, vrows = _chunk_mlp(_rmsnorm(xs, a_norm[j]), a_w_in[j], a_v_norm[j], a_w_s[j], a_b_s[j], a_w_out[j])
            chunk_v_s.append(vrows)
        else:
            op, kp, vp, ikp = _dsa_prompt(_rmsnorm(xp, b_norm[j]), b_w_in[j], b_q_norm[j], b_k_norm[j], b_w_o[j])
            os_, ksn, vsn, iks = _dsa_sample(_rmsnorm(xs, b_norm[j]), j, cache_k, cache_v, cache_idx_k, page_table,
                                             b_w_in[j], b_q_norm[j], b_k_norm[j], b_w_o[j])
            k_p.append(kp.reshape(bsz, n_seq_pages, PAGE_SIZE, N_KV_HEADS, HEAD_DIM))
            v_p.append(vp.reshape(bsz, n_seq_pages, PAGE_SIZE, N_KV_HEADS, HEAD_DIM))
            ik_p.append(ikp.reshape(bsz, n_seq_pages, PAGE_SIZE, IDX_DIM))
            k_s.append(ksn)
            v_s.append(vsn)
            ik_s.append(iks)
        xp = xp + op
        xs = xs + os_
        zeros_buf = jnp.zeros((bsz, CONV_WIDTH - 1, 2 * D_FF), xp.dtype)
        fp, cp = _conv_ffn(_rmsnorm(xp, f_norm[layer]), zeros_buf, f_w_in[layer], f_conv_w[layer], f_conv_b[layer], f_w_out[layer])
        fs, cs = _conv_ffn(_rmsnorm(xs, f_norm[layer]), state_ffn_conv[layer], f_w_in[layer], f_conv_w[layer], f_conv_b[layer], f_w_out[layer])
        xp = xp + fp
        xs = xs + fs
        conv_p.append(cp)
        conv_s.append(cs)
    return (xp, xs, jnp.stack(k_p), jnp.stack(v_p), jnp.stack(ik_p), jnp.stack(k_s), jnp.stack(v_s), jnp.stack(ik_s), jnp.stack(chunk_v_s), jnp.stack(conv_p), jnp.stack(conv_s))
```

```python
import functools
import math

import jax
import jax.numpy as jnp
from jax import lax
from jax.experimental import pallas as pl
from jax.experimental.pallas import tpu as pltpu

EPS = 1e-6
ROPE_THETA = 10000.0
TOPK_MAX = 256
LANES = 128
NEG_BIG = -1e30
INT_MIN = -(2 ** 31)
VMEM_LIMIT = 56 * 1024 * 1024
TM_ROWS = 512

f32 = jnp.float32
bf16 = jnp.bfloat16
i32 = jnp.int32


def _params(*sem):
    return pltpu.CompilerParams(dimension_semantics=sem, vmem_limit_bytes=VMEM_LIMIT)


def _dot(a, b):
    return jnp.dot(a, b, preferred_element_type=f32)


def _dot_nt(a, b):
    return lax.dot_general(a, b, (((1,), (1,)), ((), ())), preferred_element_type=f32)


def _rms(x, g):
    r = lax.rsqrt(jnp.mean(x * x, axis=-1, keepdims=True) + EPS)
    return x * r * g


def _gelu_tanh(x):
    c = math.sqrt(2.0 / math.pi)
    return 0.5 * x * (1.0 + jnp.tanh(c * (x + 0.044715 * (x * x * x))))


def _nmm_body(x_ref, g_ref, w_ref, o_ref, h_ref, *, act):
    @pl.when(pl.program_id(1) == 0)
    def _():
        h_ref[...] = _rms(x_ref[...], g_ref[...]).astype(bf16)

    z = _dot(h_ref[...], w_ref[...])
    if act == "gelu":
        z = _gelu_tanh(z)
    o_ref[...] = z.astype(o_ref.dtype)


def _norm_matmul(x, gain, w, *, act=None, tm, tn, out_dtype=f32):
    m, k = x.shape
    n = w.shape[1]
    return pl.pallas_call(
        functools.partial(_nmm_body, act=act),
        grid=(m // tm, n // tn),
        in_specs=[
            pl.BlockSpec((tm, k), lambda i, j: (i, 0)),
            pl.BlockSpec((1, k), lambda i, j: (0, 0)),
            pl.BlockSpec((k, tn), lambda i, j: (0, j)),
        ],
        out_specs=pl.BlockSpec((tm, tn), lambda i, j: (i, j)),
        out_shape=jax.ShapeDtypeStruct((m, n), out_dtype),
        scratch_shapes=[pltpu.VMEM((tm, k), bf16)],
        compiler_params=_params("parallel", "arbitrary"),
        name="norm_matmul",
    )(x, gain.reshape(1, k), w)


def _mmres_body(a_ref, w_ref, x_ref, o_ref):
    o_ref[...] = x_ref[...] + _dot(a_ref[...], w_ref[...])


def _matmul_residual(a, w, x, *, tm, tn):
    m, k = a.shape
    n = w.shape[1]
    return pl.pallas_call(
        _mmres_body,
        grid=(m // tm, n // tn),
        in_specs=[
            pl.BlockSpec((tm, k), lambda i, j: (i, 0)),
            pl.BlockSpec((k, tn), lambda i, j: (0, j)),
            pl.BlockSpec((tm, tn), lambda i, j: (i, j)),
        ],
        out_specs=pl.BlockSpec((tm, tn), lambda i, j: (i, j)),
        out_shape=jax.ShapeDtypeStruct((m, n), f32),
        compiler_params=_params("parallel", "arbitrary"),
        name="matmul_residual",
    )(a, w, x)


def _gate_body(u_ref, v_ref, vg_ref, wmix_ref, bias_ref, o_ref, *vn_out, tm, groups):
    vn = _rms(v_ref[...], vg_ref[...])
    if vn_out:
        vn_out[0][...] = vn
    vnb = vn.astype(bf16)
    gd = v_ref.shape[1] // groups
    for c in range(tm // LANES):
        rows = slice(c * LANES, (c + 1) * LANES)
        for g in range(groups):
            cols = slice(g * gd, (g + 1) * gd)
            s = _dot(wmix_ref[g], vnb[rows, cols]) + bias_ref[:, cols]
            o_ref[rows, cols] = (u_ref[rows, cols] * s).astype(bf16)


def _spatial_gate(z, v_gain, wmix, bias, *, tm, want_vn):
    m, w2 = z.shape
    w = w2 // 2
    groups = wmix.shape[0]
    out_shape = [jax.ShapeDtypeStruct((m, w), bf16)]
    out_specs = [pl.BlockSpec((tm, w), lambda i: (i, 0))]
    if want_vn:
        out_shape.append(jax.ShapeDtypeStruct((m, w), f32))
        out_specs.append(pl.BlockSpec((tm, w), lambda i: (i, 0)))
    res = pl.pallas_call(
        functools.partial(_gate_body, tm=tm, groups=groups),
        grid=(m // tm,),
        in_specs=[
            pl.BlockSpec((tm, w), lambda i: (i, 0)),
            pl.BlockSpec((tm, w), lambda i: (i, 1)),
            pl.BlockSpec((1, w), lambda i: (0, 0)),
            pl.BlockSpec(wmix.shape, lambda i: (0, 0, 0)),
            pl.BlockSpec(bias.shape, lambda i: (0, 0)),
        ],
        out_specs=out_specs,
        out_shape=out_shape,
        compiler_params=_params("parallel"),
        name="spatial_gate",
    )(z, z, v_gain.reshape(1, w), wmix, bias)
    return res if want_vn else (res[0], None)


def _conv3(a, cw, cb, prev1, prev2):
    return cb + cw[0:1] * prev2 + cw[1:2] * prev1 + cw[2:3] * a


def _ffn_body(x_ref, g_ref, wg_ref, wu_ref, cwg_ref, cwu_ref, cbg_ref, cbu_ref, wo_ref, *rest,
              sample, seq_tiles, seq_len):
    if sample:
        s1g_ref, s1u_ref, s2g_ref, s2u_ref, o_ref, ag_ref, au_ref, h_ref = rest
    else:
        o_ref, tg_ref, tu_ref, h_ref, carry_ref = rest
    i = pl.program_id(0)
    j = pl.program_id(1)

    @pl.when(j == 0)
    def _():
        h_ref[...] = _rms(x_ref[...], g_ref[...]).astype(bf16)

    h = h_ref[...]
    ag = _dot(h, wg_ref[...])
    au = _dot(h, wu_ref[...])
    tm, tn = ag.shape
    row = lax.broadcasted_iota(i32, (tm, tn), 0)
    if sample:
        t = row & (seq_len - 1)
        m1 = t == 0
        m2 = t < 2
        p1g, p1u, p2g, p2u = s1g_ref[...], s1u_ref[...], s2g_ref[...], s2u_ref[...]
        ag_ref[...] = ag
        au_ref[...] = au
    else:
        m1 = row == 0
        m2 = row < 2
        live = (i % seq_tiles) != 0
        cg = jnp.where(live, carry_ref[j, 0], 0.0)
        cu = jnp.where(live, carry_ref[j, 1], 0.0)
        p1g, p1u = cg[7:8], cu[7:8]
        p2g = jnp.where(m1, cg[6:7], cg[7:8])
        p2u = jnp.where(m1, cu[6:7], cu[7:8])
        carry_ref[j, 0] = ag[tm - 8:]
        carry_ref[j, 1] = au[tm - 8:]
        tg_ref[0] = ag[tm - 8:]
        tu_ref[0] = au[tm - 8:]
    a1g = jnp.where(m1, p1g, pltpu.roll(ag, 1, 0))
    a2g = jnp.where(m2, p2g, pltpu.roll(ag, 2, 0))
    a1u = jnp.where(m1, p1u, pltpu.roll(au, 1, 0))
    a2u = jnp.where(m2, p2u, pltpu.roll(au, 2, 0))
    cg_ = _conv3(ag, cwg_ref[...], cbg_ref[...], a1g, a2g)
    cu_ = _conv3(au, cwu_ref[...], cbu_ref[...], a1u, a2u)
    act = (cg_ * (1.0 / (1.0 + jnp.exp(-cg_))) * cu_).astype(bf16)
    contrib = _dot(act, wo_ref[...])

    @pl.when(j == 0)
    def _():
        o_ref[...] = x_ref[...] + contrib

    @pl.when(j > 0)
    def _():
        o_ref[...] += contrib


def _conv_ffn(x, gain, w_in, conv_w, conv_b, w_out, *, tm, tn, seq_len, state=None):
    m, d = x.shape
    ff = w_out.shape[0]
    nj = ff // tn
    ni = m // tm
    sample = state is not None
    cb2 = conv_b.reshape(1, 2 * ff)
    in_specs = [
        pl.BlockSpec((tm, d), lambda i, j: (i, 0)),
        pl.BlockSpec((1, d), lambda i, j: (0, 0)),
        pl.BlockSpec((d, tn), lambda i, j: (0, j)),
        pl.BlockSpec((d, tn), lambda i, j: (0, nj + j)),
        pl.BlockSpec((3, tn), lambda i, j: (0, j)),
        pl.BlockSpec((3, tn), lambda i, j: (0, nj + j)),
        pl.BlockSpec((1, tn), lambda i, j: (0, j)),
        pl.BlockSpec((1, tn), lambda i, j: (0, nj + j)),
        pl.BlockSpec((tn, d), lambda i, j: (j, 0)),
    ]
    args = [x, gain.reshape(1, d), w_in, w_in, conv_w, conv_w, cb2, cb2, w_out]
    out_specs = [pl.BlockSpec((tm, d), lambda i, j: (i, 0))]
    out_shape = [jax.ShapeDtypeStruct((m, d), f32)]
    scratch = [pltpu.VMEM((tm, d), bf16)]
    if sample:
        s1, s2 = state
        in_specs += [
            pl.BlockSpec((tm, tn), lambda i, j: (i, j)),
            pl.BlockSpec((tm, tn), lambda i, j: (i, nj + j)),
            pl.BlockSpec((tm, tn), lambda i, j: (i, j)),
            pl.BlockSpec((tm, tn), lambda i, j: (i, nj + j)),
        ]
        args += [s1, s1, s2, s2]
        out_specs += [pl.BlockSpec((tm, tn), lambda i, j: (i, j))] * 2
        out_shape += [jax.ShapeDtypeStruct((m, ff), f32)] * 2
        seq_tiles = 1
    else:
        out_specs += [pl.BlockSpec((1, 8, tn), lambda i, j: (i, 0, j))] * 2
        out_shape += [jax.ShapeDtypeStruct((ni, 8, ff), f32)] * 2
        scratch.append(pltpu.VMEM((nj, 2, 8, tn), f32))
        seq_tiles = seq_len // tm
    return pl.pallas_call(
        functools.partial(_ffn_body, sample=sample, seq_tiles=seq_tiles, seq_len=seq_len),
        grid=(ni, nj),
        in_specs=in_specs,
        out_specs=out_specs,
        out_shape=out_shape,
        scratch_shapes=scratch,
        compiler_params=_params("arbitrary", "arbitrary"),
        name="conv_ffn_sample" if sample else "conv_ffn_prompt",
    )(*args)


def _rope_full(x, c, s):
    return x * c + pltpu.roll(x, LANES // 2, 1) * s


def _rope_half(x, c, s, lo):
    partner = jnp.where(lo, pltpu.roll(x, 96, 1), pltpu.roll(x, 32, 1))
    return x * c + partner * s


def _post_body(p_ref, qg_ref, kg_ref, c1_ref, s1_ref, c2_ref, s2_ref,
               q_ref, k_ref, v_ref, kb_ref, vb_ref, iq_ref, ikw_ref, ik2_ref,
               *, n_heads, n_kv, n_ih, idx_scale, q_scale):
    hd = LANES
    c1, s1, c2, s2 = c1_ref[...], s1_ref[...], c2_ref[...], s2_ref[...]
    tm = p_ref.shape[0]
    lane = lax.broadcasted_iota(i32, (tm, LANES), 1)
    lo = (lane & 63) < 32
    off = 0
    for h in range(n_heads):
        x = p_ref[:, off + h * hd: off + (h + 1) * hd]
        q_ref[:, h * hd:(h + 1) * hd] = (_rope_full(_rms(x, qg_ref[...]), c1, s1) * q_scale).astype(bf16)
    off += n_heads * hd
    for h in range(n_kv):
        x = p_ref[:, off + h * hd: off + (h + 1) * hd]
        kr = _rope_full(_rms(x, kg_ref[...]), c1, s1)
        k_ref[:, h * hd:(h + 1) * hd] = kr
        kb_ref[:, h * hd:(h + 1) * hd] = kr.astype(bf16)
    off += n_kv * hd
    vv = p_ref[:, off: off + n_kv * hd]
    v_ref[...] = vv
    vb_ref[...] = vv.astype(bf16)
    off += n_kv * hd
    for h in range(n_ih // 2):
        x = p_ref[:, off + h * LANES: off + (h + 1) * LANES]
        iq_ref[:, h * LANES:(h + 1) * LANES] = _rope_half(x, c2, s2, lo).astype(bf16)
    off += (n_ih // 2) * LANES
    tail = p_ref[:, off: off + LANES]
    tr = _rope_half(tail, c2, s2, lo)
    ik = jnp.where(lane < 64, tr, 0.0)
    ik2_ref[...] = (ik + pltpu.roll(ik, 64, 1)).astype(bf16)
    ikw_ref[...] = jnp.where(lane < 64, tr, tail * idx_scale)


def _attn_post(p, q_gain, k_gain, tabs, *, tm, n_heads, n_kv, n_ih):
    m, npad = p.shape
    c1, s1, c2, s2 = tabs
    nt = c1.shape[0] // tm
    hd = LANES
    tab_spec = pl.BlockSpec((tm, LANES), lambda i: (i % nt, 0))
    row = lambda w: pl.BlockSpec((tm, w), lambda i: (i, 0))
    outs = [
        (n_heads * hd, bf16), (n_kv * hd, f32), (n_kv * hd, f32), (n_kv * hd, bf16), (n_kv * hd, bf16),
        (n_ih * 64, bf16), (LANES, f32), (LANES, bf16),
    ]
    return pl.pallas_call(
        functools.partial(_post_body, n_heads=n_heads, n_kv=n_kv, n_ih=n_ih,
                          idx_scale=float((n_ih * 64) ** -0.5), q_scale=float(hd ** -0.5)),
        grid=(m // tm,),
        in_specs=[row(npad), pl.BlockSpec((1, hd), lambda i: (0, 0)), pl.BlockSpec((1, hd), lambda i: (0, 0)),
                  tab_spec, tab_spec, tab_spec, tab_spec],
        out_specs=[row(w) for w, _ in outs],
        out_shape=[jax.ShapeDtypeStruct((m, w), dt) for w, dt in outs],
        compiler_params=_params("parallel"),
        name="attn_post",
    )(p, q_gain.reshape(1, hd), k_gain.reshape(1, hd), c1, s1, c2, s2)


def _sortable_key(score):
    bits = lax.bitcast_convert_type(score, i32)
    return bits ^ ((bits >> 31) & 0x7FFFFFFF)


def _select_threshold(count_fn, shape, topk, idx_bits):
    kk = float(topk)

    def bit_body(it, res):
        cand = res + lax.shift_left(jnp.int32(1), 31 - it)
        cnt = count_fn(lambda key, idx: key >= cand)
        return jnp.where(cnt >= kk, cand, res)

    thr = lax.fori_loop(0, 32, bit_body, jnp.full(shape, INT_MIN, i32))
    need = kk - count_fn(lambda key, idx: key > thr)

    def tie_body(it, end):
        cand = end + lax.shift_left(jnp.int32(1), idx_bits - 1 - it)
        cnt = count_fn(lambda key, idx: (key == thr) & (idx < cand))
        return jnp.where(cnt <= need, cand, end)

    tie_end = lax.fori_loop(0, idx_bits, tie_body, jnp.zeros(shape, i32))
    return thr, tie_end


def _dsa_prompt_body(q_ref, iq_ref, ikw_ref, ik2_ref, kb_ref, vb_ref, o_ref,
                     key_ref, bias_ref, iqm_ref, iwb_ref, m_ref, l_ref, acc_ref,
                     *, topk, n_kv, group, n_ih, idx_bits):
    T = LANES
    qb = pl.program_id(1)
    nkt = qb + 1
    lane = lax.broadcasted_iota(i32, (T, T), 1)
    row = lax.broadcasted_iota(i32, (T, T), 0)
    qidx = qb * T + row

    for h in range(n_ih):
        pair = iq_ref[0, :, (h // 2) * T:(h // 2 + 1) * T].astype(f32)
        keep = (lane < 64) if h % 2 == 0 else (lane >= 64)
        iqm_ref[h] = jnp.where(keep, pair, 0.0).astype(bf16)
        iwb_ref[h] = jnp.broadcast_to(ikw_ref[0, :, 64 + h:65 + h], (T, T))

    def score_tile(kt, carry):
        ik_t = ik2_ref[0, pl.ds(pl.multiple_of(kt * T, T), T), :]
        acc = jnp.zeros((T, T), f32)
        for h in range(n_ih):
            acc = acc + jnp.maximum(_dot_nt(iqm_ref[h], ik_t), 0.0) * iwb_ref[h]
        key_ref[kt] = jnp.where(kt * T + lane <= qidx, _sortable_key(acc), INT_MIN)
        return carry

    lax.fori_loop(0, nkt, score_tile, 0)

    def count_fn(pred):
        def body(kt, c):
            return c + jnp.where(pred(key_ref[kt], kt * T + lane), 1.0, 0.0)
        c = lax.fori_loop(0, nkt, body, jnp.zeros((T, T), f32))
        return jnp.sum(c, axis=1, keepdims=True)

    thr, tie_end = _select_threshold(count_fn, (T, T), topk, idx_bits)

    def bias_tile(kt, carry):
        key = key_ref[kt]
        kidx = kt * T + lane
        sel = ((key > thr) | ((key == thr) & (kidx < tie_end))) & (kidx <= qidx)
        bias_ref[kt] = jnp.where(sel, 0.0, NEG_BIG)
        return carry

    lax.fori_loop(0, nkt, bias_tile, 0)

    for kvh in range(n_kv):
        qg = jnp.concatenate(
            [q_ref[0, :, (kvh * group + g) * T:(kvh * group + g + 1) * T] for g in range(group)], axis=0)
        m_ref[...] = jnp.full(m_ref.shape, NEG_BIG, f32)
        l_ref[...] = jnp.zeros(l_ref.shape, f32)
        acc_ref[...] = jnp.zeros(acc_ref.shape, f32)

        def att_tile(kt, carry, qg=qg, kvh=kvh):
            rows = pl.ds(pl.multiple_of(kt * T, T), T)
            k_t = kb_ref[0, rows, kvh * T:(kvh + 1) * T]
            v_t = vb_ref[0, rows, kvh * T:(kvh + 1) * T]
            b = bias_ref[kt]
            s = _dot_nt(qg, k_t) + jnp.concatenate([b] * group, axis=0)
            m_prev = m_ref[...]
            m_new = jnp.maximum(m_prev, jnp.max(s, axis=1, keepdims=True))
            alpha = jnp.exp(m_prev - m_new)
            p = jnp.exp(s - m_new)
            l_ref[...] = alpha * l_ref[...] + jnp.sum(p, axis=1, keepdims=True)
            acc_ref[...] = alpha * acc_ref[...] + _dot(p.astype(bf16), v_t)
            m_ref[...] = m_new
            return carry

        lax.fori_loop(0, nkt, att_tile, 0)
        o = acc_ref[...] / l_ref[...]
        for g in range(group):
            h = kvh * group + g
            o_ref[0, :, h * T:(h + 1) * T] = o[g * T:(g + 1) * T].astype(bf16)


def _dsa_prompt(q, iq, ikw, ik2, kb, vb, *, topk, n_kv, n_ih):
    b, s, qd = q.shape
    T = LANES
    n_heads = qd // T
    group = n_heads // n_kv
    nq = s // T
    blk = lambda w: pl.BlockSpec((1, T, w), lambda bi, qi: (bi, qi, 0))
    full = lambda w: pl.BlockSpec((1, s, w), lambda bi, qi: (bi, 0, 0))
    return pl.pallas_call(
        functools.partial(_dsa_prompt_body, topk=topk, n_kv=n_kv, group=group, n_ih=n_ih,
                          idx_bits=int(s).bit_length()),
        grid=(b, nq),
        in_specs=[blk(qd), blk(iq.shape[2]), blk(T), full(T), full(n_kv * T), full(n_kv * T)],
        out_specs=blk(qd),
        out_shape=jax.ShapeDtypeStruct((b, s, qd), bf16),
        scratch_shapes=[
            pltpu.VMEM((nq, T, T), i32), pltpu.VMEM((nq, T, T), f32),
            pltpu.VMEM((n_ih, T, T), bf16), pltpu.VMEM((n_ih, T, T), f32),
            pltpu.VMEM((group * T, T), f32), pltpu.VMEM((group * T, T), f32), pltpu.VMEM((group * T, T), f32),
        ],
        compiler_params=_params("parallel", "arbitrary"),
        name="dsa_prompt",
    )(q, iq, ikw, ik2, kb, vb)


def _dsa_sample_body(pt_ref, q_ref, iq_ref, iwb_ref, ikn_ref, kn_ref, vn_ref, *rest,
                     n_pages, topk, n_kv, n_tok, idx_bits):
    del pt_ref
    T = LANES
    idx_pages = rest[:n_pages]
    k_pages = rest[n_pages:2 * n_pages]
    v_pages = rest[2 * n_pages:3 * n_pages]
    o_ref, key_ref, bias_ref, s_ref = rest[3 * n_pages:]
    nt = n_pages + 1
    n_rows = q_ref.shape[1]
    rows_kv = n_rows // n_kv
    lane = lax.broadcasted_iota(i32, (n_tok, T), 1)
    tok = lax.broadcasted_iota(i32, (n_tok, T), 0)

    iq = iq_ref[0]
    iwb = iwb_ref[0]
    for p in range(nt):
        ikp = (idx_pages[p][...] if p < n_pages else ikn_ref[0]).astype(bf16)
        w = jnp.maximum(_dot_nt(iq, ikp), 0.0) * iwb
        sc = jnp.sum(w.reshape(n_rows // n_tok, n_tok, T), axis=0)
        key = _sortable_key(sc)
        if p == n_pages:
            key = jnp.where((lane <= tok) & (lane < n_tok), key, INT_MIN)
        key_ref[p] = key

    def count_fn(pred):
        c = jnp.zeros((n_tok, T), f32)
        for p in range(nt):
            c = c + jnp.where(pred(key_ref[p], p * T + lane), 1.0, 0.0)
        return jnp.sum(c, axis=1, keepdims=True)

    thr, tie_end = _select_threshold(count_fn, (n_tok, T), topk, idx_bits)

    for p in range(nt):
        key = key_ref[p]
        kidx = p * T + lane
        sel = (key > thr) | ((key == thr) & (kidx < tie_end))
        if p == n_pages:
            sel = sel & (lane <= tok) & (lane < n_tok)
        bias_ref[p] = jnp.where(sel, 0.0, NEG_BIG)

    for kvh in range(n_kv):
        cols = slice(kvh * T, (kvh + 1) * T)
        rws = slice(kvh * rows_kv, (kvh + 1) * rows_kv)
        qg = q_ref[0, rws, :]
        for p in range(nt):
            k_t = (k_pages[p][:, cols] if p < n_pages else kn_ref[0, :, cols]).astype(bf16)
            s_ref[p, rws, :] = _dot_nt(qg, k_t) + jnp.concatenate([bias_ref[p]] * (rows_kv // n_tok), axis=0)

    s_all = s_ref[...]
    m = jnp.max(jnp.max(s_all, axis=0), axis=1, keepdims=True)
    p_all = jnp.exp(s_all - m[None])
    l = jnp.sum(jnp.sum(p_all, axis=0), axis=1, keepdims=True)
    s_ref[...] = p_all
    for kvh in range(n_kv):
        cols = slice(kvh * T, (kvh + 1) * T)
        rws = slice(kvh * rows_kv, (kvh + 1) * rows_kv)
        acc = jnp.zeros((rows_kv, T), f32)
        for p in range(nt):
            v_t = (v_pages[p][:, cols] if p < n_pages else vn_ref[0, :, cols]).astype(bf16)
            acc = acc + _dot(s_ref[p, rws, :].astype(bf16), v_t)
        o_ref[0, rws, :] = (acc / l[rws]).astype(bf16)


def _dsa_sample(page_table, layer, q, iq, iwb, ikn, kn, vn, cache_idx, cache_k, cache_v, *, topk, n_kv, n_tok):
    nb, n_rows, T = q.shape
    n_pages = page_table.shape[1]
    kvw = cache_k.shape[3]
    idw = cache_idx.shape[3]
    seq = lambda shape: pl.BlockSpec((1,) + shape, lambda b, pt: (b, 0, 0))

    def page(width, p):
        return pl.BlockSpec((None, None, T, width), lambda b, pt, p=p: (layer, pt[b, p], 0, 0))

    in_specs = [seq((n_rows, T)), seq((n_rows, idw)), seq((n_rows, T)), seq((T, idw)), seq((T, kvw)), seq((T, kvw))]
    in_specs += [page(idw, p) for p in range(n_pages)]
    in_specs += [page(kvw, p) for p in range(n_pages)]
    in_specs += [page(kvw, p) for p in range(n_pages)]
    nt = n_pages + 1
    grid_spec = pltpu.PrefetchScalarGridSpec(
        num_scalar_prefetch=1,
        grid=(nb,),
        in_specs=in_specs,
        out_specs=seq((n_rows, T)),
        scratch_shapes=[pltpu.VMEM((nt, n_tok, T), i32), pltpu.VMEM((nt, n_tok, T), f32),
                        pltpu.VMEM((nt, n_rows, T), f32)],
    )
    return pl.pallas_call(
        functools.partial(_dsa_sample_body, n_pages=n_pages, topk=topk, n_kv=n_kv, n_tok=n_tok,
                          idx_bits=int(nt * T).bit_length()),
        grid_spec=grid_spec,
        out_shape=jax.ShapeDtypeStruct((nb, n_rows, T), bf16),
        compiler_params=_params("parallel"),
        name="dsa_sample",
    )(page_table, q, iq, iwb, ikn, kn, vn,
      *([cache_idx] * n_pages), *([cache_k] * n_pages), *([cache_v] * n_pages))


def _rope_tables(pos, dim):
    half = dim // 2
    inv = ROPE_THETA ** (-jnp.arange(half, dtype=f32) * (2.0 / dim))
    ang = pos.astype(f32)[:, None] * inv[None, :]
    cos = jnp.concatenate([jnp.cos(ang), jnp.cos(ang)], axis=1)
    sin = jnp.concatenate([-jnp.sin(ang), jnp.sin(ang)], axis=1)
    reps = LANES // dim
    return jnp.tile(cos, (1, reps)), jnp.tile(sin, (1, reps))


def _mix_tables(w_s, b_s, c, group_dim):
    g = w_s.shape[0]
    causal = jnp.tril(jnp.ones((c, c), dtype=bool))
    ws = jnp.where(causal[None], w_s[:, :c, :c], 0)
    reps = LANES // c
    ws = jnp.tile(ws, (1, reps, reps))
    blk = jnp.arange(LANES) // c
    ws = jnp.where((blk[:, None] == blk[None, :])[None], ws, 0).astype(bf16)
    bias = jnp.tile(jnp.repeat(b_s[:, :c].T, group_dim, axis=1), (reps, 1)).astype(f32)
    return ws, bias


def kernel(x_prompt, x_sample, cache_k, cache_v, cache_idx_k, state_ffn_conv, page_table, a_norm, a_w_in, a_v_norm, a_w_s, a_b_s, a_w_out, b_norm, b_w_in, b_q_norm, b_k_norm, b_w_o, f_norm, f_w_in, f_conv_w, f_conv_b, f_w_out):
    bsz, seq, d = x_prompt.shape
    nb, n_tok, _ = x_sample.shape
    depth = f_norm.shape[0]
    n_layers_b, n_pool, page, n_kv, hd = cache_k.shape
    idw = cache_idx_k.shape[3]
    past = page_table.shape[1] * page
    n_heads = b_w_o.shape[1] // hd
    n_ih = (b_w_in.shape[2] - (n_heads + 2 * n_kv) * hd - idw) // (idw + 1)
    ff = f_w_out.shape[1]
    a_groups, chunk = a_w_s.shape[1], a_w_s.shape[2]
    a_width = a_w_in.shape[2] // 2
    assert hd == LANES and idw == 64 and page == LANES and chunk == LANES and n_ih % 2 == 0
    assert f_conv_w.shape[1] == 3 and n_tok >= 2 and LANES % n_tok == 0 and n_tok & (n_tok - 1) == 0

    mp, ms = bsz * seq, nb * n_tok
    xp = x_prompt.reshape(mp, d)
    xs = x_sample.reshape(ms, d)
    tm_p = min(TM_ROWS, seq)
    tm_s = min(TM_ROWS, ms)

    ck = cache_k.reshape(n_layers_b, n_pool, page, n_kv * hd)
    cv = cache_v.reshape(n_layers_b, n_pool, page, n_kv * hd)

    outs = dict(k_p=[], v_p=[], ik_p=[], k_s=[], v_s=[], ik_s=[], chunk_v=[], conv_p=[], conv_s=[])
    n_mixers = 2
    for layer in range(depth):
        j = layer // n_mixers
        if layer % n_mixers == 0:
            w_in = a_w_in[j].astype(bf16)
            w_out = a_w_out[j].astype(bf16)
            new = []
            for x, tm, c, want in ((xp, tm_p, min(seq, chunk), False), (xs, tm_s, min(n_tok, chunk), True)):
                wmix, bias = _mix_tables(a_w_s[j], a_b_s[j], c, a_width // a_groups)
                z = _norm_matmul(x, a_norm[j], w_in, act="gelu", tm=tm, tn=512)
                gated, vn = _spatial_gate(z, a_v_norm[j], wmix, bias, tm=min(tm, 512), want_vn=want)
                new.append(_matmul_residual(gated, w_out, x, tm=tm, tn=d))
                if want:
                    outs["chunk_v"].append(vn.reshape(nb, n_tok, a_width))
            xp, xs = new
        else:
            n_real = b_w_in.shape[2]
            n_pad = -(-n_real // LANES) * LANES
            w_in = jnp.pad(b_w_in[j], ((0, 0), (0, n_pad - n_real))).astype(bf16)
            w_o = b_w_o[j].astype(bf16)
            tn_b = LANES * 11 if n_pad % (LANES * 11) == 0 else LANES
            pos_p = jnp.arange(seq, dtype=i32)
            pos_s = jnp.tile(past + jnp.arange(n_tok, dtype=i32), nb)
            res = []
            for x, tm, pos in ((xp, tm_p, pos_p), (xs, tm_s, pos_s)):
                p = _norm_matmul(x, b_norm[j], w_in, tm=tm, tn=tn_b)
                tabs = _rope_tables(pos, hd) + _rope_tables(pos, idw)
                res.append(_attn_post(p, b_q_norm[j], b_k_norm[j], tabs, tm=tm,
                                      n_heads=n_heads, n_kv=n_kv, n_ih=n_ih))
            q, k, v, kb, vb, iq, ikw, ik2 = res[0]
            r3 = lambda a: a.reshape(bsz, seq, a.shape[1])
            o = _dsa_prompt(r3(q), r3(iq), r3(ikw), r3(ik2), r3(kb), r3(vb),
                            topk=min(TOPK_MAX, seq // 4), n_kv=n_kv, n_ih=n_ih)
            xp = _matmul_residual(o.reshape(mp, n_heads * hd), w_o, xp, tm=tm_p, tn=d)
            n_seq_pages = seq // page
            outs["k_p"].append(k.reshape(bsz, n_seq_pages, page, n_kv, hd))
            outs["v_p"].append(v.reshape(bsz, n_seq_pages, page, n_kv, hd))
            outs["ik_p"].append(ikw[:, :idw].reshape(bsz, n_seq_pages, page, idw))

            q, k, v, kb, vb, iq, ikw, ik2 = res[1]
            hm = lambda a, w: a.reshape(nb, n_tok, a.shape[1] // w, w).transpose(0, 2, 1, 3).reshape(nb, -1, w)
            qs = hm(q, hd)
            iqs = hm(iq, idw)
            iw = ikw[:, idw:idw + n_ih].reshape(nb, n_tok, n_ih).transpose(0, 2, 1).reshape(nb, n_ih * n_tok, 1)
            iwb = jnp.broadcast_to(iw, (nb, n_ih * n_tok, LANES))
            padk = lambda a: jnp.pad(a.reshape(nb, n_tok, a.shape[1]), ((0, 0), (0, LANES - n_tok), (0, 0)))
            os_ = _dsa_sample(page_table, j, qs, iqs, iwb, padk(ikw[:, :idw]), padk(kb), padk(vb),
                              cache_idx_k, ck, cv, topk=min(TOPK_MAX, (past + n_tok) // 4), n_kv=n_kv, n_tok=n_tok)
            os_ = os_.reshape(nb, n_heads, n_tok, hd).transpose(0, 2, 1, 3).reshape(ms, n_heads * hd)
            xs = _matmul_residual(os_, w_o, xs, tm=tm_s, tn=d)
            outs["k_s"].append(k.reshape(nb, n_tok, n_kv, hd))
            outs["v_s"].append(v.reshape(nb, n_tok, n_kv, hd))
            outs["ik_s"].append(ikw[:, :idw].reshape(nb, n_tok, idw))

        fw_in = f_w_in[layer].astype(bf16)
        fw_out = f_w_out[layer].astype(bf16)
        tn_f = 512 if ff % 512 == 0 else LANES
        xp, tg, tu = _conv_ffn(xp, f_norm[layer], fw_in, f_conv_w[layer], f_conv_b[layer], fw_out,
                               tm=tm_p, tn=tn_f, seq_len=seq)
        tails = jnp.concatenate([tg, tu], axis=2).reshape(bsz, seq // tm_p, 8, 2 * ff)
        outs["conv_p"].append(tails[:, -1, 6:8, :])
        st = state_ffn_conv[layer]
        zeros = jnp.zeros((nb, n_tok - 1, 2 * ff), f32)
        s1 = jnp.concatenate([st[:, 1:2], zeros], axis=1).reshape(ms, 2 * ff)
        s2 = jnp.concatenate([st, zeros[:, 1:]], axis=1).reshape(ms, 2 * ff)
        xs, ag, au = _conv_ffn(xs, f_norm[layer], fw_in, f_conv_w[layer], f_conv_b[layer], fw_out,
                               tm=tm_s, tn=tn_f, seq_len=n_tok, state=(s1, s2))
        a_s = jnp.concatenate([ag, au], axis=1).reshape(nb, n_tok, 2 * ff)
        outs["conv_s"].append(a_s[:, n_tok - 2:, :])

    st = lambda key: jnp.stack(outs[key])
    return (xp.reshape(bsz, seq, d), xs.reshape(nb, n_tok, d), st("k_p"), st("v_p"), st("ik_p"),
            st("k_s"), st("v_s"), st("ik_s"), st("chunk_v"), st("conv_p"), st("conv_s"))
```

```python
import functools
import math

import jax
import jax.numpy as jnp
from jax import lax
from jax.experimental import pallas as pl
from jax.experimental.pallas import tpu as pltpu

EPS = 1e-6
ROPE_THETA = 10000.0
TOPK_MAX = 256
LANES = 128
NEG_BIG = -1e30
INT_MIN = -(2 ** 31)
KEY_NEG_INF = -2139095041
TIE_ALL = 2 ** 30
VMEM_LIMIT = 56 * 1024 * 1024
TM_ROWS = 512
TQ_PROMPT = 256

f32 = jnp.float32
bf16 = jnp.bfloat16
i32 = jnp.int32


def _params(*sem):
    return pltpu.CompilerParams(dimension_semantics=sem, vmem_limit_bytes=VMEM_LIMIT)


def _dot(a, b):
    return jnp.dot(a, b, preferred_element_type=f32)


def _dot_nt(a, b):
    return lax.dot_general(a, b, (((1,), (1,)), ((), ())), preferred_element_type=f32)


def _rms(x, g):
    r = lax.rsqrt(jnp.mean(x * x, axis=-1, keepdims=True) + EPS)
    return x * r * g


def _rows(a):
    return a.reshape(a.shape[0], 1, a.shape[1])


def _gelu_tanh(x):
    c = math.sqrt(2.0 / math.pi)
    return 0.5 * x * (1.0 + jnp.tanh(c * (x + 0.044715 * (x * x * x))))


def _nmm_body(x_ref, g_ref, w_ref, o_ref, h_ref, *, act):
    @pl.when(pl.program_id(1) == 0)
    def _():
        h_ref[...] = _rms(x_ref[...], g_ref[...]).astype(bf16)

    z = _dot(h_ref[...], w_ref[...])
    if act == "gelu":
        z = _gelu_tanh(z)
    o_ref[...] = z.astype(o_ref.dtype)


def _norm_matmul(x, gain, w, layer, *, act=None, tm, tn, out_dtype=f32):
    m, k = x.shape
    n = w.shape[2]
    return pl.pallas_call(
        functools.partial(_nmm_body, act=act),
        grid=(m // tm, n // tn),
        in_specs=[
            pl.BlockSpec((tm, k), lambda i, j: (i, 0)),
            pl.BlockSpec((None, 1, k), lambda i, j: (layer, 0, 0)),
            pl.BlockSpec((None, k, tn), lambda i, j: (layer, 0, j)),
        ],
        out_specs=pl.BlockSpec((tm, tn), lambda i, j: (i, j)),
        out_shape=jax.ShapeDtypeStruct((m, n), out_dtype),
        scratch_shapes=[pltpu.VMEM((tm, k), bf16)],
        compiler_params=_params("parallel", "arbitrary"),
        name="norm_matmul",
    )(x, _rows(gain), w)


def _mmres_body(a_ref, w_ref, x_ref, o_ref):
    o_ref[...] = x_ref[...] + _dot(a_ref[...], w_ref[...])


def _matmul_residual(a, w, layer, x, *, tm, tn):
    m, k = a.shape
    n = w.shape[2]
    return pl.pallas_call(
        _mmres_body,
        grid=(m // tm, n // tn),
        in_specs=[
            pl.BlockSpec((tm, k), lambda i, j: (i, 0)),
            pl.BlockSpec((None, k, tn), lambda i, j: (layer, 0, j)),
            pl.BlockSpec((tm, tn), lambda i, j: (i, j)),
        ],
        out_specs=pl.BlockSpec((tm, tn), lambda i, j: (i, j)),
        out_shape=jax.ShapeDtypeStruct((m, n), f32),
        compiler_params=_params("parallel", "arbitrary"),
        name="matmul_residual",
    )(a, w, x)


def _gate_body(u_ref, v_ref, vg_ref, wmix_ref, bias_ref, o_ref, *vn_out, tm, groups):
    vn = _rms(v_ref[...], vg_ref[...])
    if vn_out:
        vn_out[0][...] = vn
    vnb = vn.astype(bf16)
    gd = v_ref.shape[1] // groups
    for c in range(tm // LANES):
        rows = slice(c * LANES, (c + 1) * LANES)
        for g in range(groups):
            cols = slice(g * gd, (g + 1) * gd)
            s = _dot(wmix_ref[g], vnb[rows, cols]) + bias_ref[:, cols]
            o_ref[rows, cols] = (u_ref[rows, cols] * s).astype(bf16)


def _spatial_gate(z, v_gain, layer, wmix, bias, *, tm, want_vn):
    m, w2 = z.shape
    w = w2 // 2
    groups = wmix.shape[0]
    out_shape = [jax.ShapeDtypeStruct((m, w), bf16)]
    out_specs = [pl.BlockSpec((tm, w), lambda i: (i, 0))]
    if want_vn:
        out_shape.append(jax.ShapeDtypeStruct((m, w), f32))
        out_specs.append(pl.BlockSpec((tm, w), lambda i: (i, 0)))
    res = pl.pallas_call(
        functools.partial(_gate_body, tm=tm, groups=groups),
        grid=(m // tm,),
        in_specs=[
            pl.BlockSpec((tm, w), lambda i: (i, 0)),
            pl.BlockSpec((tm, w), lambda i: (i, 1)),
            pl.BlockSpec((None, 1, w), lambda i: (layer, 0, 0)),
            pl.BlockSpec(wmix.shape, lambda i: (0, 0, 0)),
            pl.BlockSpec(bias.shape, lambda i: (0, 0)),
        ],
        out_specs=out_specs,
        out_shape=out_shape,
        compiler_params=_params("parallel"),
        name="spatial_gate",
    )(z, z, _rows(v_gain), wmix, bias)
    return res if want_vn else (res[0], None)


def _conv3(a, cw, cb, prev1, prev2):
    return cb + cw[0:1] * prev2 + cw[1:2] * prev1 + cw[2:3] * a


def _ffn_body(x_ref, g_ref, wg_ref, wu_ref, cwg_ref, cwu_ref, cbg_ref, cbu_ref, wo_ref, *rest,
              sample, seq_tiles, seq_len):
    if sample:
        s1g_ref, s1u_ref, s2g_ref, s2u_ref, o_ref, ag_ref, au_ref, h_ref = rest
    else:
        o_ref, tg_ref, tu_ref, h_ref, carry_ref = rest
    i = pl.program_id(0)
    j = pl.program_id(1)

    @pl.when(j == 0)
    def _():
        h_ref[...] = _rms(x_ref[...], g_ref[...]).astype(bf16)

    h = h_ref[...]
    ag = _dot(h, wg_ref[...])
    au = _dot(h, wu_ref[...])
    tm, tn = ag.shape
    row = lax.broadcasted_iota(i32, (tm, tn), 0)
    if sample:
        t = row & (seq_len - 1)
        m1 = t == 0
        m2 = t < 2
        p1g, p1u, p2g, p2u = s1g_ref[...], s1u_ref[...], s2g_ref[...], s2u_ref[...]
        ag_ref[...] = ag
        au_ref[...] = au
    else:
        m1 = row == 0
        m2 = row < 2
        live = (i % seq_tiles) != 0
        cg = jnp.where(live, carry_ref[j, 0], 0.0)
        cu = jnp.where(live, carry_ref[j, 1], 0.0)
        p1g, p1u = cg[7:8], cu[7:8]
        p2g = jnp.where(m1, cg[6:7], cg[7:8])
        p2u = jnp.where(m1, cu[6:7], cu[7:8])
        carry_ref[j, 0] = ag[tm - 8:]
        carry_ref[j, 1] = au[tm - 8:]
        tg_ref[0] = ag[tm - 8:]
        tu_ref[0] = au[tm - 8:]
    a1g = jnp.where(m1, p1g, pltpu.roll(ag, 1, 0))
    a2g = jnp.where(m2, p2g, pltpu.roll(ag, 2, 0))
    a1u = jnp.where(m1, p1u, pltpu.roll(au, 1, 0))
    a2u = jnp.where(m2, p2u, pltpu.roll(au, 2, 0))
    cg_ = _conv3(ag, cwg_ref[...], cbg_ref[...], a1g, a2g)
    cu_ = _conv3(au, cwu_ref[...], cbu_ref[...], a1u, a2u)
    act = (cg_ * (1.0 / (1.0 + jnp.exp(-cg_))) * cu_).astype(bf16)
    contrib = _dot(act, wo_ref[...])

    @pl.when(j == 0)
    def _():
        o_ref[...] = x_ref[...] + contrib

    @pl.when(j > 0)
    def _():
        o_ref[...] += contrib


def _conv_ffn(x, gain, w_in, conv_w, conv_b, w_out, layer, *, tm, tn, seq_len, state=None):
    m, d = x.shape
    ff = w_out.shape[1]
    nj = ff // tn
    ni = m // tm
    sample = state is not None
    in_specs = [
        pl.BlockSpec((tm, d), lambda i, j: (i, 0)),
        pl.BlockSpec((None, 1, d), lambda i, j: (layer, 0, 0)),
        pl.BlockSpec((None, d, tn), lambda i, j: (layer, 0, j)),
        pl.BlockSpec((None, d, tn), lambda i, j: (layer, 0, nj + j)),
        pl.BlockSpec((None, 3, tn), lambda i, j: (layer, 0, j)),
        pl.BlockSpec((None, 3, tn), lambda i, j: (layer, 0, nj + j)),
        pl.BlockSpec((None, 1, tn), lambda i, j: (layer, 0, j)),
        pl.BlockSpec((None, 1, tn), lambda i, j: (layer, 0, nj + j)),
        pl.BlockSpec((None, tn, d), lambda i, j: (layer, j, 0)),
    ]
    conv_b = _rows(conv_b)
    args = [x, _rows(gain), w_in, w_in, conv_w, conv_w, conv_b, conv_b, w_out]
    out_specs = [pl.BlockSpec((tm, d), lambda i, j: (i, 0))]
    out_shape = [jax.ShapeDtypeStruct((m, d), f32)]
    scratch = [pltpu.VMEM((tm, d), bf16)]
    if sample:
        s1, s2 = state
        in_specs += [
            pl.BlockSpec((tm, tn), lambda i, j: (i, j)),
            pl.BlockSpec((tm, tn), lambda i, j: (i, nj + j)),
            pl.BlockSpec((tm, tn), lambda i, j: (i, j)),
            pl.BlockSpec((tm, tn), lambda i, j: (i, nj + j)),
        ]
        args += [s1, s1, s2, s2]
        out_specs += [pl.BlockSpec((tm, tn), lambda i, j: (i, j))] * 2
        out_shape += [jax.ShapeDtypeStruct((m, ff), f32)] * 2
        seq_tiles = 1
    else:
        out_specs += [pl.BlockSpec((1, 8, tn), lambda i, j: (i, 0, j))] * 2
        out_shape += [jax.ShapeDtypeStruct((ni, 8, ff), f32)] * 2
        scratch.append(pltpu.VMEM((nj, 2, 8, tn), f32))
        seq_tiles = seq_len // tm
    return pl.pallas_call(
        functools.partial(_ffn_body, sample=sample, seq_tiles=seq_tiles, seq_len=seq_len),
        grid=(ni, nj),
        in_specs=in_specs,
        out_specs=out_specs,
        out_shape=out_shape,
        scratch_shapes=scratch,
        compiler_params=_params("arbitrary", "arbitrary"),
        name="conv_ffn_sample" if sample else "conv_ffn_prompt",
    )(*args)


def _rope_full(x, c, s):
    return x * c + pltpu.roll(x, LANES // 2, 1) * s


def _rope_half(x, c, s, lo):
    partner = jnp.where(lo, pltpu.roll(x, 96, 1), pltpu.roll(x, 32, 1))
    return x * c + partner * s


def _post_body(p_ref, qg_ref, kg_ref, c1_ref, s1_ref, c2_ref, s2_ref,
               q_ref, k_ref, v_ref, kb_ref, vb_ref, iq_ref, ikw_ref, ik2_ref,
               *, n_heads, n_kv, n_ih, idx_scale, q_scale):
    hd = LANES
    c1, s1, c2, s2 = c1_ref[...], s1_ref[...], c2_ref[...], s2_ref[...]
    tm = p_ref.shape[0]
    lane = lax.broadcasted_iota(i32, (tm, LANES), 1)
    lo = (lane & 63) < 32
    off = 0
    for h in range(n_heads):
        x = p_ref[:, off + h * hd: off + (h + 1) * hd]
        q_ref[:, h * hd:(h + 1) * hd] = (_rope_full(_rms(x, qg_ref[...]), c1, s1) * q_scale).astype(bf16)
    off += n_heads * hd
    for h in range(n_kv):
        x = p_ref[:, off + h * hd: off + (h + 1) * hd]
        kr = _rope_full(_rms(x, kg_ref[...]), c1, s1)
        k_ref[:, h * hd:(h + 1) * hd] = kr
        kb_ref[:, h * hd:(h + 1) * hd] = kr.astype(bf16)
    off += n_kv * hd
    vv = p_ref[:, off: off + n_kv * hd]
    v_ref[...] = vv
    vb_ref[...] = vv.astype(bf16)
    off += n_kv * hd
    for h in range(n_ih // 2):
        x = p_ref[:, off + h * LANES: off + (h + 1) * LANES]
        iq_ref[:, h * LANES:(h + 1) * LANES] = _rope_half(x, c2, s2, lo).astype(bf16)
    off += (n_ih // 2) * LANES
    tail = p_ref[:, off: off + LANES]
    tr = _rope_half(tail, c2, s2, lo)
    ik = jnp.where(lane < 64, tr, 0.0)
    ik2_ref[...] = (ik + pltpu.roll(ik, 64, 1)).astype(bf16)
    ikw_ref[...] = jnp.where(lane < 64, tr, tail * idx_scale)


def _attn_post(p, q_gain, k_gain, layer, tabs, *, tm, n_heads, n_kv, n_ih):
    m, npad = p.shape
    c1, s1, c2, s2 = tabs
    nt = c1.shape[0] // tm
    hd = LANES
    tab_spec = pl.BlockSpec((tm, LANES), lambda i: (i % nt, 0))
    row = lambda w: pl.BlockSpec((tm, w), lambda i: (i, 0))
    gain_spec = pl.BlockSpec((None, 1, hd), lambda i: (layer, 0, 0))
    outs = [
        (n_heads * hd, bf16), (n_kv * hd, f32), (n_kv * hd, f32), (n_kv * hd, bf16), (n_kv * hd, bf16),
        (n_ih * 64, bf16), (LANES, f32), (LANES, bf16),
    ]
    return pl.pallas_call(
        functools.partial(_post_body, n_heads=n_heads, n_kv=n_kv, n_ih=n_ih,
                          idx_scale=float((n_ih * 64) ** -0.5), q_scale=float(hd ** -0.5)),
        grid=(m // tm,),
        in_specs=[row(npad), gain_spec, gain_spec, tab_spec, tab_spec, tab_spec, tab_spec],
        out_specs=[row(w) for w, _ in outs],
        out_shape=[jax.ShapeDtypeStruct((m, w), dt) for w, dt in outs],
        compiler_params=_params("parallel"),
        name="attn_post",
    )(p, _rows(q_gain), _rows(k_gain), c1, s1, c2, s2)


def _key_to_float(key):
    bits = key ^ ((key >> 31) & 0x7FFFFFFF)
    return lax.bitcast_convert_type(bits, f32)


def _select_threshold(count_fn, shape, topk, bits_per_round):
    kk = float(topk)
    n_cand = (1 << bits_per_round) - 1

    def round_body(it, res):
        step = lax.shift_left(jnp.int32(1), 32 - bits_per_round * (it + 1))
        cands = [res + step * (c + 1) for c in range(n_cand)]
        floats = [_key_to_float(c) for c in cands]
        cnts = count_fn(tuple((lambda sc, idx, cf=cf: sc >= cf) for cf in floats))
        for c, n in zip(cands, cnts):
            res = jnp.where(n >= kk, c, res)
        return res

    res = lax.fori_loop(0, 32 // bits_per_round, round_body, jnp.full(shape, INT_MIN, i32))
    return _key_to_float(jnp.maximum(res, KEY_NEG_INF))


def _tie_search(count_fn, thr, need, shape, idx_bits):
    def tie_body(it, end):
        cand = end + lax.shift_left(jnp.int32(1), idx_bits - 1 - it)
        (cnt,) = count_fn((lambda sc, idx: (sc == thr) & (idx < cand),))
        return jnp.where(cnt <= need, cand, end)

    return lax.fori_loop(0, idx_bits, tie_body, jnp.zeros(shape, i32))


def _select_topk(count_fn, tie_ref, shape, topk, idx_bits, bits_per_round):
    kk = float(topk)
    thr = _select_threshold(count_fn, shape, topk, bits_per_round)
    n_gt, n_ge = count_fn((lambda sc, idx: sc > thr, lambda sc, idx: sc >= thr))
    tie_ref[...] = jnp.full(shape, TIE_ALL, i32)
    crowded = jnp.where((n_ge > kk) & (thr > -jnp.inf), 1.0, 0.0)

    @pl.when(jnp.max(crowded) > 0.0)
    def _():
        tie_ref[...] = _tie_search(count_fn, thr, kk - n_gt, shape, idx_bits)

    return thr, tie_ref[...]


def _dsa_prompt_body(q_ref, iq_ref, iwt_ref, ik2_ref, kb_ref, vb_ref, o_ref,
                     sc_ref, iqm_ref, qg_ref, tie_ref, m_ref, l_ref, acc_ref,
                     *, topk, n_kv, group, n_ih, idx_bits, tq):
    T = LANES
    nch = tq // T
    strips = tq // 8
    qb = pl.program_id(1)
    nkt = qb + 1
    lane = lax.broadcasted_iota(i32, (tq, T), 1)

    for h in range(n_ih):
        pair = iq_ref[0, :, (h // 2) * T:(h // 2 + 1) * T].astype(f32)
        keep = (lane < 64) if h % 2 == 0 else (lane >= 64)
        iqm_ref[h] = jnp.where(keep, pair, 0.0).astype(bf16)
    for h in range(n_kv * group):
        qg_ref[h // group, (h % group) * tq:(h % group + 1) * tq, :] = q_ref[0, :, h * T:(h + 1) * T]

    def key_rows(kt):
        return pl.ds(pl.multiple_of(kt * tq, tq), tq)

    key_in_tile = (lax.broadcasted_iota(i32, (strips, 8, tq), 0) * 8
                   + lax.broadcasted_iota(i32, (strips, 8, tq), 1))
    qidx = qb * tq + lax.broadcasted_iota(i32, (strips, 8, tq), 2)

    def score_tile(kt, carry):
        ik_t = ik2_ref[0, key_rows(kt), :]
        acc = jnp.zeros((tq, tq), f32)
        for h in range(n_ih):
            acc = acc + jnp.maximum(_dot_nt(ik_t, iqm_ref[h]), 0.0) * iwt_ref[0, h:h + 1, :]
        admissible = kt * tq + key_in_tile <= qidx
        sc_ref[kt] = jnp.where(admissible, acc.reshape(strips, 8, tq), -jnp.inf).reshape(tq, tq)
        return carry

    lax.fori_loop(0, nkt, score_tile, 0)

    def count_fn(preds):
        def body(kt, accs):
            sc = sc_ref[kt].reshape(strips, 8, tq)
            kidx = kt * tq + key_in_tile
            return tuple(a + jnp.sum(jnp.where(pred(sc, kidx), 1.0, 0.0), axis=0)
                         for a, pred in zip(accs, preds))

        accs = lax.fori_loop(0, nkt, body, tuple(jnp.zeros((8, tq), f32) for _ in preds))
        return tuple(jnp.broadcast_to(jnp.sum(a, axis=0, keepdims=True), (8, tq)) for a in accs)

    thr, tie_end = _select_topk(count_fn, tie_ref, (8, tq), topk, idx_bits, 1)

    def bias_tile(kt, carry):
        sc = sc_ref[kt].reshape(strips, 8, tq)
        kidx = kt * tq + key_in_tile
        sel = ((sc > thr) | ((sc == thr) & (kidx < tie_end))) & (kidx <= qidx)
        sc_ref[kt] = jnp.where(sel, 0.0, NEG_BIG).reshape(tq, tq).T
        return carry

    lax.fori_loop(0, nkt, bias_tile, 0)

    gq = group * tq

    def scores(kt, kvh):
        k_t = kb_ref[0, key_rows(kt), kvh * T:(kvh + 1) * T]
        s = _dot_nt(qg_ref[kvh], k_t)
        return (s.reshape(group, tq, tq) + sc_ref[kt][None]).reshape(gq, tq)

    m_ref[...] = jnp.full(m_ref.shape, NEG_BIG, f32)

    def max_tile(kt, carry):
        for kvh in range(n_kv):
            s = scores(kt, kvh)
            mx = m_ref[kvh]
            for c in range(nch):
                mx = jnp.maximum(mx, s[:, c * T:(c + 1) * T])
            m_ref[kvh] = mx
        return carry

    lax.fori_loop(0, nkt, max_tile, 0)
    for kvh in range(n_kv):
        m_ref[kvh] = jnp.broadcast_to(jnp.max(m_ref[kvh], axis=1, keepdims=True), (gq, T))
    l_ref[...] = jnp.zeros(l_ref.shape, f32)
    acc_ref[...] = jnp.zeros(acc_ref.shape, f32)

    def pv_tile(kt, carry):
        for kvh in range(n_kv):
            s = scores(kt, kvh)
            mx = m_ref[kvh]
            ps = [jnp.exp(s[:, c * T:(c + 1) * T] - mx) for c in range(nch)]
            l_ref[kvh] += functools.reduce(lambda a, b: a + b, ps)
            p = jnp.concatenate(ps, axis=1).astype(bf16)
            acc_ref[kvh] += _dot(p, vb_ref[0, key_rows(kt), kvh * T:(kvh + 1) * T])
        return carry

    lax.fori_loop(0, nkt, pv_tile, 0)
    for kvh in range(n_kv):
        o = acc_ref[kvh] / jnp.sum(l_ref[kvh], axis=1, keepdims=True)
        for g in range(group):
            h = kvh * group + g
            o_ref[0, :, h * T:(h + 1) * T] = o[g * tq:(g + 1) * tq].astype(bf16)


def _dsa_prompt(q, iq, iwt, ik2, kb, vb, *, topk, n_kv, n_ih, tq):
    b, s, qd = q.shape
    T = LANES
    n_heads = qd // T
    group = n_heads // n_kv
    nq = s // tq
    blk = lambda w: pl.BlockSpec((1, tq, w), lambda bi, qi: (bi, qi, 0))
    full = lambda w: pl.BlockSpec((1, s, w), lambda bi, qi: (bi, 0, 0))
    att = pltpu.VMEM((n_kv, group * tq, T), f32)
    return pl.pallas_call(
        functools.partial(_dsa_prompt_body, topk=topk, n_kv=n_kv, group=group, n_ih=n_ih,
                          idx_bits=int(s).bit_length(), tq=tq),
        grid=(b, nq),
        in_specs=[blk(qd), blk(iq.shape[2]), pl.BlockSpec((1, n_ih, tq), lambda bi, qi: (bi, 0, qi)),
                  full(T), full(n_kv * T), full(n_kv * T)],
        out_specs=blk(qd),
        out_shape=jax.ShapeDtypeStruct((b, s, qd), bf16),
        scratch_shapes=[
            pltpu.VMEM((nq, tq, tq), f32),
            pltpu.VMEM((n_ih, tq, T), bf16),
            pltpu.VMEM((n_kv, group * tq, T), bf16), pltpu.VMEM((8, tq), i32),
            att, att, att,
        ],
        compiler_params=_params("parallel", "arbitrary"),
        name="dsa_prompt",
    )(q, iq, iwt, ik2, kb, vb)


def _dsa_sample_body(pt_ref, q_ref, iq_ref, iwb_ref, ikn_ref, kn_ref, vn_ref, hm_ref, ex_ref, *rest,
                     n_pages, topk, n_tok, idx_bits):
    del pt_ref
    T = LANES
    idx_pages = rest[:n_pages]
    k_pages = rest[n_pages:2 * n_pages]
    v_pages = rest[2 * n_pages:3 * n_pages]
    o_ref, sc_ref, s_ref, tie_ref = rest[3 * n_pages:]
    nt = n_pages + 1
    n_rows = q_ref.shape[1]
    lane = lax.broadcasted_iota(i32, (n_tok, T), 1)
    tok = lax.broadcasted_iota(i32, (n_tok, T), 0)
    fresh = (lane <= tok) & (lane < n_tok)

    iq = iq_ref[0]
    iwb = iwb_ref[0]
    for p in range(nt):
        ikt = (idx_pages[p][...] if p < n_pages else ikn_ref[0]).astype(bf16)
        w = jnp.maximum(_dot(iq, ikt), 0.0) * iwb
        sc = jnp.sum(w.reshape(n_rows // n_tok, n_tok, T), axis=0)
        if p == n_pages:
            sc = jnp.where(fresh, sc, -jnp.inf)
        sc_ref[p] = sc

    def count_fn(preds):
        accs = [jnp.zeros((n_tok, T), f32) for _ in preds]
        for p in range(nt):
            sc = sc_ref[p]
            for n, pred in enumerate(preds):
                accs[n] = accs[n] + jnp.where(pred(sc, p * T + lane), 1.0, 0.0)
        return tuple(jnp.broadcast_to(jnp.sum(a, axis=1, keepdims=True), (n_tok, T)) for a in accs)

    thr, tie_end = _select_topk(count_fn, tie_ref, (n_tok, T), topk, idx_bits, 4)

    q = q_ref[0]
    head_ok = hm_ref[...] > 0.5
    mx = jnp.full((n_rows, T), NEG_BIG, f32)
    for p in range(nt):
        sc = sc_ref[p]
        sel = (sc > thr) | ((sc == thr) & (p * T + lane < tie_end))
        if p == n_pages:
            sel = sel & fresh
        sel = jnp.concatenate([jnp.where(sel, 1.0, 0.0)] * (n_rows // n_tok), axis=0).astype(bf16)
        picked = _dot(sel, ex_ref[...]) > 0.5
        k_t = (k_pages[p][...] if p < n_pages else kn_ref[0]).astype(bf16)
        s = _dot_nt(q, k_t) + jnp.where(picked & head_ok, 0.0, NEG_BIG)
        s_ref[p] = s
        for c in range(s.shape[1] // T):
            mx = jnp.maximum(mx, s[:, c * T:(c + 1) * T])
    m = jnp.max(mx, axis=1, keepdims=True)
    l = jnp.zeros((n_rows, 1), f32)
    acc = jnp.zeros((n_rows, T), f32)
    for p in range(nt):
        e = jnp.exp(s_ref[p] - m)
        l = l + jnp.sum(e, axis=1, keepdims=True)
        v_t = (v_pages[p][...] if p < n_pages else vn_ref[0]).astype(bf16)
        acc = acc + _dot(e.astype(bf16), v_t)
    o_ref[0] = (acc / l).astype(bf16)


def _dsa_sample(page_table, layer, q, iq, iwb, ikn, kn, vn, head_ok, expand, cache_idx_t, cache_k, cache_v,
                *, topk, n_tok):
    nb, n_rows, T = q.shape
    n_pages = page_table.shape[1]
    kv_rows = cache_k.shape[2]
    idw = cache_idx_t.shape[2]
    seq = lambda shape: pl.BlockSpec((1,) + shape, lambda b, pt: (b, 0, 0))
    const = lambda a: pl.BlockSpec(a.shape, lambda b, pt: (0, 0))

    def page(shape, p):
        return pl.BlockSpec((None, None) + shape, lambda b, pt, p=p: (layer, pt[b, p], 0, 0))

    in_specs = [seq((n_rows, T)), seq((n_rows, idw)), seq((n_rows, T)), seq((idw, T)),
                seq((kv_rows, T)), seq((kv_rows, T)), const(head_ok), const(expand)]
    in_specs += [page((idw, T), p) for p in range(n_pages)]
    in_specs += [page((kv_rows, T), p) for p in range(n_pages)]
    in_specs += [page((kv_rows, T), p) for p in range(n_pages)]
    nt = n_pages + 1
    grid_spec = pltpu.PrefetchScalarGridSpec(
        num_scalar_prefetch=1,
        grid=(nb,),
        in_specs=in_specs,
        out_specs=seq((n_rows, T)),
        scratch_shapes=[pltpu.VMEM((nt, n_tok, T), f32), pltpu.VMEM((nt, n_rows, kv_rows), f32),
                        pltpu.VMEM((n_tok, T), i32)],
    )
    return pl.pallas_call(
        functools.partial(_dsa_sample_body, n_pages=n_pages, topk=topk, n_tok=n_tok,
                          idx_bits=int(nt * T).bit_length()),
        grid_spec=grid_spec,
        out_shape=jax.ShapeDtypeStruct((nb, n_rows, T), bf16),
        compiler_params=_params("parallel"),
        name="dsa_sample",
    )(page_table, q, iq, iwb, ikn, kn, vn, head_ok, expand,
      *([cache_idx_t] * n_pages), *([cache_k] * n_pages), *([cache_v] * n_pages))


def _rope_tables(pos, dim):
    half = dim // 2
    inv = ROPE_THETA ** (-jnp.arange(half, dtype=f32) * (2.0 / dim))
    ang = pos.astype(f32)[:, None] * inv[None, :]
    cos = jnp.concatenate([jnp.cos(ang), jnp.cos(ang)], axis=1)
    sin = jnp.concatenate([-jnp.sin(ang), jnp.sin(ang)], axis=1)
    reps = LANES // dim
    return jnp.tile(cos, (1, reps)), jnp.tile(sin, (1, reps))


def _mix_tables(w_s, b_s, c, group_dim):
    causal = jnp.tril(jnp.ones((c, c), dtype=bool))
    ws = jnp.where(causal[None], w_s[:, :c, :c], 0)
    reps = LANES // c
    ws = jnp.tile(ws, (1, reps, reps))
    blk = jnp.arange(LANES) // c
    ws = jnp.where((blk[:, None] == blk[None, :])[None], ws, 0).astype(bf16)
    bias = jnp.tile(jnp.repeat(b_s[:, :c].T, group_dim, axis=1), (reps, 1)).astype(f32)
    return ws, bias


def kernel(x_prompt, x_sample, cache_k, cache_v, cache_idx_k, state_ffn_conv, page_table, a_norm, a_w_in, a_v_norm, a_w_s, a_b_s, a_w_out, b_norm, b_w_in, b_q_norm, b_k_norm, b_w_o, f_norm, f_w_in, f_conv_w, f_conv_b, f_w_out):
    bsz, seq, d = x_prompt.shape
    nb, n_tok, _ = x_sample.shape
    depth = f_norm.shape[0]
    n_layers_b, n_pool, page, n_kv, hd = cache_k.shape
    idw = cache_idx_k.shape[3]
    past = page_table.shape[1] * page
    n_heads = b_w_o.shape[1] // hd
    group = n_heads // n_kv
    n_ih = (b_w_in.shape[2] - (n_heads + 2 * n_kv) * hd - idw) // (idw + 1)
    ff = f_w_out.shape[1]
    a_groups, chunk = a_w_s.shape[1], a_w_s.shape[2]
    a_width = a_w_in.shape[2] // 2
    assert hd == LANES and idw == 64 and page == LANES and chunk == LANES and n_ih % 2 == 0
    assert f_conv_w.shape[1] == 3 and n_tok >= 2 and LANES % n_tok == 0 and n_tok & (n_tok - 1) == 0

    mp, ms = bsz * seq, nb * n_tok
    xp = x_prompt.reshape(mp, d)
    xs = x_sample.reshape(ms, d)
    tm_p = min(TM_ROWS, seq)
    tm_s = min(TM_ROWS, ms)
    tq = min(TQ_PROMPT, seq)

    aw_in, aw_out = a_w_in.astype(bf16), a_w_out.astype(bf16)
    n_real = b_w_in.shape[2]
    n_pad = -(-n_real // LANES) * LANES
    bw_in = jnp.pad(b_w_in, ((0, 0), (0, 0), (0, n_pad - n_real))).astype(bf16)
    bw_o = b_w_o.astype(bf16)
    fw_in, fw_out = f_w_in.astype(bf16), f_w_out.astype(bf16)
    tn_b = LANES * 11 if n_pad % (LANES * 11) == 0 else LANES
    tn_f = 512 if ff % 512 == 0 else LANES

    ck = cache_k.reshape(n_layers_b, n_pool, page * n_kv, hd)
    cv = cache_v.reshape(n_layers_b, n_pool, page * n_kv, hd)
    cidx_t = jnp.swapaxes(cache_idx_k, 2, 3)
    row_kv = (jnp.arange(n_heads * n_tok) // n_tok) // group
    head_ok = (row_kv[:, None] == (jnp.arange(page * n_kv) % n_kv)[None, :]).astype(f32)
    expand = jnp.repeat(jnp.eye(page, dtype=bf16), n_kv, axis=1)

    outs = dict(k_p=[], v_p=[], ik_p=[], k_s=[], v_s=[], ik_s=[], chunk_v=[], conv_p=[], conv_s=[])
    n_mixers = 2
    for layer in range(depth):
        j = layer // n_mixers
        if layer % n_mixers == 0:
            new = []
            for x, tm, c, want in ((xp, tm_p, min(seq, chunk), False), (xs, tm_s, min(n_tok, chunk), True)):
                wmix, bias = _mix_tables(a_w_s[j], a_b_s[j], c, a_width // a_groups)
                z = _norm_matmul(x, a_norm, aw_in, j, act="gelu", tm=tm, tn=512)
                gated, vn = _spatial_gate(z, a_v_norm, j, wmix, bias, tm=min(tm, 512), want_vn=want)
                new.append(_matmul_residual(gated, aw_out, j, x, tm=tm, tn=d))
                if want:
                    outs["chunk_v"].append(vn.reshape(nb, n_tok, a_width))
            xp, xs = new
        else:
            pos_p = jnp.arange(seq, dtype=i32)
            pos_s = jnp.tile(past + jnp.arange(n_tok, dtype=i32), nb)
            res = []
            for x, tm, pos in ((xp, tm_p, pos_p), (xs, tm_s, pos_s)):
                p = _norm_matmul(x, b_norm, bw_in, j, tm=tm, tn=tn_b)
                tabs = _rope_tables(pos, hd) + _rope_tables(pos, idw)
                res.append(_attn_post(p, b_q_norm, b_k_norm, j, tabs, tm=tm,
                                      n_heads=n_heads, n_kv=n_kv, n_ih=n_ih))
            q, k, v, kb, vb, iq, ikw, ik2 = res[0]
            r3 = lambda a: a.reshape(bsz, seq, a.shape[1])
            iwt = jnp.swapaxes(ikw[:, idw:idw + n_ih].reshape(bsz, seq, n_ih), 1, 2)
            o = _dsa_prompt(r3(q), r3(iq), iwt, r3(ik2), r3(kb), r3(vb),
                            topk=min(TOPK_MAX, seq // 4), n_kv=n_kv, n_ih=n_ih, tq=tq)
            xp = _matmul_residual(o.reshape(mp, n_heads * hd), bw_o, j, xp, tm=tm_p, tn=d)
            n_seq_pages = seq // page
            outs["k_p"].append(k.reshape(bsz, n_seq_pages, page, n_kv, hd))
            outs["v_p"].append(v.reshape(bsz, n_seq_pages, page, n_kv, hd))
            outs["ik_p"].append(ikw[:, :idw].reshape(bsz, n_seq_pages, page, idw))

            q, k, v, kb, vb, iq, ikw, ik2 = res[1]
            hm = lambda a, w: a.reshape(nb, n_tok, a.shape[1] // w, w).transpose(0, 2, 1, 3).reshape(nb, -1, w)
            qs = hm(q, hd)
            iqs = hm(iq, idw)
            iw = ikw[:, idw:idw + n_ih].reshape(nb, n_tok, n_ih).transpose(0, 2, 1).reshape(nb, n_ih * n_tok, 1)
            iwb = jnp.broadcast_to(iw, (nb, n_ih * n_tok, LANES))
            pad_tok = lambda a: jnp.pad(a.reshape(nb, n_tok, -1), ((0, 0), (0, page - n_tok), (0, 0)))
            ikn = jnp.swapaxes(pad_tok(ikw[:, :idw]), 1, 2)
            kn = pad_tok(kb).reshape(nb, page * n_kv, hd)
            vn_ = pad_tok(vb).reshape(nb, page * n_kv, hd)
            os_ = _dsa_sample(page_table, j, qs, iqs, iwb, ikn, kn, vn_, head_ok, expand, cidx_t, ck, cv,
                              topk=min(TOPK_MAX, (past + n_tok) // 4), n_tok=n_tok)
            os_ = os_.reshape(nb, n_heads, n_tok, hd).transpose(0, 2, 1, 3).reshape(ms, n_heads * hd)
            xs = _matmul_residual(os_, bw_o, j, xs, tm=tm_s, tn=d)
            outs["k_s"].append(k.reshape(nb, n_tok, n_kv, hd))
            outs["v_s"].append(v.reshape(nb, n_tok, n_kv, hd))
            outs["ik_s"].append(ikw[:, :idw].reshape(nb, n_tok, idw))

        xp, tg, tu = _conv_ffn(xp, f_norm, fw_in, f_conv_w, f_conv_b, fw_out, layer,
                               tm=tm_p, tn=tn_f, seq_len=seq)
        tails = jnp.concatenate([tg, tu], axis=2).reshape(bsz, seq // tm_p, 8, 2 * ff)
        outs["conv_p"].append(tails[:, -1, 6:8, :])
        st = state_ffn_conv[layer]
        zeros = jnp.zeros((nb, n_tok - 1, 2 * ff), f32)
        s1 = jnp.concatenate([st[:, 1:2], zeros], axis=1).reshape(ms, 2 * ff)
        s2 = jnp.concatenate([st, zeros[:, 1:]], axis=1).reshape(ms, 2 * ff)
        xs, ag, au = _conv_ffn(xs, f_norm, fw_in, f_conv_w, f_conv_b, fw_out, layer,
                               tm=tm_s, tn=tn_f, seq_len=n_tok, state=(s1, s2))
        a_s = jnp.concatenate([ag, au], axis=1).reshape(nb, n_tok, 2 * ff)
        outs["conv_s"].append(a_s[:, n_tok - 2:, :])

    st = lambda key: jnp.stack(outs[key])
    return (xp.reshape(bsz, seq, d), xs.reshape(nb, n_tok, d), st("k_p"), st("v_p"), st("ik_p"),
            st("k_s"), st("v_s"), st("ik_s"), st("chunk_v"), st("conv_p"), st("conv_s"))
```

```python
import functools
import math

import jax
import jax.numpy as jnp
from jax import lax
from jax.experimental import pallas as pl
from jax.experimental.pallas import tpu as pltpu

EPS = 1e-6
ROPE_THETA = 10000.0
TOPK_MAX = 256
LANES = 128
NEG_BIG = -1e30
INT_MIN = -(2 ** 31)
KEY_NEG_INF = -2139095041
TIE_ALL = 2 ** 30
VMEM_LIMIT = 56 * 1024 * 1024
TM_ROWS = 512
TM_FFN = 1024
TQ_PROMPT = 256

f32 = jnp.float32
bf16 = jnp.bfloat16
i32 = jnp.int32


def _params(*sem):
    return pltpu.CompilerParams(dimension_semantics=sem, vmem_limit_bytes=VMEM_LIMIT)


def _dot(a, b):
    return jnp.dot(a, b, preferred_element_type=f32)


def _dot_nt(a, b):
    return lax.dot_general(a, b, (((1,), (1,)), ((), ())), preferred_element_type=f32)


def _rms(x, g):
    r = lax.rsqrt(jnp.mean(x * x, axis=-1, keepdims=True) + EPS)
    return x * r * g


def _rows(a):
    return a.reshape(a.shape[0], 1, a.shape[1])


def _gelu_tanh(x):
    c = math.sqrt(2.0 / math.pi)
    return 0.5 * x * (1.0 + jnp.tanh(c * (x + 0.044715 * (x * x * x))))


def _nmm_body(x_ref, g_ref, w_ref, o_ref, h_ref, *, act):
    @pl.when(pl.program_id(1) == 0)
    def _():
        h_ref[...] = _rms(x_ref[...], g_ref[...]).astype(bf16)

    z = _dot(h_ref[...], w_ref[...])
    if act == "gelu":
        z = _gelu_tanh(z)
    o_ref[...] = z.astype(o_ref.dtype)


def _norm_matmul(x, gain, w, layer, *, act=None, tm, tn, out_dtype=f32):
    m, k = x.shape
    n = w.shape[2]
    return pl.pallas_call(
        functools.partial(_nmm_body, act=act),
        grid=(m // tm, n // tn),
        in_specs=[
            pl.BlockSpec((tm, k), lambda i, j: (i, 0)),
            pl.BlockSpec((None, 1, k), lambda i, j: (layer, 0, 0)),
            pl.BlockSpec((None, k, tn), lambda i, j: (layer, 0, j)),
        ],
        out_specs=pl.BlockSpec((tm, tn), lambda i, j: (i, j)),
        out_shape=jax.ShapeDtypeStruct((m, n), out_dtype),
        scratch_shapes=[pltpu.VMEM((tm, k), bf16)],
        compiler_params=_params("parallel", "arbitrary"),
        name="norm_matmul",
    )(x, _rows(gain), w)


def _mmres_body(a_ref, w_ref, x_ref, o_ref):
    o_ref[...] = x_ref[...] + _dot(a_ref[...], w_ref[...])


def _matmul_residual(a, w, layer, x, *, tm, tn):
    m, k = a.shape
    n = w.shape[2]
    return pl.pallas_call(
        _mmres_body,
        grid=(m // tm, n // tn),
        in_specs=[
            pl.BlockSpec((tm, k), lambda i, j: (i, 0)),
            pl.BlockSpec((None, k, tn), lambda i, j: (layer, 0, j)),
            pl.BlockSpec((tm, tn), lambda i, j: (i, j)),
        ],
        out_specs=pl.BlockSpec((tm, tn), lambda i, j: (i, j)),
        out_shape=jax.ShapeDtypeStruct((m, n), f32),
        compiler_params=_params("parallel", "arbitrary"),
        name="matmul_residual",
    )(a, w, x)


def _gate_body(u_ref, v_ref, vg_ref, wmix_ref, bias_ref, o_ref, *vn_out, tm, groups):
    vn = _rms(v_ref[...], vg_ref[...])
    if vn_out:
        vn_out[0][...] = vn
    vnb = vn.astype(bf16)
    gd = v_ref.shape[1] // groups
    for c in range(tm // LANES):
        rows = slice(c * LANES, (c + 1) * LANES)
        for g in range(groups):
            cols = slice(g * gd, (g + 1) * gd)
            s = _dot(wmix_ref[g], vnb[rows, cols]) + bias_ref[:, cols]
            o_ref[rows, cols] = (u_ref[rows, cols] * s).astype(bf16)


def _spatial_gate(z, v_gain, layer, wmix, bias, *, tm, want_vn):
    m, w2 = z.shape
    w = w2 // 2
    groups = wmix.shape[0]
    out_shape = [jax.ShapeDtypeStruct((m, w), bf16)]
    out_specs = [pl.BlockSpec((tm, w), lambda i: (i, 0))]
    if want_vn:
        out_shape.append(jax.ShapeDtypeStruct((m, w), f32))
        out_specs.append(pl.BlockSpec((tm, w), lambda i: (i, 0)))
    res = pl.pallas_call(
        functools.partial(_gate_body, tm=tm, groups=groups),
        grid=(m // tm,),
        in_specs=[
            pl.BlockSpec((tm, w), lambda i: (i, 0)),
            pl.BlockSpec((tm, w), lambda i: (i, 1)),
            pl.BlockSpec((None, 1, w), lambda i: (layer, 0, 0)),
            pl.BlockSpec(wmix.shape, lambda i: (0, 0, 0)),
            pl.BlockSpec(bias.shape, lambda i: (0, 0)),
        ],
        out_specs=out_specs,
        out_shape=out_shape,
        compiler_params=_params("parallel"),
        name="spatial_gate",
    )(z, z, _rows(v_gain), wmix, bias)
    return res if want_vn else (res[0], None)


def _conv3(a, cw, cb, prev1, prev2):
    return cb + cw[0:1] * prev2 + cw[1:2] * prev1 + cw[2:3] * a


def _ffn_body(x_ref, g_ref, wg_ref, wu_ref, cwg_ref, cwu_ref, cbg_ref, cbu_ref, wo_ref, *rest,
              sample, seq_tiles, seq_len, nj):
    if sample:
        s1g_ref, s1u_ref, s2g_ref, s2u_ref, o_ref, ag_ref, au_ref, h_ref, act0_ref, act1_ref = rest
    else:
        o_ref, tg_ref, tu_ref, h_ref, act0_ref, act1_ref, carry_ref = rest
    acts = (act0_ref, act1_ref)
    i = pl.program_id(0)
    j = pl.program_id(1)

    @pl.when(j == 0)
    def _():
        x = x_ref[...]
        h_ref[...] = _rms(x, g_ref[...]).astype(bf16)
        o_ref[...] = x
        act1_ref[...] = jnp.zeros(act1_ref.shape, bf16)

    def down(src_ref):
        o_ref[...] += _dot(src_ref[...], wo_ref[...])

    def up_conv_gate(dst_ref):
        h = h_ref[...]
        ag = _dot(h, wg_ref[...])
        au = _dot(h, wu_ref[...])
        tm, tn = ag.shape
        row = lax.broadcasted_iota(i32, (tm, tn), 0)
        if sample:
            t = row & (seq_len - 1)
            m1 = t == 0
            m2 = t < 2
            p1g, p1u, p2g, p2u = s1g_ref[...], s1u_ref[...], s2g_ref[...], s2u_ref[...]
            ag_ref[...] = ag
            au_ref[...] = au
        else:
            m1 = row == 0
            m2 = row < 2
            live = (i % seq_tiles) != 0
            cg = jnp.where(live, carry_ref[j, 0], 0.0)
            cu = jnp.where(live, carry_ref[j, 1], 0.0)
            p1g, p1u = cg[7:8], cu[7:8]
            p2g = jnp.where(m1, cg[6:7], cg[7:8])
            p2u = jnp.where(m1, cu[6:7], cu[7:8])
            carry_ref[j, 0] = ag[tm - 8:]
            carry_ref[j, 1] = au[tm - 8:]
            tg_ref[0] = ag[tm - 8:]
            tu_ref[0] = au[tm - 8:]
        a1g = jnp.where(m1, p1g, pltpu.roll(ag, 1, 0))
        a2g = jnp.where(m2, p2g, pltpu.roll(ag, 2, 0))
        a1u = jnp.where(m1, p1u, pltpu.roll(au, 1, 0))
        a2u = jnp.where(m2, p2u, pltpu.roll(au, 2, 0))
        cg_ = _conv3(ag, cwg_ref[...], cbg_ref[...], a1g, a2g)
        cu_ = _conv3(au, cwu_ref[...], cbu_ref[...], a1u, a2u)
        dst_ref[...] = (cg_ * (1.0 / (1.0 + jnp.exp(-cg_))) * cu_).astype(bf16)

    for parity in (0, 1):
        @pl.when((j < nj) & ((j & 1) == parity))
        def _(parity=parity):
            down(acts[1 - parity])
            up_conv_gate(acts[parity])

    @pl.when(j == nj)
    def _():
        down(acts[(nj - 1) % 2])


def _conv_ffn(x, gain, w_in, conv_w, conv_b, w_out, layer, *, tm, tn, seq_len, state=None):
    m, d = x.shape
    ff = w_out.shape[1]
    nj = ff // tn
    ni = m // tm
    sample = state is not None
    up = lambda j: jnp.minimum(j, nj - 1)
    down = lambda j: jnp.maximum(j - 1, 0)
    in_specs = [
        pl.BlockSpec((tm, d), lambda i, j: (i, 0), pipeline_mode=pl.Buffered(1)),
        pl.BlockSpec((None, 1, d), lambda i, j: (layer, 0, 0)),
        pl.BlockSpec((None, d, tn), lambda i, j: (layer, 0, up(j))),
        pl.BlockSpec((None, d, tn), lambda i, j: (layer, 0, nj + up(j))),
        pl.BlockSpec((None, 3, tn), lambda i, j: (layer, 0, up(j))),
        pl.BlockSpec((None, 3, tn), lambda i, j: (layer, 0, nj + up(j))),
        pl.BlockSpec((None, 1, tn), lambda i, j: (layer, 0, up(j))),
        pl.BlockSpec((None, 1, tn), lambda i, j: (layer, 0, nj + up(j))),
        pl.BlockSpec((None, tn, d), lambda i, j: (layer, down(j), 0)),
    ]
    conv_b = _rows(conv_b)
    args = [x, _rows(gain), w_in, w_in, conv_w, conv_w, conv_b, conv_b, w_out]
    out_specs = [pl.BlockSpec((tm, d), lambda i, j: (i, 0))]
    out_shape = [jax.ShapeDtypeStruct((m, d), f32)]
    scratch = [pltpu.VMEM((tm, d), bf16), pltpu.VMEM((tm, tn), bf16), pltpu.VMEM((tm, tn), bf16)]
    if sample:
        s1, s2 = state
        in_specs += [
            pl.BlockSpec((tm, tn), lambda i, j: (i, up(j))),
            pl.BlockSpec((tm, tn), lambda i, j: (i, nj + up(j))),
            pl.BlockSpec((tm, tn), lambda i, j: (i, up(j))),
            pl.BlockSpec((tm, tn), lambda i, j: (i, nj + up(j))),
        ]
        args += [s1, s1, s2, s2]
        out_specs += [pl.BlockSpec((tm, tn), lambda i, j: (i, up(j)))] * 2
        out_shape += [jax.ShapeDtypeStruct((m, ff), f32)] * 2
        seq_tiles = 1
    else:
        out_specs += [pl.BlockSpec((1, 8, tn), lambda i, j: (i, 0, up(j)))] * 2
        out_shape += [jax.ShapeDtypeStruct((ni, 8, ff), f32)] * 2
        scratch.append(pltpu.VMEM((nj, 2, 8, tn), f32))
        seq_tiles = seq_len // tm
    return pl.pallas_call(
        functools.partial(_ffn_body, sample=sample, seq_tiles=seq_tiles, seq_len=seq_len, nj=nj),
        grid=(ni, nj + 1),
        in_specs=in_specs,
        out_specs=out_specs,
        out_shape=out_shape,
        scratch_shapes=scratch,
        compiler_params=_params("arbitrary", "arbitrary"),
        name="conv_ffn_sample" if sample else "conv_ffn_prompt",
    )(*args)


def _rope_full(x, c, s):
    return x * c + pltpu.roll(x, LANES // 2, 1) * s


def _rope_half(x, c, s, lo):
    partner = jnp.where(lo, pltpu.roll(x, 96, 1), pltpu.roll(x, 32, 1))
    return x * c + partner * s


def _post_body(p_ref, qg_ref, kg_ref, c1_ref, s1_ref, c2_ref, s2_ref,
               q_ref, k_ref, v_ref, kb_ref, vb_ref, iq_ref, ikw_ref, ik2_ref,
               *, n_heads, n_kv, n_ih, idx_scale, q_scale):
    hd = LANES
    c1, s1, c2, s2 = c1_ref[...], s1_ref[...], c2_ref[...], s2_ref[...]
    tm = p_ref.shape[0]
    lane = lax.broadcasted_iota(i32, (tm, LANES), 1)
    lo = (lane & 63) < 32
    off = 0
    for h in range(n_heads):
        x = p_ref[:, off + h * hd: off + (h + 1) * hd]
        q_ref[:, h * hd:(h + 1) * hd] = (_rope_full(_rms(x, qg_ref[...]), c1, s1) * q_scale).astype(bf16)
    off += n_heads * hd
    for h in range(n_kv):
        x = p_ref[:, off + h * hd: off + (h + 1) * hd]
        kr = _rope_full(_rms(x, kg_ref[...]), c1, s1)
        k_ref[:, h * hd:(h + 1) * hd] = kr
        kb_ref[:, h * hd:(h + 1) * hd] = kr.astype(bf16)
    off += n_kv * hd
    vv = p_ref[:, off: off + n_kv * hd]
    v_ref[...] = vv
    vb_ref[...] = vv.astype(bf16)
    off += n_kv * hd
    for h in range(n_ih // 2):
        x = p_ref[:, off + h * LANES: off + (h + 1) * LANES]
        iq_ref[:, h * LANES:(h + 1) * LANES] = _rope_half(x, c2, s2, lo).astype(bf16)
    off += (n_ih // 2) * LANES
    tail = p_ref[:, off: off + LANES]
    tr = _rope_half(tail, c2, s2, lo)
    ik = jnp.where(lane < 64, tr, 0.0)
    ik2_ref[...] = (ik + pltpu.roll(ik, 64, 1)).astype(bf16)
    ikw_ref[...] = jnp.where(lane < 64, tr, tail * idx_scale)


def _attn_post(p, q_gain, k_gain, layer, tabs, *, tm, n_heads, n_kv, n_ih):
    m, npad = p.shape
    c1, s1, c2, s2 = tabs
    nt = c1.shape[0] // tm
    hd = LANES
    tab_spec = pl.BlockSpec((tm, LANES), lambda i: (i % nt, 0))
    row = lambda w: pl.BlockSpec((tm, w), lambda i: (i, 0))
    gain_spec = pl.BlockSpec((None, 1, hd), lambda i: (layer, 0, 0))
    outs = [
        (n_heads * hd, bf16), (n_kv * hd, f32), (n_kv * hd, f32), (n_kv * hd, bf16), (n_kv * hd, bf16),
        (n_ih * 64, bf16), (LANES, f32), (LANES, bf16),
    ]
    return pl.pallas_call(
        functools.partial(_post_body, n_heads=n_heads, n_kv=n_kv, n_ih=n_ih,
                          idx_scale=float((n_ih * 64) ** -0.5), q_scale=float(hd ** -0.5)),
        grid=(m // tm,),
        in_specs=[row(npad), gain_spec, gain_spec, tab_spec, tab_spec, tab_spec, tab_spec],
        out_specs=[row(w) for w, _ in outs],
        out_shape=[jax.ShapeDtypeStruct((m, w), dt) for w, dt in outs],
        compiler_params=_params("parallel"),
        name="attn_post",
    )(p, _rows(q_gain), _rows(k_gain), c1, s1, c2, s2)


def _key_to_float(key):
    bits = key ^ ((key >> 31) & 0x7FFFFFFF)
    return lax.bitcast_convert_type(bits, f32)


def _select_threshold(count_fn, shape, topk, bits_per_round):
    kk = float(topk)
    n_cand = (1 << bits_per_round) - 1

    def round_body(it, res):
        step = lax.shift_left(jnp.int32(1), 32 - bits_per_round * (it + 1))
        cands = [res + step * (c + 1) for c in range(n_cand)]
        floats = [_key_to_float(c) for c in cands]
        cnts = count_fn(tuple((lambda sc, idx, cf=cf: sc >= cf) for cf in floats))
        for c, n in zip(cands, cnts):
            res = jnp.where(n >= kk, c, res)
        return res

    res = lax.fori_loop(0, 32 // bits_per_round, round_body, jnp.full(shape, INT_MIN, i32))
    return _key_to_float(jnp.maximum(res, KEY_NEG_INF))


def _tie_search(count_fn, thr, need, shape, idx_bits):
    def tie_body(it, end):
        cand = end + lax.shift_left(jnp.int32(1), idx_bits - 1 - it)
        (cnt,) = count_fn((lambda sc, idx: (sc == thr) & (idx < cand),))
        return jnp.where(cnt <= need, cand, end)

    return lax.fori_loop(0, idx_bits, tie_body, jnp.zeros(shape, i32))


def _select_topk(count_fn, tie_ref, shape, topk, idx_bits, bits_per_round):
    kk = float(topk)
    thr = _select_threshold(count_fn, shape, topk, bits_per_round)
    n_gt, n_ge = count_fn((lambda sc, idx: sc > thr, lambda sc, idx: sc >= thr))
    tie_ref[...] = jnp.full(shape, TIE_ALL, i32)
    crowded = jnp.where((n_ge > kk) & (thr > -jnp.inf), 1.0, 0.0)

    @pl.when(jnp.max(crowded) > 0.0)
    def _():
        tie_ref[...] = _tie_search(count_fn, thr, kk - n_gt, shape, idx_bits)

    return thr, tie_ref[...]


def _dsa_prompt_body(q_ref, iq_ref, iwt_ref, ik2_ref, kb_ref, vb_ref, o_ref,
                     sc_ref, iqm_ref, qg_ref, tie_ref, m_ref, l_ref, acc_ref,
                     *, topk, n_kv, group, n_ih, idx_bits, tq):
    T = LANES
    nch = tq // T
    strips = tq // 8
    qb = pl.program_id(1)
    nkt = qb + 1
    lane = lax.broadcasted_iota(i32, (tq, T), 1)

    for h in range(n_ih):
        pair = iq_ref[0, :, (h // 2) * T:(h // 2 + 1) * T].astype(f32)
        keep = (lane < 64) if h % 2 == 0 else (lane >= 64)
        iqm_ref[h] = jnp.where(keep, pair, 0.0).astype(bf16)
    for h in range(n_kv * group):
        qg_ref[h // group, (h % group) * tq:(h % group + 1) * tq, :] = q_ref[0, :, h * T:(h + 1) * T]

    def key_rows(kt):
        return pl.ds(pl.multiple_of(kt * tq, tq), tq)

    key_in_tile = (lax.broadcasted_iota(i32, (strips, 8, tq), 0) * 8
                   + lax.broadcasted_iota(i32, (strips, 8, tq), 1))
    qidx = qb * tq + lax.broadcasted_iota(i32, (strips, 8, tq), 2)

    def score_tile(kt, carry):
        ik_t = ik2_ref[0, key_rows(kt), :]
        acc = jnp.zeros((tq, tq), f32)
        for h in range(n_ih):
            acc = acc + jnp.maximum(_dot_nt(ik_t, iqm_ref[h]), 0.0) * iwt_ref[0, h:h + 1, :]
        admissible = kt * tq + key_in_tile <= qidx
        sc_ref[kt] = jnp.where(admissible, acc.reshape(strips, 8, tq), -jnp.inf).reshape(tq, tq)
        return carry

    lax.fori_loop(0, nkt, score_tile, 0)

    def count_fn(preds):
        def body(kt, accs):
            sc = sc_ref[kt].reshape(strips, 8, tq)
            kidx = kt * tq + key_in_tile
            return tuple(a + jnp.sum(jnp.where(pred(sc, kidx), 1.0, 0.0), axis=0)
                         for a, pred in zip(accs, preds))

        accs = lax.fori_loop(0, nkt, body, tuple(jnp.zeros((8, tq), f32) for _ in preds))
        return tuple(jnp.broadcast_to(jnp.sum(a, axis=0, keepdims=True), (8, tq)) for a in accs)

    thr, tie_end = _select_topk(count_fn, tie_ref, (8, tq), topk, idx_bits, 1)

    def bias_tile(kt, carry):
        sc = sc_ref[kt].reshape(strips, 8, tq)
        kidx = kt * tq + key_in_tile
        sel = ((sc > thr) | ((sc == thr) & (kidx < tie_end))) & (kidx <= qidx)
        sc_ref[kt] = jnp.where(sel, 0.0, NEG_BIG).reshape(tq, tq).T
        return carry

    lax.fori_loop(0, nkt, bias_tile, 0)

    gq = group * tq

    def scores(kt, kvh):
        k_t = kb_ref[0, key_rows(kt), kvh * T:(kvh + 1) * T]
        s = _dot_nt(qg_ref[kvh], k_t)
        return (s.reshape(group, tq, tq) + sc_ref[kt][None]).reshape(gq, tq)

    m_ref[...] = jnp.full(m_ref.shape, NEG_BIG, f32)

    def max_tile(kt, carry):
        for kvh in range(n_kv):
            s = scores(kt, kvh)
            mx = m_ref[kvh]
            for c in range(nch):
                mx = jnp.maximum(mx, s[:, c * T:(c + 1) * T])
            m_ref[kvh] = mx
        return carry

    lax.fori_loop(0, nkt, max_tile, 0)
    for kvh in range(n_kv):
        m_ref[kvh] = jnp.broadcast_to(jnp.max(m_ref[kvh], axis=1, keepdims=True), (gq, T))
    l_ref[...] = jnp.zeros(l_ref.shape, f32)
    acc_ref[...] = jnp.zeros(acc_ref.shape, f32)

    def pv_tile(kt, carry):
        for kvh in range(n_kv):
            s = scores(kt, kvh)
            mx = m_ref[kvh]
            ps = [jnp.exp(s[:, c * T:(c + 1) * T] - mx) for c in range(nch)]
            l_ref[kvh] += functools.reduce(lambda a, b: a + b, ps)
            p = jnp.concatenate(ps, axis=1).astype(bf16)
            acc_ref[kvh] += _dot(p, vb_ref[0, key_rows(kt), kvh * T:(kvh + 1) * T])
        return carry

    lax.fori_loop(0, nkt, pv_tile, 0)
    for kvh in range(n_kv):
        o = acc_ref[kvh] / jnp.sum(l_ref[kvh], axis=1, keepdims=True)
        for g in range(group):
            h = kvh * group + g
            o_ref[0, :, h * T:(h + 1) * T] = o[g * tq:(g + 1) * tq].astype(bf16)


def _dsa_prompt(q, iq, iwt, ik2, kb, vb, *, topk, n_kv, n_ih, tq):
    b, s, qd = q.shape
    T = LANES
    n_heads = qd // T
    group = n_heads // n_kv
    nq = s // tq
    blk = lambda w: pl.BlockSpec((1, tq, w), lambda bi, qi: (bi, qi, 0))
    full = lambda w: pl.BlockSpec((1, s, w), lambda bi, qi: (bi, 0, 0))
    att = pltpu.VMEM((n_kv, group * tq, T), f32)
    return pl.pallas_call(
        functools.partial(_dsa_prompt_body, topk=topk, n_kv=n_kv, group=group, n_ih=n_ih,
                          idx_bits=int(s).bit_length(), tq=tq),
        grid=(b, nq),
        in_specs=[blk(qd), blk(iq.shape[2]), pl.BlockSpec((1, n_ih, tq), lambda bi, qi: (bi, 0, qi)),
                  full(T), full(n_kv * T), full(n_kv * T)],
        out_specs=blk(qd),
        out_shape=jax.ShapeDtypeStruct((b, s, qd), bf16),
        scratch_shapes=[
            pltpu.VMEM((nq, tq, tq), f32),
            pltpu.VMEM((n_ih, tq, T), bf16),
            pltpu.VMEM((n_kv, group * tq, T), bf16), pltpu.VMEM((8, tq), i32),
            att, att, att,
        ],
        compiler_params=_params("parallel", "arbitrary"),
        name="dsa_prompt",
    )(q, iq, iwt, ik2, kb, vb)


def _dsa_sample_body(pt_ref, q_ref, iq_ref, iwb_ref, ikn_ref, kn_ref, vn_ref, hm_ref, ex_ref, *rest,
                     n_pages, topk, n_tok, idx_bits):
    del pt_ref
    T = LANES
    idx_pages = rest[:n_pages]
    k_pages = rest[n_pages:2 * n_pages]
    v_pages = rest[2 * n_pages:3 * n_pages]
    o_ref, sc_ref, s_ref, tie_ref = rest[3 * n_pages:]
    nt = n_pages + 1
    n_rows = q_ref.shape[1]
    lane = lax.broadcasted_iota(i32, (n_tok, T), 1)
    tok = lax.broadcasted_iota(i32, (n_tok, T), 0)
    fresh = (lane <= tok) & (lane < n_tok)

    iq = iq_ref[0]
    iwb = iwb_ref[0]
    for p in range(nt):
        ikt = (idx_pages[p][...] if p < n_pages else ikn_ref[0]).astype(bf16)
        w = jnp.maximum(_dot(iq, ikt), 0.0) * iwb
        sc = jnp.sum(w.reshape(n_rows // n_tok, n_tok, T), axis=0)
        if p == n_pages:
            sc = jnp.where(fresh, sc, -jnp.inf)
        sc_ref[p] = sc

    def count_fn(preds):
        accs = [jnp.zeros((n_tok, T), f32) for _ in preds]
        for p in range(nt):
            sc = sc_ref[p]
            for n, pred in enumerate(preds):
                accs[n] = accs[n] + jnp.where(pred(sc, p * T + lane), 1.0, 0.0)
        return tuple(jnp.broadcast_to(jnp.sum(a, axis=1, keepdims=True), (n_tok, T)) for a in accs)

    thr, tie_end = _select_topk(count_fn, tie_ref, (n_tok, T), topk, idx_bits, 4)

    q = q_ref[0]
    head_ok = hm_ref[...] > 0.5
    mx = jnp.full((n_rows, T), NEG_BIG, f32)
    for p in range(nt):
        sc = sc_ref[p]
        sel = (sc > thr) | ((sc == thr) & (p * T + lane < tie_end))
        if p == n_pages:
            sel = sel & fresh
        sel = jnp.concatenate([jnp.where(sel, 1.0, 0.0)] * (n_rows // n_tok), axis=0).astype(bf16)
        picked = _dot(sel, ex_ref[...]) > 0.5
        k_t = (k_pages[p][...] if p < n_pages else kn_ref[0]).astype(bf16)
        s = _dot_nt(q, k_t) + jnp.where(picked & head_ok, 0.0, NEG_BIG)
        s_ref[p] = s
        for c in range(s.shape[1] // T):
            mx = jnp.maximum(mx, s[:, c * T:(c + 1) * T])
    m = jnp.max(mx, axis=1, keepdims=True)
    l = jnp.zeros((n_rows, 1), f32)
    acc = jnp.zeros((n_rows, T), f32)
    for p in range(nt):
        e = jnp.exp(s_ref[p] - m)
        l = l + jnp.sum(e, axis=1, keepdims=True)
        v_t = (v_pages[p][...] if p < n_pages else vn_ref[0]).astype(bf16)
        acc = acc + _dot(e.astype(bf16), v_t)
    o_ref[0] = (acc / l).astype(bf16)


def _dsa_sample(page_table, layer, q, iq, iwb, ikn, kn, vn, head_ok, expand, cache_idx_t, cache_k, cache_v,
                *, topk, n_tok):
    nb, n_rows, T = q.shape
    n_pages = page_table.shape[1]
    kv_rows = cache_k.shape[2]
    idw = cache_idx_t.shape[2]
    seq = lambda shape: pl.BlockSpec((1,) + shape, lambda b, pt: (b, 0, 0))
    const = lambda a: pl.BlockSpec(a.shape, lambda b, pt: (0, 0))

    def page(shape, p):
        return pl.BlockSpec((None, None) + shape, lambda b, pt, p=p: (layer, pt[b, p], 0, 0))

    in_specs = [seq((n_rows, T)), seq((n_rows, idw)), seq((n_rows, T)), seq((idw, T)),
                seq((kv_rows, T)), seq((kv_rows, T)), const(head_ok), const(expand)]
    in_specs += [page((idw, T), p) for p in range(n_pages)]
    in_specs += [page((kv_rows, T), p) for p in range(n_pages)]
    in_specs += [page((kv_rows, T), p) for p in range(n_pages)]
    nt = n_pages + 1
    grid_spec = pltpu.PrefetchScalarGridSpec(
        num_scalar_prefetch=1,
        grid=(nb,),
        in_specs=in_specs,
        out_specs=seq((n_rows, T)),
        scratch_shapes=[pltpu.VMEM((nt, n_tok, T), f32), pltpu.VMEM((nt, n_rows, kv_rows), f32),
                        pltpu.VMEM((n_tok, T), i32)],
    )
    return pl.pallas_call(
        functools.partial(_dsa_sample_body, n_pages=n_pages, topk=topk, n_tok=n_tok,
                          idx_bits=int(nt * T).bit_length()),
        grid_spec=grid_spec,
        out_shape=jax.ShapeDtypeStruct((nb, n_rows, T), bf16),
        compiler_params=_params("parallel"),
        name="dsa_sample",
    )(page_table, q, iq, iwb, ikn, kn, vn, head_ok, expand,
      *([cache_idx_t] * n_pages), *([cache_k] * n_pages), *([cache_v] * n_pages))


def _rope_tables(pos, dim):
    half = dim // 2
    inv = ROPE_THETA ** (-jnp.arange(half, dtype=f32) * (2.0 / dim))
    ang = pos.astype(f32)[:, None] * inv[None, :]
    cos = jnp.concatenate([jnp.cos(ang), jnp.cos(ang)], axis=1)
    sin = jnp.concatenate([-jnp.sin(ang), jnp.sin(ang)], axis=1)
    reps = LANES // dim
    return jnp.tile(cos, (1, reps)), jnp.tile(sin, (1, reps))


def _mix_tables(w_s, b_s, c, group_dim):
    causal = jnp.tril(jnp.ones((c, c), dtype=bool))
    ws = jnp.where(causal[None], w_s[:, :c, :c], 0)
    reps = LANES // c
    ws = jnp.tile(ws, (1, reps, reps))
    blk = jnp.arange(LANES) // c
    ws = jnp.where((blk[:, None] == blk[None, :])[None], ws, 0).astype(bf16)
    bias = jnp.tile(jnp.repeat(b_s[:, :c].T, group_dim, axis=1), (reps, 1)).astype(f32)
    return ws, bias


def kernel(x_prompt, x_sample, cache_k, cache_v, cache_idx_k, state_ffn_conv, page_table, a_norm, a_w_in, a_v_norm, a_w_s, a_b_s, a_w_out, b_norm, b_w_in, b_q_norm, b_k_norm, b_w_o, f_norm, f_w_in, f_conv_w, f_conv_b, f_w_out):
    bsz, seq, d = x_prompt.shape
    nb, n_tok, _ = x_sample.shape
    depth = f_norm.shape[0]
    n_layers_b, n_pool, page, n_kv, hd = cache_k.shape
    idw = cache_idx_k.shape[3]
    past = page_table.shape[1] * page
    n_heads = b_w_o.shape[1] // hd
    group = n_heads // n_kv
    n_ih = (b_w_in.shape[2] - (n_heads + 2 * n_kv) * hd - idw) // (idw + 1)
    ff = f_w_out.shape[1]
    a_groups, chunk = a_w_s.shape[1], a_w_s.shape[2]
    a_width = a_w_in.shape[2] // 2
    assert hd == LANES and idw == 64 and page == LANES and chunk == LANES and n_ih % 2 == 0
    assert f_conv_w.shape[1] == 3 and n_tok >= 2 and LANES % n_tok == 0 and n_tok & (n_tok - 1) == 0

    mp, ms = bsz * seq, nb * n_tok
    xp = x_prompt.reshape(mp, d)
    xs = x_sample.reshape(ms, d)
    tm_p = min(TM_ROWS, seq)
    tm_s = min(TM_ROWS, ms)
    tq = min(TQ_PROMPT, seq)

    aw_in, aw_out = a_w_in.astype(bf16), a_w_out.astype(bf16)
    n_real = b_w_in.shape[2]
    n_pad = -(-n_real // LANES) * LANES
    bw_in = jnp.pad(b_w_in, ((0, 0), (0, 0), (0, n_pad - n_real))).astype(bf16)
    bw_o = b_w_o.astype(bf16)
    fw_in, fw_out = f_w_in.astype(bf16), f_w_out.astype(bf16)
    tn_b = LANES * 11 if n_pad % (LANES * 11) == 0 else LANES
    tn_f = 512 if ff % 512 == 0 else LANES

    ck = cache_k.reshape(n_layers_b, n_pool, page * n_kv, hd)
    cv = cache_v.reshape(n_layers_b, n_pool, page * n_kv, hd)
    cidx_t = jnp.swapaxes(cache_idx_k, 2, 3)
    row_kv = (jnp.arange(n_heads * n_tok) // n_tok) // group
    head_ok = (row_kv[:, None] == (jnp.arange(page * n_kv) % n_kv)[None, :]).astype(f32)
    expand = jnp.repeat(jnp.eye(page, dtype=bf16), n_kv, axis=1)

    outs = dict(k_p=[], v_p=[], ik_p=[], k_s=[], v_s=[], ik_s=[], chunk_v=[], conv_p=[], conv_s=[])
    n_mixers = 2
    for layer in range(depth):
        j = layer // n_mixers
        if layer % n_mixers == 0:
            new = []
            for x, tm, c, want in ((xp, tm_p, min(seq, chunk), False), (xs, tm_s, min(n_tok, chunk), True)):
                wmix, bias = _mix_tables(a_w_s[j], a_b_s[j], c, a_width // a_groups)
                z = _norm_matmul(x, a_norm, aw_in, j, act="gelu", tm=min(TM_FFN, x.shape[0]), tn=512)
                gated, vn = _spatial_gate(z, a_v_norm, j, wmix, bias, tm=min(tm, 512), want_vn=want)
                new.append(_matmul_residual(gated, aw_out, j, x, tm=tm, tn=d))
                if want:
                    outs["chunk_v"].append(vn.reshape(nb, n_tok, a_width))
            xp, xs = new
        else:
            pos_p = jnp.arange(seq, dtype=i32)
            pos_s = jnp.tile(past + jnp.arange(n_tok, dtype=i32), nb)
            res = []
            for x, tm, pos in ((xp, tm_p, pos_p), (xs, tm_s, pos_s)):
                p = _norm_matmul(x, b_norm, bw_in, j, tm=min(TM_FFN, x.shape[0]), tn=tn_b)
                tabs = _rope_tables(pos, hd) + _rope_tables(pos, idw)
                res.append(_attn_post(p, b_q_norm, b_k_norm, j, tabs, tm=tm,
                                      n_heads=n_heads, n_kv=n_kv, n_ih=n_ih))
            q, k, v, kb, vb, iq, ikw, ik2 = res[0]
            r3 = lambda a: a.reshape(bsz, seq, a.shape[1])
            iwt = jnp.swapaxes(ikw[:, idw:idw + n_ih].reshape(bsz, seq, n_ih), 1, 2)
            o = _dsa_prompt(r3(q), r3(iq), iwt, r3(ik2), r3(kb), r3(vb),
                            topk=min(TOPK_MAX, seq // 4), n_kv=n_kv, n_ih=n_ih, tq=tq)
            xp = _matmul_residual(o.reshape(mp, n_heads * hd), bw_o, j, xp, tm=tm_p, tn=d)
            n_seq_pages = seq // page
            outs["k_p"].append(k.reshape(bsz, n_seq_pages, page, n_kv, hd))
            outs["v_p"].append(v.reshape(bsz, n_seq_pages, page, n_kv, hd))
            outs["ik_p"].append(ikw[:, :idw].reshape(bsz, n_seq_pages, page, idw))

            q, k, v, kb, vb, iq, ikw, ik2 = res[1]
            hm = lambda a, w: a.reshape(nb, n_tok, a.shape[1] // w, w).transpose(0, 2, 1, 3).reshape(nb, -1, w)
            qs = hm(q, hd)
            iqs = hm(iq, idw)
            iw = ikw[:, idw:idw + n_ih].reshape(nb, n_tok, n_ih).transpose(0, 2, 1).reshape(nb, n_ih * n_tok, 1)
            iwb = jnp.broadcast_to(iw, (nb, n_ih * n_tok, LANES))
            pad_tok = lambda a: jnp.pad(a.reshape(nb, n_tok, -1), ((0, 0), (0, page - n_tok), (0, 0)))
            ikn = jnp.swapaxes(pad_tok(ikw[:, :idw]), 1, 2)
            kn = pad_tok(kb).reshape(nb, page * n_kv, hd)
            vn_ = pad_tok(vb).reshape(nb, page * n_kv, hd)
            os_ = _dsa_sample(page_table, j, qs, iqs, iwb, ikn, kn, vn_, head_ok, expand, cidx_t, ck, cv,
                              topk=min(TOPK_MAX, (past + n_tok) // 4), n_tok=n_tok)
            os_ = os_.reshape(nb, n_heads, n_tok, hd).transpose(0, 2, 1, 3).reshape(ms, n_heads * hd)
            xs = _matmul_residual(os_, bw_o, j, xs, tm=tm_s, tn=d)
            outs["k_s"].append(k.reshape(nb, n_tok, n_kv, hd))
            outs["v_s"].append(v.reshape(nb, n_tok, n_kv, hd))
            outs["ik_s"].append(ikw[:, :idw].reshape(nb, n_tok, idw))

        tm_f = min(TM_FFN, seq)
        xp, tg, tu = _conv_ffn(xp, f_norm, fw_in, f_conv_w, f_conv_b, fw_out, layer,
                               tm=tm_f, tn=tn_f, seq_len=seq)
        tails = jnp.concatenate([tg, tu], axis=2).reshape(bsz, seq // tm_f, 8, 2 * ff)
        outs["conv_p"].append(tails[:, -1, 6:8, :])
        st = state_ffn_conv[layer]
        zeros = jnp.zeros((nb, n_tok - 1, 2 * ff), f32)
        s1 = jnp.concatenate([st[:, 1:2], zeros], axis=1).reshape(ms, 2 * ff)
        s2 = jnp.concatenate([st, zeros[:, 1:]], axis=1).reshape(ms, 2 * ff)
        xs, ag, au = _conv_ffn(xs, f_norm, fw_in, f_conv_w, f_conv_b, fw_out, layer,
                               tm=tm_s, tn=tn_f, seq_len=n_tok, state=(s1, s2))
        a_s = jnp.concatenate([ag, au], axis=1).reshape(nb, n_tok, 2 * ff)
        outs["conv_s"].append(a_s[:, n_tok - 2:, :])

    st = lambda key: jnp.stack(outs[key])
    return (xp.reshape(bsz, seq, d), xs.reshape(nb, n_tok, d), st("k_p"), st("v_p"), st("ik_p"),
            st("k_s"), st("v_s"), st("ik_s"), st("chunk_v"), st("conv_p"), st("conv_s"))
```

```python
import functools
import math

import jax
import jax.numpy as jnp
from jax import lax
from jax.experimental import pallas as pl
from jax.experimental.pallas import tpu as pltpu

EPS = 1e-6
ROPE_THETA = 10000.0
TOPK_MAX = 256
LANES = 128
NEG_BIG = -1e30
INT_MIN = -(2 ** 31)
KEY_NEG_INF = -2139095041
TIE_ALL = 2 ** 30
SAFE_SHIFT_BOUND = 40.0
VMEM_LIMIT = 56 * 1024 * 1024
TM_ROWS = 512
TM_FFN = 1024
TQ_PROMPT = 256

f32 = jnp.float32
bf16 = jnp.bfloat16
i32 = jnp.int32


def _params(*sem):
    return pltpu.CompilerParams(dimension_semantics=sem, vmem_limit_bytes=VMEM_LIMIT)


def _dot(a, b):
    return jnp.dot(a, b, preferred_element_type=f32)


def _dot_nt(a, b):
    return lax.dot_general(a, b, (((1,), (1,)), ((), ())), preferred_element_type=f32)


def _rms(x, g):
    r = lax.rsqrt(jnp.mean(x * x, axis=-1, keepdims=True) + EPS)
    return x * r * g


def _rows(a):
    return a.reshape(a.shape[0], 1, a.shape[1])


def _gelu_tanh(x):
    c = math.sqrt(2.0 / math.pi)
    return 0.5 * x * (1.0 + jnp.tanh(c * (x + 0.044715 * (x * x * x))))


def _nmm_body(x_ref, g_ref, w_ref, o_ref, h_ref, *, act):
    @pl.when(pl.program_id(1) == 0)
    def _():
        h_ref[...] = _rms(x_ref[...], g_ref[...]).astype(bf16)

    z = _dot(h_ref[...], w_ref[...])
    if act == "gelu":
        z = _gelu_tanh(z)
    o_ref[...] = z.astype(o_ref.dtype)


def _norm_matmul(x, gain, w, layer, *, act=None, tm, tn, out_dtype=f32):
    m, k = x.shape
    n = w.shape[2]
    return pl.pallas_call(
        functools.partial(_nmm_body, act=act),
        grid=(m // tm, n // tn),
        in_specs=[
            pl.BlockSpec((tm, k), lambda i, j: (i, 0)),
            pl.BlockSpec((None, 1, k), lambda i, j: (layer, 0, 0)),
            pl.BlockSpec((None, k, tn), lambda i, j: (layer, 0, j)),
        ],
        out_specs=pl.BlockSpec((tm, tn), lambda i, j: (i, j)),
        out_shape=jax.ShapeDtypeStruct((m, n), out_dtype),
        scratch_shapes=[pltpu.VMEM((tm, k), bf16)],
        compiler_params=_params("parallel", "arbitrary"),
        name="norm_matmul",
    )(x, _rows(gain), w)


def _mmres_body(a_ref, w_ref, x_ref, o_ref):
    o_ref[...] = x_ref[...] + _dot(a_ref[...], w_ref[...])


def _matmul_residual(a, w, layer, x, *, tm, tn):
    m, k = a.shape
    n = w.shape[2]
    return pl.pallas_call(
        _mmres_body,
        grid=(m // tm, n // tn),
        in_specs=[
            pl.BlockSpec((tm, k), lambda i, j: (i, 0)),
            pl.BlockSpec((None, k, tn), lambda i, j: (layer, 0, j)),
            pl.BlockSpec((tm, tn), lambda i, j: (i, j)),
        ],
        out_specs=pl.BlockSpec((tm, tn), lambda i, j: (i, j)),
        out_shape=jax.ShapeDtypeStruct((m, n), f32),
        compiler_params=_params("parallel", "arbitrary"),
        name="matmul_residual",
    )(a, w, x)


def _gate_body(u_ref, v_ref, vg_ref, wmix_ref, bias_ref, o_ref, *vn_out, tm, groups):
    vn = _rms(v_ref[...], vg_ref[...])
    if vn_out:
        vn_out[0][...] = vn
    vnb = vn.astype(bf16)
    gd = v_ref.shape[1] // groups
    for c in range(tm // LANES):
        rows = slice(c * LANES, (c + 1) * LANES)
        for g in range(groups):
            cols = slice(g * gd, (g + 1) * gd)
            s = _dot(wmix_ref[g], vnb[rows, cols]) + bias_ref[:, cols]
            o_ref[rows, cols] = (u_ref[rows, cols] * s).astype(bf16)


def _spatial_gate(z, v_gain, layer, wmix, bias, *, tm, want_vn):
    m, w2 = z.shape
    w = w2 // 2
    groups = wmix.shape[0]
    out_shape = [jax.ShapeDtypeStruct((m, w), bf16)]
    out_specs = [pl.BlockSpec((tm, w), lambda i: (i, 0))]
    if want_vn:
        out_shape.append(jax.ShapeDtypeStruct((m, w), f32))
        out_specs.append(pl.BlockSpec((tm, w), lambda i: (i, 0)))
    res = pl.pallas_call(
        functools.partial(_gate_body, tm=tm, groups=groups),
        grid=(m // tm,),
        in_specs=[
            pl.BlockSpec((tm, w), lambda i: (i, 0)),
            pl.BlockSpec((tm, w), lambda i: (i, 1)),
            pl.BlockSpec((None, 1, w), lambda i: (layer, 0, 0)),
            pl.BlockSpec(wmix.shape, lambda i: (0, 0, 0)),
            pl.BlockSpec(bias.shape, lambda i: (0, 0)),
        ],
        out_specs=out_specs,
        out_shape=out_shape,
        compiler_params=_params("parallel"),
        name="spatial_gate",
    )(z, z, _rows(v_gain), wmix, bias)
    return res if want_vn else (res[0], None)


def _conv3(a, cw, cb, prev1, prev2):
    return cb + cw[0:1] * prev2 + cw[1:2] * prev1 + cw[2:3] * a


def _ffn_body(x_ref, g_ref, wg_ref, wu_ref, cwg_ref, cwu_ref, cbg_ref, cbu_ref, wo_ref, *rest,
              sample, seq_tiles, seq_len, nj):
    if sample:
        sg_ref, su_ref, o_ref, ag_ref, au_ref, h_ref, act0_ref, act1_ref = rest
    else:
        o_ref, tg_ref, tu_ref, h_ref, act0_ref, act1_ref, carry_ref = rest
    acts = (act0_ref, act1_ref)
    i = pl.program_id(0)
    j = pl.program_id(1)

    @pl.when(j == 0)
    def _():
        x = x_ref[...]
        h_ref[...] = _rms(x, g_ref[...]).astype(bf16)
        o_ref[...] = x
        act1_ref[...] = jnp.zeros(act1_ref.shape, bf16)

    def down(src_ref):
        o_ref[...] += _dot(src_ref[...], wo_ref[...])

    def up_conv_gate(dst_ref):
        h = h_ref[...]
        ag = _dot(h, wg_ref[...])
        au = _dot(h, wu_ref[...])
        tm, tn = ag.shape
        row = lax.broadcasted_iota(i32, (tm, tn), 0)
        if sample:
            t = row & (seq_len - 1)
            m1 = t == 0
            m2 = t < 2
            p2g, p2u = sg_ref[...], su_ref[...]
            p1g, p1u = pltpu.roll(p2g, tm - 1, 0), pltpu.roll(p2u, tm - 1, 0)
            for a, a_ref in ((ag, ag_ref), (au, au_ref)):
                last = pltpu.roll(a, tm - (seq_len - 2), 0).reshape(tm // seq_len, seq_len, tn)
                a_ref[...] = last[:, :2, :]
        else:
            m1 = row == 0
            m2 = row < 2
            live = (i % seq_tiles) != 0
            cg = jnp.where(live, carry_ref[j, 0], 0.0)
            cu = jnp.where(live, carry_ref[j, 1], 0.0)
            p1g, p1u = cg[7:8], cu[7:8]
            p2g = jnp.where(m1, cg[6:7], cg[7:8])
            p2u = jnp.where(m1, cu[6:7], cu[7:8])
            carry_ref[j, 0] = ag[tm - 8:]
            carry_ref[j, 1] = au[tm - 8:]
            tg_ref[0] = ag[tm - 8:]
            tu_ref[0] = au[tm - 8:]
        a1g = jnp.where(m1, p1g, pltpu.roll(ag, 1, 0))
        a2g = jnp.where(m2, p2g, pltpu.roll(ag, 2, 0))
        a1u = jnp.where(m1, p1u, pltpu.roll(au, 1, 0))
        a2u = jnp.where(m2, p2u, pltpu.roll(au, 2, 0))
        cg_ = _conv3(ag, cwg_ref[...], cbg_ref[...], a1g, a2g)
        cu_ = _conv3(au, cwu_ref[...], cbu_ref[...], a1u, a2u)
        dst_ref[...] = (cg_ * (1.0 / (1.0 + jnp.exp(-cg_))) * cu_).astype(bf16)

    for parity in (0, 1):
        @pl.when((j < nj) & ((j & 1) == parity))
        def _(parity=parity):
            down(acts[1 - parity])
            up_conv_gate(acts[parity])

    @pl.when(j == nj)
    def _():
        down(acts[(nj - 1) % 2])


def _conv_ffn(x, gain, w_in, conv_w, conv_b, w_out, layer, *, tm, tn, seq_len, state=None):
    m, d = x.shape
    ff = w_out.shape[1]
    nj = ff // tn
    ni = m // tm
    sample = state is not None
    up = lambda j: jnp.minimum(j, nj - 1)
    down = lambda j: jnp.maximum(j - 1, 0)
    in_specs = [
        pl.BlockSpec((tm, d), lambda i, j: (i, 0), pipeline_mode=pl.Buffered(1)),
        pl.BlockSpec((None, 1, d), lambda i, j: (layer, 0, 0)),
        pl.BlockSpec((None, d, tn), lambda i, j: (layer, 0, up(j))),
        pl.BlockSpec((None, d, tn), lambda i, j: (layer, 0, nj + up(j))),
        pl.BlockSpec((None, 3, tn), lambda i, j: (layer, 0, up(j))),
        pl.BlockSpec((None, 3, tn), lambda i, j: (layer, 0, nj + up(j))),
        pl.BlockSpec((None, 1, tn), lambda i, j: (layer, 0, up(j))),
        pl.BlockSpec((None, 1, tn), lambda i, j: (layer, 0, nj + up(j))),
        pl.BlockSpec((None, tn, d), lambda i, j: (layer, down(j), 0)),
    ]
    conv_b = _rows(conv_b)
    args = [x, _rows(gain), w_in, w_in, conv_w, conv_w, conv_b, conv_b, w_out]
    out_specs = [pl.BlockSpec((tm, d), lambda i, j: (i, 0))]
    out_shape = [jax.ShapeDtypeStruct((m, d), f32)]
    scratch = [pltpu.VMEM((tm, d), bf16), pltpu.VMEM((tm, tn), bf16), pltpu.VMEM((tm, tn), bf16)]
    if sample:
        in_specs += [
            pl.BlockSpec((tm, tn), lambda i, j: (i, up(j))),
            pl.BlockSpec((tm, tn), lambda i, j: (i, nj + up(j))),
        ]
        args += [state, state]
        out_specs += [pl.BlockSpec((tm // seq_len, 2, tn), lambda i, j: (i, 0, up(j)))] * 2
        out_shape += [jax.ShapeDtypeStruct((m // seq_len, 2, ff), f32)] * 2
        seq_tiles = 1
    else:
        out_specs += [pl.BlockSpec((1, 8, tn), lambda i, j: (i, 0, up(j)))] * 2
        out_shape += [jax.ShapeDtypeStruct((ni, 8, ff), f32)] * 2
        scratch.append(pltpu.VMEM((nj, 2, 8, tn), f32))
        seq_tiles = seq_len // tm
    return pl.pallas_call(
        functools.partial(_ffn_body, sample=sample, seq_tiles=seq_tiles, seq_len=seq_len, nj=nj),
        grid=(ni, nj + 1),
        in_specs=in_specs,
        out_specs=out_specs,
        out_shape=out_shape,
        scratch_shapes=scratch,
        compiler_params=_params("arbitrary", "arbitrary"),
        name="conv_ffn_sample" if sample else "conv_ffn_prompt",
    )(*args)


def _rope_full(x, c, s):
    return x * c + pltpu.roll(x, LANES // 2, 1) * s


def _rope_half(x, c, s, lo):
    partner = jnp.where(lo, pltpu.roll(x, 96, 1), pltpu.roll(x, 32, 1))
    return x * c + partner * s


def _post_body(p_ref, qg_ref, kg_ref, c1_ref, s1_ref, c2_ref, s2_ref,
               q_ref, k_ref, v_ref, kb_ref, vb_ref, iq_ref, ikw_ref, ik2_ref,
               *, n_heads, n_kv, n_ih, idx_scale, q_scale):
    hd = LANES
    c1, s1, c2, s2 = c1_ref[...], s1_ref[...], c2_ref[...], s2_ref[...]
    tm = p_ref.shape[0]
    lane = lax.broadcasted_iota(i32, (tm, LANES), 1)
    lo = (lane & 63) < 32
    off = 0
    for h in range(n_heads):
        x = p_ref[:, off + h * hd: off + (h + 1) * hd]
        q_ref[:, h * hd:(h + 1) * hd] = (_rope_full(_rms(x, qg_ref[...]), c1, s1) * q_scale).astype(bf16)
    off += n_heads * hd
    for h in range(n_kv):
        x = p_ref[:, off + h * hd: off + (h + 1) * hd]
        kr = _rope_full(_rms(x, kg_ref[...]), c1, s1)
        k_ref[pl.ds(h, tm, stride=n_kv), :] = kr
        kb_ref[:, h * hd:(h + 1) * hd] = kr.astype(bf16)
    off += n_kv * hd
    vv = p_ref[:, off: off + n_kv * hd]
    for h in range(n_kv):
        v_ref[pl.ds(h, tm, stride=n_kv), :] = vv[:, h * hd:(h + 1) * hd]
    vb_ref[...] = vv.astype(bf16)
    off += n_kv * hd
    for h in range(n_ih // 2):
        x = p_ref[:, off + h * LANES: off + (h + 1) * LANES]
        iq_ref[:, h * LANES:(h + 1) * LANES] = _rope_half(x, c2, s2, lo).astype(bf16)
    off += (n_ih // 2) * LANES
    tail = p_ref[:, off: off + LANES]
    tr = _rope_half(tail, c2, s2, lo)
    ik = jnp.where(lane < 64, tr, 0.0)
    ik2_ref[...] = (ik + pltpu.roll(ik, 64, 1)).astype(bf16)
    ikw_ref[...] = jnp.where(lane < 64, tr, tail * idx_scale)


def _attn_post(p, q_gain, k_gain, layer, tabs, *, tm, n_heads, n_kv, n_ih):
    m, npad = p.shape
    c1, s1, c2, s2 = tabs
    nt = c1.shape[0] // tm
    hd = LANES
    tab_spec = pl.BlockSpec((tm, LANES), lambda i: (i % nt, 0))
    row = lambda w: pl.BlockSpec((tm, w), lambda i: (i, 0))
    gain_spec = pl.BlockSpec((None, 1, hd), lambda i: (layer, 0, 0))
    outs = [
        (1, n_heads * hd, bf16), (n_kv, hd, f32), (n_kv, hd, f32), (1, n_kv * hd, bf16), (1, n_kv * hd, bf16),
        (1, n_ih * 64, bf16), (1, LANES, f32), (1, LANES, bf16),
    ]
    return pl.pallas_call(
        functools.partial(_post_body, n_heads=n_heads, n_kv=n_kv, n_ih=n_ih,
                          idx_scale=float((n_ih * 64) ** -0.5), q_scale=float(hd ** -0.5)),
        grid=(m // tm,),
        in_specs=[row(npad), gain_spec, gain_spec, tab_spec, tab_spec, tab_spec, tab_spec],
        out_specs=[pl.BlockSpec((tm * r, w), lambda i: (i, 0)) for r, w, _ in outs],
        out_shape=[jax.ShapeDtypeStruct((m * r, w), dt) for r, w, dt in outs],
        compiler_params=_params("parallel"),
        name="attn_post",
    )(p, _rows(q_gain), _rows(k_gain), c1, s1, c2, s2)


def _key_to_float(key):
    bits = key ^ ((key >> 31) & 0x7FFFFFFF)
    return lax.bitcast_convert_type(bits, f32)


def _select_threshold(count_fn, shape, topk, bits_per_round):
    kk = float(topk)
    n_cand = (1 << bits_per_round) - 1

    def round_body(it, res):
        step = lax.shift_left(jnp.int32(1), 32 - bits_per_round * (it + 1))
        cands = [res + step * (c + 1) for c in range(n_cand)]
        floats = [_key_to_float(c) for c in cands]
        cnts = count_fn(tuple((lambda sc, idx, cf=cf: sc >= cf) for cf in floats))
        for c, n in zip(cands, cnts):
            res = jnp.where(n >= kk, c, res)
        return res

    res = lax.fori_loop(0, 32 // bits_per_round, round_body, jnp.full(shape, INT_MIN, i32))
    return _key_to_float(jnp.maximum(res, KEY_NEG_INF))


def _tie_search(count_fn, thr, need, shape, idx_bits):
    def tie_body(it, end):
        cand = end + lax.shift_left(jnp.int32(1), idx_bits - 1 - it)
        (cnt,) = count_fn((lambda sc, idx: (sc == thr) & (idx < cand),))
        return jnp.where(cnt <= need, cand, end)

    return lax.fori_loop(0, idx_bits, tie_body, jnp.zeros(shape, i32))


def _select_topk(count_fn, tie_ref, shape, topk, idx_bits, bits_per_round):
    kk = float(topk)
    thr = _select_threshold(count_fn, shape, topk, bits_per_round)
    n_gt, n_ge = count_fn((lambda sc, idx: sc > thr, lambda sc, idx: sc >= thr))
    tie_ref[...] = jnp.full(shape, TIE_ALL, i32)
    crowded = jnp.where((n_ge > kk) & (thr > -jnp.inf), 1.0, 0.0)

    @pl.when(jnp.max(crowded) > 0.0)
    def _():
        tie_ref[...] = _tie_search(count_fn, thr, kk - n_gt, shape, idx_bits)

    return thr, tie_ref[...]


def _dsa_prompt_body(q_ref, iq_ref, iwt_ref, ik2_ref, kb_ref, vb_ref, bound_ref, o_ref,
                     sc_ref, iqm_ref, qg_ref, tie_ref, m_ref, l_ref, acc_ref,
                     *, topk, n_kv, group, n_ih, idx_bits, tq):
    T = LANES
    nch = tq // T
    strips = tq // 8
    qb = pl.program_id(1)
    nkt = qb + 1
    lane = lax.broadcasted_iota(i32, (tq, T), 1)

    for h in range(n_ih):
        pair = iq_ref[0, :, (h // 2) * T:(h // 2 + 1) * T].astype(f32)
        keep = (lane < 64) if h % 2 == 0 else (lane >= 64)
        iqm_ref[h] = jnp.where(keep, pair, 0.0).astype(bf16)
    for h in range(n_kv * group):
        qg_ref[h // group, (h % group) * tq:(h % group + 1) * tq, :] = q_ref[0, :, h * T:(h + 1) * T]

    def key_rows(kt):
        return pl.ds(pl.multiple_of(kt * tq, tq), tq)

    key_in_tile = (lax.broadcasted_iota(i32, (strips, 8, tq), 0) * 8
                   + lax.broadcasted_iota(i32, (strips, 8, tq), 1))
    qidx = qb * tq + lax.broadcasted_iota(i32, (strips, 8, tq), 2)

    def score_tile(kt, carry):
        ik_t = ik2_ref[0, key_rows(kt), :]
        acc = jnp.zeros((tq, tq), f32)
        for h in range(n_ih):
            acc = acc + jnp.maximum(_dot_nt(ik_t, iqm_ref[h]), 0.0) * iwt_ref[0, h:h + 1, :]
        admissible = kt * tq + key_in_tile <= qidx
        sc_ref[kt] = jnp.where(admissible, acc.reshape(strips, 8, tq), -jnp.inf).reshape(tq, tq)
        return carry

    lax.fori_loop(0, nkt, score_tile, 0)

    def count_fn(preds):
        def body(kt, accs):
            sc = sc_ref[kt].reshape(strips, 8, tq)
            kidx = kt * tq + key_in_tile
            return tuple(a + jnp.sum(jnp.where(pred(sc, kidx), 1.0, 0.0), axis=0)
                         for a, pred in zip(accs, preds))

        accs = lax.fori_loop(0, nkt, body, tuple(jnp.zeros((8, tq), f32) for _ in preds))
        return tuple(jnp.broadcast_to(jnp.sum(a, axis=0, keepdims=True), (8, tq)) for a in accs)

    thr, tie_end = _select_topk(count_fn, tie_ref, (8, tq), topk, idx_bits, 1)

    def bias_tile(kt, carry):
        sc = sc_ref[kt].reshape(strips, 8, tq)
        kidx = kt * tq + key_in_tile
        sel = ((sc > thr) | ((sc == thr) & (kidx < tie_end))) & (kidx <= qidx)
        sc_ref[kt] = jnp.where(sel, 0.0, NEG_BIG).reshape(tq, tq).T
        return carry

    lax.fori_loop(0, nkt, bias_tile, 0)

    gq = group * tq

    def scores(kt, kvh):
        k_t = kb_ref[0, key_rows(kt), kvh * T:(kvh + 1) * T]
        s = _dot_nt(qg_ref[kvh], k_t)
        return (s.reshape(group, tq, tq) + sc_ref[kt][None]).reshape(gq, tq)

    bound = bound_ref[...]
    m_ref[...] = jnp.broadcast_to(bound[None], m_ref.shape)

    @pl.when(jnp.max(bound) > SAFE_SHIFT_BOUND)
    def _():
        m_ref[...] = jnp.full(m_ref.shape, NEG_BIG, f32)

        def max_tile(kt, carry):
            for kvh in range(n_kv):
                s = scores(kt, kvh)
                mx = m_ref[kvh]
                for c in range(nch):
                    mx = jnp.maximum(mx, s[:, c * T:(c + 1) * T])
                m_ref[kvh] = mx
            return carry

        lax.fori_loop(0, nkt, max_tile, 0)
        for kvh in range(n_kv):
            m_ref[kvh] = jnp.broadcast_to(jnp.max(m_ref[kvh], axis=1, keepdims=True), (gq, T))

    l_ref[...] = jnp.zeros(l_ref.shape, f32)
    acc_ref[...] = jnp.zeros(acc_ref.shape, f32)

    def pv_tile(kt, carry):
        for kvh in range(n_kv):
            s = scores(kt, kvh)
            mx = m_ref[kvh]
            ps = [jnp.exp(s[:, c * T:(c + 1) * T] - mx) for c in range(nch)]
            l_ref[kvh] += functools.reduce(lambda a, b: a + b, ps)
            p = jnp.concatenate(ps, axis=1).astype(bf16)
            acc_ref[kvh] += _dot(p, vb_ref[0, key_rows(kt), kvh * T:(kvh + 1) * T])
        return carry

    lax.fori_loop(0, nkt, pv_tile, 0)
    for kvh in range(n_kv):
        o = acc_ref[kvh] / jnp.sum(l_ref[kvh], axis=1, keepdims=True)
        for g in range(group):
            h = kvh * group + g
            o_ref[0, :, h * T:(h + 1) * T] = o[g * tq:(g + 1) * tq].astype(bf16)


def _dsa_prompt(q, iq, iwt, ik2, kb, vb, bound, *, topk, n_kv, n_ih, tq):
    b, s, qd = q.shape
    T = LANES
    n_heads = qd // T
    group = n_heads // n_kv
    nq = s // tq
    blk = lambda w: pl.BlockSpec((1, tq, w), lambda bi, qi: (bi, qi, 0))
    full = lambda w: pl.BlockSpec((1, s, w), lambda bi, qi: (bi, 0, 0))
    att = pltpu.VMEM((n_kv, group * tq, T), f32)
    return pl.pallas_call(
        functools.partial(_dsa_prompt_body, topk=topk, n_kv=n_kv, group=group, n_ih=n_ih,
                          idx_bits=int(s).bit_length(), tq=tq),
        grid=(b, nq),
        in_specs=[blk(qd), blk(iq.shape[2]), pl.BlockSpec((1, n_ih, tq), lambda bi, qi: (bi, 0, qi)),
                  full(T), full(n_kv * T), full(n_kv * T), pl.BlockSpec((1, T), lambda bi, qi: (0, 0))],
        out_specs=blk(qd),
        out_shape=jax.ShapeDtypeStruct((b, s, qd), bf16),
        scratch_shapes=[
            pltpu.VMEM((nq, tq, tq), f32),
            pltpu.VMEM((n_ih, tq, T), bf16),
            pltpu.VMEM((n_kv, group * tq, T), bf16), pltpu.VMEM((8, tq), i32),
            att, att, att,
        ],
        compiler_params=_params("parallel", "arbitrary"),
        name="dsa_prompt",
    )(q, iq, iwt, ik2, kb, vb, bound)


def _dsa_sample_body(pt_ref, q_ref, iq_ref, iwb_ref, ikn_ref, kn_ref, vn_ref, hm_ref, ex_ref, *rest,
                     n_pages, topk, n_tok, idx_bits):
    del pt_ref
    T = LANES
    idx_pages = rest[:n_pages]
    k_pages = rest[n_pages:2 * n_pages]
    v_pages = rest[2 * n_pages:3 * n_pages]
    o_ref, sc_ref, s_ref, tie_ref = rest[3 * n_pages:]
    nt = n_pages + 1
    n_rows = q_ref.shape[1]
    lane = lax.broadcasted_iota(i32, (n_tok, T), 1)
    tok = lax.broadcasted_iota(i32, (n_tok, T), 0)
    fresh = (lane <= tok) & (lane < n_tok)

    def fresh_tile(new_ref):
        new = new_ref[0]
        fill = jnp.zeros((k_pages[0].shape[0] - new.shape[0], T), f32)
        return jnp.concatenate([new, fill], axis=0)

    iq = iq_ref[0]
    iwb = iwb_ref[0]
    for p in range(nt):
        ikt = (idx_pages[p][...] if p < n_pages else ikn_ref[0]).astype(bf16)
        w = jnp.maximum(_dot(iq, ikt), 0.0) * iwb
        sc = jnp.sum(w.reshape(n_rows // n_tok, n_tok, T), axis=0)
        if p == n_pages:
            sc = jnp.where(fresh, sc, -jnp.inf)
        sc_ref[p] = sc

    def count_fn(preds):
        accs = [jnp.zeros((n_tok, T), f32) for _ in preds]
        for p in range(nt):
            sc = sc_ref[p]
            for n, pred in enumerate(preds):
                accs[n] = accs[n] + jnp.where(pred(sc, p * T + lane), 1.0, 0.0)
        return tuple(jnp.broadcast_to(jnp.sum(a, axis=1, keepdims=True), (n_tok, T)) for a in accs)

    thr, tie_end = _select_topk(count_fn, tie_ref, (n_tok, T), topk, idx_bits, 4)

    q = q_ref[0]
    head_ok = hm_ref[...] > 0.5
    mx = jnp.full((n_rows, T), NEG_BIG, f32)
    for p in range(nt):
        sc = sc_ref[p]
        sel = (sc > thr) | ((sc == thr) & (p * T + lane < tie_end))
        if p == n_pages:
            sel = sel & fresh
        sel = jnp.concatenate([jnp.where(sel, 1.0, 0.0)] * (n_rows // n_tok), axis=0).astype(bf16)
        picked = _dot(sel, ex_ref[...]) > 0.5
        k_t = (k_pages[p][...] if p < n_pages else fresh_tile(kn_ref)).astype(bf16)
        s = _dot_nt(q, k_t) + jnp.where(picked & head_ok, 0.0, NEG_BIG)
        s_ref[p] = s
        for c in range(s.shape[1] // T):
            mx = jnp.maximum(mx, s[:, c * T:(c + 1) * T])
    m = jnp.max(mx, axis=1, keepdims=True)
    l = jnp.zeros((n_rows, 1), f32)
    acc = jnp.zeros((n_rows, T), f32)
    for p in range(nt):
        e = jnp.exp(s_ref[p] - m)
        l = l + jnp.sum(e, axis=1, keepdims=True)
        v_t = (v_pages[p][...] if p < n_pages else fresh_tile(vn_ref)).astype(bf16)
        acc = acc + _dot(e.astype(bf16), v_t)
    o_ref[0] = (acc / l).astype(bf16)


def _dsa_sample(page_table, layer, q, iq, iwb, ikn, kn, vn, head_ok, expand, cache_idx_t, cache_k, cache_v,
                *, topk, n_tok):
    nb, n_rows, T = q.shape
    n_pages = page_table.shape[1]
    kv_rows = cache_k.shape[2]
    idw = cache_idx_t.shape[2]
    seq = lambda shape: pl.BlockSpec((1,) + shape, lambda b, pt: (b, 0, 0))
    const = lambda a: pl.BlockSpec(a.shape, lambda b, pt: (0, 0))

    def page(shape, p):
        return pl.BlockSpec((None, None) + shape, lambda b, pt, p=p: (layer, pt[b, p], 0, 0))

    in_specs = [seq((n_rows, T)), seq((n_rows, idw)), seq((n_rows, T)), seq((idw, T)),
                seq(kn.shape[1:]), seq(vn.shape[1:]), const(head_ok), const(expand)]
    in_specs += [page((idw, T), p) for p in range(n_pages)]
    in_specs += [page((kv_rows, T), p) for p in range(n_pages)]
    in_specs += [page((kv_rows, T), p) for p in range(n_pages)]
    nt = n_pages + 1
    grid_spec = pltpu.PrefetchScalarGridSpec(
        num_scalar_prefetch=1,
        grid=(nb,),
        in_specs=in_specs,
        out_specs=seq((n_rows, T)),
        scratch_shapes=[pltpu.VMEM((nt, n_tok, T), f32), pltpu.VMEM((nt, n_rows, kv_rows), f32),
                        pltpu.VMEM((n_tok, T), i32)],
    )
    return pl.pallas_call(
        functools.partial(_dsa_sample_body, n_pages=n_pages, topk=topk, n_tok=n_tok,
                          idx_bits=int(nt * T).bit_length()),
        grid_spec=grid_spec,
        out_shape=jax.ShapeDtypeStruct((nb, n_rows, T), bf16),
        compiler_params=_params("parallel"),
        name="dsa_sample",
    )(page_table, q, iq, iwb, ikn, kn, vn, head_ok, expand,
      *([cache_idx_t] * n_pages), *([cache_k] * n_pages), *([cache_v] * n_pages))


def _rope_tables(pos, dim):
    half = dim // 2
    inv = ROPE_THETA ** (-jnp.arange(half, dtype=f32) * (2.0 / dim))
    ang = pos.astype(f32)[:, None] * inv[None, :]
    cos = jnp.concatenate([jnp.cos(ang), jnp.cos(ang)], axis=1)
    sin = jnp.concatenate([-jnp.sin(ang), jnp.sin(ang)], axis=1)
    reps = LANES // dim
    return jnp.tile(cos, (1, reps)), jnp.tile(sin, (1, reps))


def _mix_tables(w_s, b_s, c, group_dim):
    causal = jnp.tril(jnp.ones((c, c), dtype=bool))
    ws = jnp.where(causal[None], w_s[:, :c, :c], 0)
    reps = LANES // c
    ws = jnp.tile(ws, (1, reps, reps))
    blk = jnp.arange(LANES) // c
    ws = jnp.where((blk[:, None] == blk[None, :])[None], ws, 0).astype(bf16)
    bias = jnp.tile(jnp.repeat(b_s[:, :c].T, group_dim, axis=1), (reps, 1)).astype(f32)
    return ws, bias


def kernel(x_prompt, x_sample, cache_k, cache_v, cache_idx_k, state_ffn_conv, page_table, a_norm, a_w_in, a_v_norm, a_w_s, a_b_s, a_w_out, b_norm, b_w_in, b_q_norm, b_k_norm, b_w_o, f_norm, f_w_in, f_conv_w, f_conv_b, f_w_out):
    bsz, seq, d = x_prompt.shape
    nb, n_tok, _ = x_sample.shape
    depth = f_norm.shape[0]
    n_layers_b, n_pool, page, n_kv, hd = cache_k.shape
    idw = cache_idx_k.shape[3]
    past = page_table.shape[1] * page
    n_heads = b_w_o.shape[1] // hd
    group = n_heads // n_kv
    n_ih = (b_w_in.shape[2] - (n_heads + 2 * n_kv) * hd - idw) // (idw + 1)
    ff = f_w_out.shape[1]
    a_groups, chunk = a_w_s.shape[1], a_w_s.shape[2]
    a_width = a_w_in.shape[2] // 2
    assert hd == LANES and idw == 64 and page == LANES and chunk == LANES and n_ih % 2 == 0
    assert f_conv_w.shape[1] == 3 and n_tok >= 2 and LANES % n_tok == 0 and n_tok & (n_tok - 1) == 0

    mp, ms = bsz * seq, nb * n_tok
    xp = x_prompt.reshape(mp, d)
    xs = x_sample.reshape(ms, d)
    tm_p = min(TM_ROWS, seq)
    tm_s = min(TM_ROWS, ms)
    tq = min(TQ_PROMPT, seq)

    aw_in, aw_out = a_w_in.astype(bf16), a_w_out.astype(bf16)
    n_real = b_w_in.shape[2]
    n_pad = -(-n_real // LANES) * LANES
    bw_in = jnp.pad(b_w_in, ((0, 0), (0, 0), (0, n_pad - n_real))).astype(bf16)
    bw_o = b_w_o.astype(bf16)
    fw_in, fw_out = f_w_in.astype(bf16), f_w_out.astype(bf16)
    tn_b = LANES * 11 if n_pad % (LANES * 11) == 0 else LANES
    tn_f = 512 if ff % 512 == 0 else LANES

    ck = cache_k.reshape(n_layers_b, n_pool, page * n_kv, hd)
    cv = cache_v.reshape(n_layers_b, n_pool, page * n_kv, hd)
    cidx_t = jnp.swapaxes(cache_idx_k, 2, 3)
    row_kv = (jnp.arange(n_heads * n_tok) // n_tok) // group
    head_ok = (row_kv[:, None] == (jnp.arange(page * n_kv) % n_kv)[None, :]).astype(f32)
    expand = jnp.repeat(jnp.eye(page, dtype=bf16), n_kv, axis=1)

    outs = dict(k_p=[], v_p=[], ik_p=[], k_s=[], v_s=[], ik_s=[], chunk_v=[], conv_p=[], conv_s=[])
    n_mixers = 2
    for layer in range(depth):
        j = layer // n_mixers
        if layer % n_mixers == 0:
            new = []
            for x, tm, c, want in ((xp, tm_p, min(seq, chunk), False), (xs, tm_s, min(n_tok, chunk), True)):
                wmix, bias = _mix_tables(a_w_s[j], a_b_s[j], c, a_width // a_groups)
                z = _norm_matmul(x, a_norm, aw_in, j, act="gelu", tm=min(TM_FFN, x.shape[0]), tn=512)
                gated, vn = _spatial_gate(z, a_v_norm, j, wmix, bias, tm=min(tm, 512), want_vn=want)
                new.append(_matmul_residual(gated, aw_out, j, x, tm=tm, tn=d))
                if want:
                    outs["chunk_v"].append(vn.reshape(nb, n_tok, a_width))
            xp, xs = new
        else:
            pos_p = jnp.arange(seq, dtype=i32)
            pos_s = jnp.tile(past + jnp.arange(n_tok, dtype=i32), nb)
            res = []
            for x, tm, pos in ((xp, tm_p, pos_p), (xs, tm_s, pos_s)):
                p = _norm_matmul(x, b_norm, bw_in, j, tm=min(TM_FFN, x.shape[0]), tn=tn_b)
                tabs = _rope_tables(pos, hd) + _rope_tables(pos, idw)
                res.append(_attn_post(p, b_q_norm, b_k_norm, j, tabs, tm=tm,
                                      n_heads=n_heads, n_kv=n_kv, n_ih=n_ih))
            q, k, v, kb, vb, iq, ikw, ik2 = res[0]
            r3 = lambda a: a.reshape(bsz, seq, a.shape[1])
            iwt = jnp.swapaxes(ikw[:, idw:idw + n_ih].reshape(bsz, seq, n_ih), 1, 2)
            bound = 1.01 * math.sqrt(hd) * jnp.max(jnp.abs(b_q_norm[j])) * jnp.max(jnp.abs(b_k_norm[j]))
            bound = jnp.full((1, LANES), bound, f32)
            o = _dsa_prompt(r3(q), r3(iq), iwt, r3(ik2), r3(kb), r3(vb), bound,
                            topk=min(TOPK_MAX, seq // 4), n_kv=n_kv, n_ih=n_ih, tq=tq)
            xp = _matmul_residual(o.reshape(mp, n_heads * hd), bw_o, j, xp, tm=tm_p, tn=d)
            n_seq_pages = seq // page
            outs["k_p"].append(k.reshape(bsz, n_seq_pages, page, n_kv, hd))
            outs["v_p"].append(v.reshape(bsz, n_seq_pages, page, n_kv, hd))
            outs["ik_p"].append(ikw[:, :idw].reshape(bsz, n_seq_pages, page, idw))

            q, k, v, kb, vb, iq, ikw, ik2 = res[1]
            hm = lambda a, w: a.reshape(nb, n_tok, a.shape[1] // w, w).transpose(0, 2, 1, 3).reshape(nb, -1, w)
            qs = hm(q, hd)
            iqs = hm(iq, idw)
            iw = ikw[:, idw:idw + n_ih].reshape(nb, n_tok, n_ih).transpose(0, 2, 1).reshape(nb, n_ih * n_tok, 1)
            iwb = jnp.broadcast_to(iw, (nb, n_ih * n_tok, LANES))
            pad_tok = lambda a: jnp.pad(a.reshape(nb, n_tok, -1), ((0, 0), (0, page - n_tok), (0, 0)))
            ikn = jnp.swapaxes(pad_tok(ikw[:, :idw]), 1, 2)
            kn = k.reshape(nb, n_tok * n_kv, hd)
            vn_ = v.reshape(nb, n_tok * n_kv, hd)
            os_ = _dsa_sample(page_table, j, qs, iqs, iwb, ikn, kn, vn_, head_ok, expand, cidx_t, ck, cv,
                              topk=min(TOPK_MAX, (past + n_tok) // 4), n_tok=n_tok)
            os_ = os_.reshape(nb, n_heads, n_tok, hd).transpose(0, 2, 1, 3).reshape(ms, n_heads * hd)
            xs = _matmul_residual(os_, bw_o, j, xs, tm=tm_s, tn=d)
            outs["k_s"].append(k.reshape(nb, n_tok, n_kv, hd))
            outs["v_s"].append(v.reshape(nb, n_tok, n_kv, hd))
            outs["ik_s"].append(ikw[:, :idw].reshape(nb, n_tok, idw))

        tm_f = min(TM_FFN, seq)
        xp, tg, tu = _conv_ffn(xp, f_norm, fw_in, f_conv_w, f_conv_b, fw_out, layer,
                               tm=tm_f, tn=tn_f, seq_len=seq)
        tails = jnp.concatenate([tg, tu], axis=2).reshape(bsz, seq // tm_f, 8, 2 * ff)
        outs["conv_p"].append(tails[:, -1, 6:8, :])
        st = state_ffn_conv[layer]
        st = jnp.pad(st, ((0, 0), (0, n_tok - 2), (0, 0))).reshape(ms, 2 * ff)
        xs, ng, nu = _conv_ffn(xs, f_norm, fw_in, f_conv_w, f_conv_b, fw_out, layer,
                               tm=tm_s, tn=tn_f, seq_len=n_tok, state=st)
        outs["conv_s"].append(jnp.concatenate([ng, nu], axis=2))

    st = lambda key: jnp.stack(outs[key])
    return (xp.reshape(bsz, seq, d), xs.reshape(nb, n_tok, d), st("k_p"), st("v_p"), st("ik_p"),
            st("k_s"), st("v_s"), st("ik_s"), st("chunk_v"), st("conv_p"), st("conv_s"))
```

```python
import functools
import math

import jax
import jax.numpy as jnp
from jax import lax
from jax.experimental import pallas as pl
from jax.experimental.pallas import tpu as pltpu

EPS = 1e-6
ROPE_THETA = 10000.0
TOPK_MAX = 256
LANES = 128
NEG_BIG = -1e30
INT_MIN = -(2 ** 31)
KEY_NEG_INF = -2139095041
TIE_ALL = 2 ** 30
SAFE_SHIFT_BOUND = 40.0
VMEM_LIMIT = 56 * 1024 * 1024
TM_ROWS = 512
TM_FFN = 1024
TQ_PROMPT = 256
SEQ_PER_STEP = 2

f32 = jnp.float32
bf16 = jnp.bfloat16
i32 = jnp.int32


def _params(*sem):
    return pltpu.CompilerParams(dimension_semantics=sem, vmem_limit_bytes=VMEM_LIMIT)


def _dot(a, b):
    return jnp.dot(a, b, preferred_element_type=f32)


def _dot_nt(a, b):
    return lax.dot_general(a, b, (((1,), (1,)), ((), ())), preferred_element_type=f32)


def _rms(x, g):
    r = lax.rsqrt(jnp.mean(x * x, axis=-1, keepdims=True) + EPS)
    return x * r * g


def _rows(a):
    return a.reshape(a.shape[0], 1, a.shape[1])


def _gelu_tanh(x):
    c = math.sqrt(2.0 / math.pi)
    return 0.5 * x * (1.0 + jnp.tanh(c * (x + 0.044715 * (x * x * x))))


def _nmm_body(x_ref, g_ref, w_ref, o_ref, h_ref, *, act):
    @pl.when(pl.program_id(1) == 0)
    def _():
        h_ref[...] = _rms(x_ref[...], g_ref[...]).astype(bf16)

    z = _dot(h_ref[...], w_ref[...])
    if act == "gelu":
        z = _gelu_tanh(z)
    o_ref[...] = z.astype(o_ref.dtype)


def _norm_matmul(x, gain, w, layer, *, act=None, tm, tn, out_dtype=f32):
    m, k = x.shape
    n = w.shape[2]
    return pl.pallas_call(
        functools.partial(_nmm_body, act=act),
        grid=(m // tm, n // tn),
        in_specs=[
            pl.BlockSpec((tm, k), lambda i, j: (i, 0)),
            pl.BlockSpec((None, 1, k), lambda i, j: (layer, 0, 0)),
            pl.BlockSpec((None, k, tn), lambda i, j: (layer, 0, j)),
        ],
        out_specs=pl.BlockSpec((tm, tn), lambda i, j: (i, j)),
        out_shape=jax.ShapeDtypeStruct((m, n), out_dtype),
        scratch_shapes=[pltpu.VMEM((tm, k), bf16)],
        compiler_params=_params("parallel", "arbitrary"),
        name="norm_matmul",
    )(x, _rows(gain), w)


def _mmres_body(a_ref, w_ref, x_ref, o_ref):
    o_ref[...] = x_ref[...] + _dot(a_ref[...], w_ref[...])


def _matmul_residual(a, w, layer, x, *, tm, tn):
    m, k = a.shape
    n = w.shape[2]
    return pl.pallas_call(
        _mmres_body,
        grid=(m // tm, n // tn),
        in_specs=[
            pl.BlockSpec((tm, k), lambda i, j: (i, 0)),
            pl.BlockSpec((None, k, tn), lambda i, j: (layer, 0, j)),
            pl.BlockSpec((tm, tn), lambda i, j: (i, j)),
        ],
        out_specs=pl.BlockSpec((tm, tn), lambda i, j: (i, j)),
        out_shape=jax.ShapeDtypeStruct((m, n), f32),
        compiler_params=_params("parallel", "arbitrary"),
        name="matmul_residual",
    )(a, w, x)


def _gate_body(u_ref, v_ref, vg_ref, wmix_ref, bias_ref, o_ref, *vn_out, tm, groups):
    vn = _rms(v_ref[...], vg_ref[...])
    if vn_out:
        vn_out[0][...] = vn
    vnb = vn.astype(bf16)
    gd = v_ref.shape[1] // groups
    for c in range(tm // LANES):
        rows = slice(c * LANES, (c + 1) * LANES)
        for g in range(groups):
            cols = slice(g * gd, (g + 1) * gd)
            s = _dot(wmix_ref[g], vnb[rows, cols]) + bias_ref[:, cols]
            o_ref[rows, cols] = (u_ref[rows, cols] * s).astype(bf16)


def _spatial_gate(z, v_gain, layer, wmix, bias, *, tm, want_vn):
    m, w2 = z.shape
    w = w2 // 2
    groups = wmix.shape[0]
    out_shape = [jax.ShapeDtypeStruct((m, w), bf16)]
    out_specs = [pl.BlockSpec((tm, w), lambda i: (i, 0))]
    if want_vn:
        out_shape.append(jax.ShapeDtypeStruct((m, w), f32))
        out_specs.append(pl.BlockSpec((tm, w), lambda i: (i, 0)))
    res = pl.pallas_call(
        functools.partial(_gate_body, tm=tm, groups=groups),
        grid=(m // tm,),
        in_specs=[
            pl.BlockSpec((tm, w), lambda i: (i, 0)),
            pl.BlockSpec((tm, w), lambda i: (i, 1)),
            pl.BlockSpec((None, 1, w), lambda i: (layer, 0, 0)),
            pl.BlockSpec(wmix.shape, lambda i: (0, 0, 0)),
            pl.BlockSpec(bias.shape, lambda i: (0, 0)),
        ],
        out_specs=out_specs,
        out_shape=out_shape,
        compiler_params=_params("parallel"),
        name="spatial_gate",
    )(z, z, _rows(v_gain), wmix, bias)
    return res if want_vn else (res[0], None)


def _conv3(a, cw, cb, prev1, prev2):
    return cb + cw[0:1] * prev2 + cw[1:2] * prev1 + cw[2:3] * a


def _ffn_body(x_ref, g_ref, wg_ref, wu_ref, cwg_ref, cwu_ref, cbg_ref, cbu_ref, wo_ref, *rest,
              sample, seq_tiles, seq_len, nj):
    if sample:
        sg_ref, su_ref, o_ref, ag_ref, au_ref, h_ref, act0_ref, act1_ref = rest
    else:
        o_ref, tg_ref, tu_ref, h_ref, act0_ref, act1_ref, carry_ref = rest
    acts = (act0_ref, act1_ref)
    i = pl.program_id(0)
    j = pl.program_id(1)

    @pl.when(j == 0)
    def _():
        x = x_ref[...]
        h_ref[...] = _rms(x, g_ref[...]).astype(bf16)
        o_ref[...] = x
        act1_ref[...] = jnp.zeros(act1_ref.shape, bf16)

    def down(src_ref):
        o_ref[...] += _dot(src_ref[...], wo_ref[...])

    def up_conv_gate(dst_ref):
        h = h_ref[...]
        ag = _dot(h, wg_ref[...])
        au = _dot(h, wu_ref[...])
        tm, tn = ag.shape
        row = lax.broadcasted_iota(i32, (tm, tn), 0)
        if sample:
            t = row & (seq_len - 1)
            m1 = t == 0
            m2 = t < 2
            p2g, p2u = sg_ref[...], su_ref[...]
            p1g, p1u = pltpu.roll(p2g, tm - 1, 0), pltpu.roll(p2u, tm - 1, 0)
            for a, a_ref in ((ag, ag_ref), (au, au_ref)):
                last = pltpu.roll(a, tm - (seq_len - 2), 0).reshape(tm // seq_len, seq_len, tn)
                a_ref[...] = last[:, :2, :]
        else:
            m1 = row == 0
            m2 = row < 2
            live = (i % seq_tiles) != 0
            cg = jnp.where(live, carry_ref[j, 0], 0.0)
            cu = jnp.where(live, carry_ref[j, 1], 0.0)
            p1g, p1u = cg[7:8], cu[7:8]
            p2g = jnp.where(m1, cg[6:7], cg[7:8])
            p2u = jnp.where(m1, cu[6:7], cu[7:8])
            carry_ref[j, 0] = ag[tm - 8:]
            carry_ref[j, 1] = au[tm - 8:]
            tg_ref[0] = ag[tm - 8:]
            tu_ref[0] = au[tm - 8:]
        a1g = jnp.where(m1, p1g, pltpu.roll(ag, 1, 0))
        a2g = jnp.where(m2, p2g, pltpu.roll(ag, 2, 0))
        a1u = jnp.where(m1, p1u, pltpu.roll(au, 1, 0))
        a2u = jnp.where(m2, p2u, pltpu.roll(au, 2, 0))
        cg_ = _conv3(ag, cwg_ref[...], cbg_ref[...], a1g, a2g)
        cu_ = _conv3(au, cwu_ref[...], cbu_ref[...], a1u, a2u)
        dst_ref[...] = (cg_ * (1.0 / (1.0 + jnp.exp(-cg_))) * cu_).astype(bf16)

    for parity in (0, 1):
        @pl.when((j < nj) & ((j & 1) == parity))
        def _(parity=parity):
            up_conv_gate(acts[parity])
            down(acts[1 - parity])

    @pl.when(j == nj)
    def _():
        down(acts[(nj - 1) % 2])


def _conv_ffn(x, gain, w_in, conv_w, conv_b, w_out, layer, *, tm, tn, seq_len, state=None):
    m, d = x.shape
    ff = w_out.shape[1]
    nj = ff // tn
    ni = m // tm
    sample = state is not None
    up = lambda j: jnp.minimum(j, nj - 1)
    down = lambda j: jnp.maximum(j - 1, 0)
    in_specs = [
        pl.BlockSpec((tm, d), lambda i, j: (i, 0), pipeline_mode=pl.Buffered(1)),
        pl.BlockSpec((None, 1, d), lambda i, j: (layer, 0, 0)),
        pl.BlockSpec((None, d, tn), lambda i, j: (layer, 0, up(j))),
        pl.BlockSpec((None, d, tn), lambda i, j: (layer, 0, nj + up(j))),
        pl.BlockSpec((None, 3, tn), lambda i, j: (layer, 0, up(j))),
        pl.BlockSpec((None, 3, tn), lambda i, j: (layer, 0, nj + up(j))),
        pl.BlockSpec((None, 1, tn), lambda i, j: (layer, 0, up(j))),
        pl.BlockSpec((None, 1, tn), lambda i, j: (layer, 0, nj + up(j))),
        pl.BlockSpec((None, tn, d), lambda i, j: (layer, down(j), 0)),
    ]
    conv_b = _rows(conv_b)
    args = [x, _rows(gain), w_in, w_in, conv_w, conv_w, conv_b, conv_b, w_out]
    out_specs = [pl.BlockSpec((tm, d), lambda i, j: (i, 0))]
    out_shape = [jax.ShapeDtypeStruct((m, d), f32)]
    scratch = [pltpu.VMEM((tm, d), bf16), pltpu.VMEM((tm, tn), bf16), pltpu.VMEM((tm, tn), bf16)]
    if sample:
        in_specs += [
            pl.BlockSpec((tm, tn), lambda i, j: (i, up(j))),
            pl.BlockSpec((tm, tn), lambda i, j: (i, nj + up(j))),
        ]
        args += [state, state]
        out_specs += [pl.BlockSpec((tm // seq_len, 2, tn), lambda i, j: (i, 0, up(j)))] * 2
        out_shape += [jax.ShapeDtypeStruct((m // seq_len, 2, ff), f32)] * 2
        seq_tiles = 1
    else:
        out_specs += [pl.BlockSpec((1, 8, tn), lambda i, j: (i, 0, up(j)))] * 2
        out_shape += [jax.ShapeDtypeStruct((ni, 8, ff), f32)] * 2
        scratch.append(pltpu.VMEM((nj, 2, 8, tn), f32))
        seq_tiles = seq_len // tm
    return pl.pallas_call(
        functools.partial(_ffn_body, sample=sample, seq_tiles=seq_tiles, seq_len=seq_len, nj=nj),
        grid=(ni, nj + 1),
        in_specs=in_specs,
        out_specs=out_specs,
        out_shape=out_shape,
        scratch_shapes=scratch,
        compiler_params=_params("arbitrary", "arbitrary"),
        name="conv_ffn_sample" if sample else "conv_ffn_prompt",
    )(*args)


def _rope_full(x, c, s):
    return x * c + pltpu.roll(x, LANES // 2, 1) * s


def _rope_half(x, c, s, lo):
    partner = jnp.where(lo, pltpu.roll(x, 96, 1), pltpu.roll(x, 32, 1))
    return x * c + partner * s


def _post_body(p_ref, qg_ref, kg_ref, c1_ref, s1_ref, c2_ref, s2_ref,
               q_ref, k_ref, v_ref, kb_ref, vb_ref, iq_ref, ikw_ref, ik2_ref,
               *, n_heads, n_kv, n_ih, idx_scale, q_scale):
    hd = LANES
    c1, s1, c2, s2 = c1_ref[...], s1_ref[...], c2_ref[...], s2_ref[...]
    tm = p_ref.shape[0]
    lane = lax.broadcasted_iota(i32, (tm, LANES), 1)
    lo = (lane & 63) < 32
    off = 0
    for h in range(n_heads):
        x = p_ref[:, off + h * hd: off + (h + 1) * hd]
        q_ref[:, h * hd:(h + 1) * hd] = (_rope_full(_rms(x, qg_ref[...]), c1, s1) * q_scale).astype(bf16)
    off += n_heads * hd
    for h in range(n_kv):
        x = p_ref[:, off + h * hd: off + (h + 1) * hd]
        kr = _rope_full(_rms(x, kg_ref[...]), c1, s1)
        k_ref[pl.ds(h, tm, stride=n_kv), :] = kr
        kb_ref[:, h * hd:(h + 1) * hd] = kr.astype(bf16)
    off += n_kv * hd
    vv = p_ref[:, off: off + n_kv * hd]
    for h in range(n_kv):
        v_ref[pl.ds(h, tm, stride=n_kv), :] = vv[:, h * hd:(h + 1) * hd]
    vb_ref[...] = vv.astype(bf16)
    off += n_kv * hd
    for h in range(n_ih // 2):
        x = p_ref[:, off + h * LANES: off + (h + 1) * LANES]
        iq_ref[:, h * LANES:(h + 1) * LANES] = _rope_half(x, c2, s2, lo).astype(bf16)
    off += (n_ih // 2) * LANES
    tail = p_ref[:, off: off + LANES]
    tr = _rope_half(tail, c2, s2, lo)
    ik = jnp.where(lane < 64, tr, 0.0)
    ik2_ref[...] = (ik + pltpu.roll(ik, 64, 1)).astype(bf16)
    ikw_ref[...] = jnp.where(lane < 64, tr, tail * idx_scale)


def _attn_post(p, q_gain, k_gain, layer, tabs, *, tm, n_heads, n_kv, n_ih):
    m, npad = p.shape
    c1, s1, c2, s2 = tabs
    nt = c1.shape[0] // tm
    hd = LANES
    tab_spec = pl.BlockSpec((tm, LANES), lambda i: (i % nt, 0))
    row = lambda w: pl.BlockSpec((tm, w), lambda i: (i, 0))
    gain_spec = pl.BlockSpec((None, 1, hd), lambda i: (layer, 0, 0))
    outs = [
        (1, n_heads * hd, bf16), (n_kv, hd, f32), (n_kv, hd, f32), (1, n_kv * hd, bf16), (1, n_kv * hd, bf16),
        (1, n_ih * 64, bf16), (1, LANES, f32), (1, LANES, bf16),
    ]
    return pl.pallas_call(
        functools.partial(_post_body, n_heads=n_heads, n_kv=n_kv, n_ih=n_ih,
                          idx_scale=float((n_ih * 64) ** -0.5), q_scale=float(hd ** -0.5)),
        grid=(m // tm,),
        in_specs=[row(npad), gain_spec, gain_spec, tab_spec, tab_spec, tab_spec, tab_spec],
        out_specs=[pl.BlockSpec((tm * r, w), lambda i: (i, 0)) for r, w, _ in outs],
        out_shape=[jax.ShapeDtypeStruct((m * r, w), dt) for r, w, dt in outs],
        compiler_params=_params("parallel"),
        name="attn_post",
    )(p, _rows(q_gain), _rows(k_gain), c1, s1, c2, s2)


def _key_to_float(key):
    bits = key ^ ((key >> 31) & 0x7FFFFFFF)
    return lax.bitcast_convert_type(bits, f32)


def _select_threshold(count_fn, shape, topk, bits_per_round):
    kk = float(topk)
    n_cand = (1 << bits_per_round) - 1

    def round_body(it, res):
        step = lax.shift_left(jnp.int32(1), 32 - bits_per_round * (it + 1))
        cands = [res + step * (c + 1) for c in range(n_cand)]
        floats = [_key_to_float(c) for c in cands]
        cnts = count_fn(tuple((lambda sc, idx, cf=cf: sc >= cf) for cf in floats))
        for c, n in zip(cands, cnts):
            res = jnp.where(n >= kk, c, res)
        return res

    res = lax.fori_loop(0, 32 // bits_per_round, round_body, jnp.full(shape, INT_MIN, i32))
    return _key_to_float(jnp.maximum(res, KEY_NEG_INF))


def _tie_search(count_fn, thr, need, shape, idx_bits):
    def tie_body(it, end):
        cand = end + lax.shift_left(jnp.int32(1), idx_bits - 1 - it)
        (cnt,) = count_fn((lambda sc, idx: (sc == thr) & (idx < cand),))
        return jnp.where(cnt <= need, cand, end)

    return lax.fori_loop(0, idx_bits, tie_body, jnp.zeros(shape, i32))


def _select_topk(count_fn, tie_ref, shape, topk, idx_bits, bits_per_round):
    kk = float(topk)
    thr = _select_threshold(count_fn, shape, topk, bits_per_round)
    n_gt, n_ge = count_fn((lambda sc, idx: sc > thr, lambda sc, idx: sc >= thr))
    tie_ref[...] = jnp.full(shape, TIE_ALL, i32)
    crowded = jnp.where((n_ge > kk) & (thr > -jnp.inf), 1.0, 0.0)

    @pl.when(jnp.max(crowded) > 0.0)
    def _():
        tie_ref[...] = _tie_search(count_fn, thr, kk - n_gt, shape, idx_bits)

    return thr, tie_ref[...]


def _dsa_prompt_body(q_ref, iq_ref, iwt_ref, ik2_ref, kb_ref, vb_ref, bound_ref, o_ref,
                     sc_ref, iqm_ref, qg_ref, tie_ref, m_ref, l_ref, acc_ref,
                     *, topk, n_kv, group, n_ih, idx_bits, tq):
    T = LANES
    nch = tq // T
    strips = tq // 8
    qb = pl.program_id(1)
    nkt = qb + 1
    lane = lax.broadcasted_iota(i32, (tq, T), 1)

    for h in range(n_ih):
        pair = iq_ref[0, :, (h // 2) * T:(h // 2 + 1) * T].astype(f32)
        keep = (lane < 64) if h % 2 == 0 else (lane >= 64)
        iqm_ref[h] = jnp.where(keep, pair, 0.0).astype(bf16)
    for h in range(n_kv * group):
        qg_ref[h // group, (h % group) * tq:(h % group + 1) * tq, :] = q_ref[0, :, h * T:(h + 1) * T]

    def key_rows(kt):
        return pl.ds(pl.multiple_of(kt * tq, tq), tq)

    key_in_tile = (lax.broadcasted_iota(i32, (strips, 8, tq), 0) * 8
                   + lax.broadcasted_iota(i32, (strips, 8, tq), 1))
    qidx = qb * tq + lax.broadcasted_iota(i32, (strips, 8, tq), 2)

    def score_tile(kt, carry):
        ik_t = ik2_ref[0, key_rows(kt), :]
        acc = jnp.zeros((tq, tq), f32)
        for h in range(n_ih):
            acc = acc + jnp.maximum(_dot_nt(ik_t, iqm_ref[h]), 0.0) * iwt_ref[0, h:h + 1, :]
        admissible = kt * tq + key_in_tile <= qidx
        sc_ref[kt] = jnp.where(admissible, acc.reshape(strips, 8, tq), -jnp.inf).reshape(tq, tq)
        return carry

    lax.fori_loop(0, nkt, score_tile, 0)

    def count_fn(preds):
        def body(kt, accs):
            sc = sc_ref[kt].reshape(strips, 8, tq)
            kidx = kt * tq + key_in_tile
            return tuple(a + jnp.sum(jnp.where(pred(sc, kidx), 1.0, 0.0), axis=0)
                         for a, pred in zip(accs, preds))

        accs = lax.fori_loop(0, nkt, body, tuple(jnp.zeros((8, tq), f32) for _ in preds))
        return tuple(jnp.broadcast_to(jnp.sum(a, axis=0, keepdims=True), (8, tq)) for a in accs)

    thr, tie_end = _select_topk(count_fn, tie_ref, (8, tq), topk, idx_bits, 1)

    def bias_tile(kt, carry):
        sc = sc_ref[kt].reshape(strips, 8, tq)
        kidx = kt * tq + key_in_tile
        sel = ((sc > thr) | ((sc == thr) & (kidx < tie_end))) & (kidx <= qidx)
        sc_ref[kt] = jnp.where(sel, 0.0, NEG_BIG).reshape(tq, tq).T
        return carry

    lax.fori_loop(0, nkt, bias_tile, 0)

    gq = group * tq

    def scores(kt, kvh):
        k_t = kb_ref[0, key_rows(kt), kvh * T:(kvh + 1) * T]
        s = _dot_nt(qg_ref[kvh], k_t)
        return (s.reshape(group, tq, tq) + sc_ref[kt][None]).reshape(gq, tq)

    bound = bound_ref[...]
    m_ref[...] = jnp.broadcast_to(bound[None], m_ref.shape)

    @pl.when(jnp.max(bound) > SAFE_SHIFT_BOUND)
    def _():
        m_ref[...] = jnp.full(m_ref.shape, NEG_BIG, f32)

        def max_tile(kt, carry):
            for kvh in range(n_kv):
                s = scores(kt, kvh)
                mx = m_ref[kvh]
                for c in range(nch):
                    mx = jnp.maximum(mx, s[:, c * T:(c + 1) * T])
                m_ref[kvh] = mx
            return carry

        lax.fori_loop(0, nkt, max_tile, 0)
        for kvh in range(n_kv):
            m_ref[kvh] = jnp.broadcast_to(jnp.max(m_ref[kvh], axis=1, keepdims=True), (gq, T))

    l_ref[...] = jnp.zeros(l_ref.shape, f32)
    acc_ref[...] = jnp.zeros(acc_ref.shape, f32)

    def pv_tile(kt, carry):
        for kvh in range(n_kv):
            s = scores(kt, kvh)
            mx = m_ref[kvh]
            ps = [jnp.exp(s[:, c * T:(c + 1) * T] - mx) for c in range(nch)]
            l_ref[kvh] += functools.reduce(lambda a, b: a + b, ps)
            p = jnp.concatenate(ps, axis=1).astype(bf16)
            acc_ref[kvh] += _dot(p, vb_ref[0, key_rows(kt), kvh * T:(kvh + 1) * T])
        return carry

    lax.fori_loop(0, nkt, pv_tile, 0)
    for kvh in range(n_kv):
        o = acc_ref[kvh] / jnp.sum(l_ref[kvh], axis=1, keepdims=True)
        for g in range(group):
            h = kvh * group + g
            o_ref[0, :, h * T:(h + 1) * T] = o[g * tq:(g + 1) * tq].astype(bf16)


def _dsa_prompt(q, iq, iwt, ik2, kb, vb, bound, *, topk, n_kv, n_ih, tq):
    b, s, qd = q.shape
    T = LANES
    n_heads = qd // T
    group = n_heads // n_kv
    nq = s // tq
    blk = lambda w: pl.BlockSpec((1, tq, w), lambda bi, qi: (bi, qi, 0))
    full = lambda w: pl.BlockSpec((1, s, w), lambda bi, qi: (bi, 0, 0))
    att = pltpu.VMEM((n_kv, group * tq, T), f32)
    return pl.pallas_call(
        functools.partial(_dsa_prompt_body, topk=topk, n_kv=n_kv, group=group, n_ih=n_ih,
                          idx_bits=int(s).bit_length(), tq=tq),
        grid=(b, nq),
        in_specs=[blk(qd), blk(iq.shape[2]), pl.BlockSpec((1, n_ih, tq), lambda bi, qi: (bi, 0, qi)),
                  full(T), full(n_kv * T), full(n_kv * T), pl.BlockSpec((1, T), lambda bi, qi: (0, 0))],
        out_specs=blk(qd),
        out_shape=jax.ShapeDtypeStruct((b, s, qd), bf16),
        scratch_shapes=[
            pltpu.VMEM((nq, tq, tq), f32),
            pltpu.VMEM((n_ih, tq, T), bf16),
            pltpu.VMEM((n_kv, group * tq, T), bf16), pltpu.VMEM((8, tq), i32),
            att, att, att,
        ],
        compiler_params=_params("parallel", "arbitrary"),
        name="dsa_prompt",
    )(q, iq, iwt, ik2, kb, vb, bound)


def _dsa_sample_body(pt_ref, q_ref, iq_ref, iwb_ref, ikn_ref, kn_ref, vn_ref, *rest,
                     n_seq, n_pages, n_kv, topk, n_tok, idx_bits):
    del pt_ref
    T = LANES
    pages = [rest[i * n_pages:(i + 1) * n_pages] for i in range(3 * n_seq)]
    idx_pages, k_pages, v_pages = pages[:n_seq], pages[n_seq:2 * n_seq], pages[2 * n_seq:]
    o_ref, sc_ref, s_ref, tie_ref, thr_ref = rest[3 * n_seq * n_pages:]
    nt = n_pages + 1
    n_rows = q_ref.shape[1]
    sel_rows = n_seq * n_tok
    lane = lax.broadcasted_iota(i32, (sel_rows, T), 1)
    tok = lax.broadcasted_iota(i32, (sel_rows, T), 0) & (n_tok - 1)
    fresh = (lane <= tok) & (lane < n_tok)

    for s in range(n_seq):
        iq = iq_ref[s]
        iwb = iwb_ref[s]
        for p in range(nt):
            ikt = (idx_pages[s][p][...] if p < n_pages else ikn_ref[s]).astype(bf16)
            w = jnp.maximum(_dot(iq, ikt), 0.0) * iwb
            sc_ref[p, s * n_tok:(s + 1) * n_tok, :] = jnp.sum(w.reshape(n_rows // n_tok, n_tok, T), axis=0)
    sc_ref[n_pages] = jnp.where(fresh, sc_ref[n_pages], -jnp.inf)

    def count_fn(preds):
        accs = [jnp.zeros((sel_rows, T), f32) for _ in preds]
        for p in range(nt):
            sc = sc_ref[p]
            for n, pred in enumerate(preds):
                accs[n] = accs[n] + jnp.where(pred(sc, p * T + lane), 1.0, 0.0)
        return tuple(jnp.broadcast_to(jnp.sum(a, axis=1, keepdims=True), (sel_rows, T)) for a in accs)

    thr, _ = _select_topk(count_fn, tie_ref, (sel_rows, T), topk, idx_bits, 4)
    thr_ref[...] = thr

    lane_s = lax.broadcasted_iota(i32, (n_tok, T), 1)
    fresh_s = (lane_s <= lax.broadcasted_iota(i32, (n_tok, T), 0)) & (lane_s < n_tok)
    rows_kv = n_rows // n_kv

    def head_rows(ref, kvh):
        n_keys = ref.shape[0] // n_kv
        rows = ref[pl.ds(kvh, n_keys, stride=n_kv), :]
        if n_keys < T:
            rows = jnp.concatenate([rows, jnp.zeros((T - n_keys, T), f32)], axis=0)
        return rows.astype(bf16)

    for s in range(n_seq):
        q = q_ref[s]
        toks = slice(s * n_tok, (s + 1) * n_tok)
        thr_s, tie_s = thr_ref[toks, :], tie_ref[toks, :]
        mx = jnp.full((n_rows, T), NEG_BIG, f32)
        for p in range(nt):
            sc = sc_ref[p, toks, :]
            sel = (sc > thr_s) | ((sc == thr_s) & (p * T + lane_s < tie_s))
            if p == n_pages:
                sel = sel & fresh_s
            bias = jnp.concatenate([jnp.where(sel, 0.0, NEG_BIG)] * (rows_kv // n_tok), axis=0)
            k_ref = k_pages[s][p] if p < n_pages else kn_ref.at[s]
            sm = jnp.concatenate(
                [_dot_nt(q[kvh * rows_kv:(kvh + 1) * rows_kv], head_rows(k_ref, kvh)) + bias
                 for kvh in range(n_kv)], axis=0)
            s_ref[s, p] = sm
            mx = jnp.maximum(mx, sm)
        m = jnp.max(mx, axis=1, keepdims=True)
        l = jnp.zeros((n_rows, 1), f32)
        accs = [jnp.zeros((rows_kv, T), f32) for _ in range(n_kv)]
        for p in range(nt):
            e = jnp.exp(s_ref[s, p] - m)
            l = l + jnp.sum(e, axis=1, keepdims=True)
            eb = e.astype(bf16)
            v_ref = v_pages[s][p] if p < n_pages else vn_ref.at[s]
            for kvh in range(n_kv):
                accs[kvh] = accs[kvh] + _dot(eb[kvh * rows_kv:(kvh + 1) * rows_kv], head_rows(v_ref, kvh))
        o_ref[s] = (jnp.concatenate(accs, axis=0) / l).astype(bf16)


def _dsa_sample(page_table, layer, q, iq, iwb, ikn, kn, vn, cache_idx_t, cache_k, cache_v,
                *, topk, n_tok, n_kv):
    nb, n_rows, T = q.shape
    n_pages = page_table.shape[1]
    kv_rows = cache_k.shape[2]
    idw = cache_idx_t.shape[2]
    ns = SEQ_PER_STEP if nb % SEQ_PER_STEP == 0 else 1
    seq = lambda shape: pl.BlockSpec((ns,) + shape, lambda b, pt: (b, 0, 0))

    def pages(shape):
        return [pl.BlockSpec((None, None) + shape, lambda b, pt, s=s, p=p: (layer, pt[ns * b + s, p], 0, 0))
                for s in range(ns) for p in range(n_pages)]

    in_specs = [seq((n_rows, T)), seq((n_rows, idw)), seq((n_rows, T)), seq((idw, T)),
                seq(kn.shape[1:]), seq(vn.shape[1:])]
    in_specs += pages((idw, T)) + pages((kv_rows, T)) + pages((kv_rows, T))
    nt = n_pages + 1
    grid_spec = pltpu.PrefetchScalarGridSpec(
        num_scalar_prefetch=1,
        grid=(nb // ns,),
        in_specs=in_specs,
        out_specs=seq((n_rows, T)),
        scratch_shapes=[pltpu.VMEM((nt, ns * n_tok, T), f32), pltpu.VMEM((ns, nt, n_rows, T), f32),
                        pltpu.VMEM((ns * n_tok, T), i32), pltpu.VMEM((ns * n_tok, T), f32)],
    )
    return pl.pallas_call(
        functools.partial(_dsa_sample_body, n_seq=ns, n_pages=n_pages, n_kv=n_kv, topk=topk, n_tok=n_tok,
                          idx_bits=int(nt * T).bit_length()),
        grid_spec=grid_spec,
        out_shape=jax.ShapeDtypeStruct((nb, n_rows, T), bf16),
        compiler_params=_params("parallel"),
        name="dsa_sample",
    )(page_table, q, iq, iwb, ikn, kn, vn,
      *([cache_idx_t] * (ns * n_pages)), *([cache_k] * (ns * n_pages)), *([cache_v] * (ns * n_pages)))


def _rope_tables(pos, dim):
    half = dim // 2
    inv = ROPE_THETA ** (-jnp.arange(half, dtype=f32) * (2.0 / dim))
    ang = pos.astype(f32)[:, None] * inv[None, :]
    cos = jnp.concatenate([jnp.cos(ang), jnp.cos(ang)], axis=1)
    sin = jnp.concatenate([-jnp.sin(ang), jnp.sin(ang)], axis=1)
    reps = LANES // dim
    return jnp.tile(cos, (1, reps)), jnp.tile(sin, (1, reps))


def _mix_tables(w_s, b_s, c, group_dim):
    causal = jnp.tril(jnp.ones((c, c), dtype=bool))
    ws = jnp.where(causal[None], w_s[:, :c, :c], 0)
    reps = LANES // c
    ws = jnp.tile(ws, (1, reps, reps))
    blk = jnp.arange(LANES) // c
    ws = jnp.where((blk[:, None] == blk[None, :])[None], ws, 0).astype(bf16)
    bias = jnp.tile(jnp.repeat(b_s[:, :c].T, group_dim, axis=1), (reps, 1)).astype(f32)
    return ws, bias


def kernel(x_prompt, x_sample, cache_k, cache_v, cache_idx_k, state_ffn_conv, page_table, a_norm, a_w_in, a_v_norm, a_w_s, a_b_s, a_w_out, b_norm, b_w_in, b_q_norm, b_k_norm, b_w_o, f_norm, f_w_in, f_conv_w, f_conv_b, f_w_out):
    bsz, seq, d = x_prompt.shape
    nb, n_tok, _ = x_sample.shape
    depth = f_norm.shape[0]
    n_layers_b, n_pool, page, n_kv, hd = cache_k.shape
    idw = cache_idx_k.shape[3]
    past = page_table.shape[1] * page
    n_heads = b_w_o.shape[1] // hd
    group = n_heads // n_kv
    n_ih = (b_w_in.shape[2] - (n_heads + 2 * n_kv) * hd - idw) // (idw + 1)
    ff = f_w_out.shape[1]
    a_groups, chunk = a_w_s.shape[1], a_w_s.shape[2]
    a_width = a_w_in.shape[2] // 2
    assert hd == LANES and idw == 64 and page == LANES and chunk == LANES and n_ih % 2 == 0
    assert f_conv_w.shape[1] == 3 and n_tok >= 2 and LANES % n_tok == 0 and n_tok & (n_tok - 1) == 0

    mp, ms = bsz * seq, nb * n_tok
    xp = x_prompt.reshape(mp, d)
    xs = x_sample.reshape(ms, d)
    tm_p = min(TM_ROWS, seq)
    tm_s = min(TM_ROWS, ms)
    tq = min(TQ_PROMPT, seq)

    aw_in, aw_out = a_w_in.astype(bf16), a_w_out.astype(bf16)
    n_real = b_w_in.shape[2]
    n_pad = -(-n_real // LANES) * LANES
    bw_in = jnp.pad(b_w_in, ((0, 0), (0, 0), (0, n_pad - n_real))).astype(bf16)
    bw_o = b_w_o.astype(bf16)
    fw_in, fw_out = f_w_in.astype(bf16), f_w_out.astype(bf16)
    tn_b = LANES * 11 if n_pad % (LANES * 11) == 0 else LANES
    tn_f = 512 if ff % 512 == 0 else LANES

    ck = cache_k.reshape(n_layers_b, n_pool, page * n_kv, hd)
    cv = cache_v.reshape(n_layers_b, n_pool, page * n_kv, hd)
    cidx_t = jnp.swapaxes(cache_idx_k, 2, 3)
    outs = dict(k_p=[], v_p=[], ik_p=[], k_s=[], v_s=[], ik_s=[], chunk_v=[], conv_p=[], conv_s=[])
    n_mixers = 2
    for layer in range(depth):
        j = layer // n_mixers
        if layer % n_mixers == 0:
            new = []
            for x, tm, c, want in ((xp, tm_p, min(seq, chunk), False), (xs, tm_s, min(n_tok, chunk), True)):
                wmix, bias = _mix_tables(a_w_s[j], a_b_s[j], c, a_width // a_groups)
                z = _norm_matmul(x, a_norm, aw_in, j, act="gelu", tm=min(TM_FFN, x.shape[0]), tn=512)
                gated, vn = _spatial_gate(z, a_v_norm, j, wmix, bias, tm=min(tm, 512), want_vn=want)
                new.append(_matmul_residual(gated, aw_out, j, x, tm=tm, tn=d))
                if want:
                    outs["chunk_v"].append(vn.reshape(nb, n_tok, a_width))
            xp, xs = new
        else:
            pos_p = jnp.arange(seq, dtype=i32)
            pos_s = jnp.tile(past + jnp.arange(n_tok, dtype=i32), nb)
            res = []
            for x, tm, pos in ((xp, tm_p, pos_p), (xs, tm_s, pos_s)):
                p = _norm_matmul(x, b_norm, bw_in, j, tm=min(TM_FFN, x.shape[0]), tn=tn_b)
                tabs = _rope_tables(pos, hd) + _rope_tables(pos, idw)
                res.append(_attn_post(p, b_q_norm, b_k_norm, j, tabs, tm=tm,
                                      n_heads=n_heads, n_kv=n_kv, n_ih=n_ih))
            q, k, v, kb, vb, iq, ikw, ik2 = res[0]
            r3 = lambda a: a.reshape(bsz, seq, a.shape[1])
            iwt = jnp.swapaxes(ikw[:, idw:idw + n_ih].reshape(bsz, seq, n_ih), 1, 2)
            bound = 1.01 * math.sqrt(hd) * jnp.max(jnp.abs(b_q_norm[j])) * jnp.max(jnp.abs(b_k_norm[j]))
            bound = jnp.full((1, LANES), bound, f32)
            o = _dsa_prompt(r3(q), r3(iq), iwt, r3(ik2), r3(kb), r3(vb), bound,
                            topk=min(TOPK_MAX, seq // 4), n_kv=n_kv, n_ih=n_ih, tq=tq)
            xp = _matmul_residual(o.reshape(mp, n_heads * hd), bw_o, j, xp, tm=tm_p, tn=d)
            n_seq_pages = seq // page
            outs["k_p"].append(k.reshape(bsz, n_seq_pages, page, n_kv, hd))
            outs["v_p"].append(v.reshape(bsz, n_seq_pages, page, n_kv, hd))
            outs["ik_p"].append(ikw[:, :idw].reshape(bsz, n_seq_pages, page, idw))

            q, k, v, kb, vb, iq, ikw, ik2 = res[1]
            hm = lambda a, w: a.reshape(nb, n_tok, a.shape[1] // w, w).transpose(0, 2, 1, 3).reshape(nb, -1, w)
            qs = hm(q, hd)
            iqs = hm(iq, idw)
            iw = ikw[:, idw:idw + n_ih].reshape(nb, n_tok, n_ih).transpose(0, 2, 1).reshape(nb, n_ih * n_tok, 1)
            iwb = jnp.broadcast_to(iw, (nb, n_ih * n_tok, LANES))
            pad_tok = lambda a: jnp.pad(a.reshape(nb, n_tok, -1), ((0, 0), (0, page - n_tok), (0, 0)))
            ikn = jnp.swapaxes(pad_tok(ikw[:, :idw]), 1, 2)
            kn = k.reshape(nb, n_tok * n_kv, hd)
            vn_ = v.reshape(nb, n_tok * n_kv, hd)
            os_ = _dsa_sample(page_table, j, qs, iqs, iwb, ikn, kn, vn_, cidx_t, ck, cv,
                              topk=min(TOPK_MAX, (past + n_tok) // 4), n_tok=n_tok, n_kv=n_kv)
            os_ = os_.reshape(nb, n_heads, n_tok, hd).transpose(0, 2, 1, 3).reshape(ms, n_heads * hd)
            xs = _matmul_residual(os_, bw_o, j, xs, tm=tm_s, tn=d)
            outs["k_s"].append(k.reshape(nb, n_tok, n_kv, hd))
            outs["v_s"].append(v.reshape(nb, n_tok, n_kv, hd))
            outs["ik_s"].append(ikw[:, :idw].reshape(nb, n_tok, idw))

        tm_f = min(TM_FFN, seq)
        xp, tg, tu = _conv_ffn(xp, f_norm, fw_in, f_conv_w, f_conv_b, fw_out, layer,
                               tm=tm_f, tn=tn_f, seq_len=seq)
        tails = jnp.concatenate([tg, tu], axis=2).reshape(bsz, seq // tm_f, 8, 2 * ff)
        outs["conv_p"].append(tails[:, -1, 6:8, :])
        st = state_ffn_conv[layer]
        st = jnp.pad(st, ((0, 0), (0, n_tok - 2), (0, 0))).reshape(ms, 2 * ff)
        xs, ng, nu = _conv_ffn(xs, f_norm, fw_in, f_conv_w, f_conv_b, fw_out, layer,
                               tm=tm_s, tn=tn_f, seq_len=n_tok, state=st)
        outs["conv_s"].append(jnp.concatenate([ng, nu], axis=2))

    st = lambda key: jnp.stack(outs[key])
    return (xp.reshape(bsz, seq, d), xs.reshape(nb, n_tok, d), st("k_p"), st("v_p"), st("ik_p"),
            st("k_s"), st("v_s"), st("ik_s"), st("chunk_v"), st("conv_p"), st("conv_s"))
```

```python
import functools
import math

import jax
import jax.numpy as jnp
from jax import lax
from jax.experimental import pallas as pl
from jax.experimental.pallas import tpu as pltpu

EPS = 1e-6
ROPE_THETA = 10000.0
TOPK_MAX = 256
LANES = 128
NEG_BIG = -1e30
INT_MIN = -(2 ** 31)
KEY_NEG_INF = -2139095041
TIE_ALL = 2 ** 30
SAFE_SHIFT_BOUND = 40.0
VMEM_LIMIT = 56 * 1024 * 1024
TM_ROWS = 512
TM_FFN = 1024
TQ_PROMPT = 256
SEQ_PER_STEP = 2

f32 = jnp.float32
bf16 = jnp.bfloat16
i32 = jnp.int32


def _params(*sem):
    return pltpu.CompilerParams(dimension_semantics=sem, vmem_limit_bytes=VMEM_LIMIT)


def _dot(a, b):
    return jnp.dot(a, b, preferred_element_type=f32)


def _dot_nt(a, b):
    return lax.dot_general(a, b, (((1,), (1,)), ((), ())), preferred_element_type=f32)


def _rms(x, g):
    r = lax.rsqrt(jnp.mean(x * x, axis=-1, keepdims=True) + EPS)
    return x * r * g


def _rows(a):
    return a.reshape(a.shape[0], 1, a.shape[1])


def _gelu_tanh(x):
    c = math.sqrt(2.0 / math.pi)
    return 0.5 * x * (1.0 + jnp.tanh(c * (x + 0.044715 * (x * x * x))))


def _nmm_body(x_ref, g_ref, w_ref, o_ref, h_ref, *, act):
    @pl.when(pl.program_id(1) == 0)
    def _():
        h_ref[...] = _rms(x_ref[...], g_ref[...]).astype(bf16)

    z = _dot(h_ref[...], w_ref[...])
    if act == "gelu":
        z = _gelu_tanh(z)
    o_ref[...] = z.astype(o_ref.dtype)


def _norm_matmul(x, gain, w, layer, *, act=None, tm, tn, out_dtype=f32):
    m, k = x.shape
    n = w.shape[2]
    return pl.pallas_call(
        functools.partial(_nmm_body, act=act),
        grid=(m // tm, n // tn),
        in_specs=[
            pl.BlockSpec((tm, k), lambda i, j: (i, 0)),
            pl.BlockSpec((None, 1, k), lambda i, j: (layer, 0, 0)),
            pl.BlockSpec((None, k, tn), lambda i, j: (layer, 0, j)),
        ],
        out_specs=pl.BlockSpec((tm, tn), lambda i, j: (i, j)),
        out_shape=jax.ShapeDtypeStruct((m, n), out_dtype),
        scratch_shapes=[pltpu.VMEM((tm, k), bf16)],
        compiler_params=_params("parallel", "arbitrary"),
        name="norm_matmul",
    )(x, _rows(gain), w)


def _mmres_body(a_ref, w_ref, x_ref, o_ref):
    o_ref[...] = x_ref[...] + _dot(a_ref[...], w_ref[...])


def _matmul_residual(a, w, layer, x, *, tm, tn):
    m, k = a.shape
    n = w.shape[2]
    return pl.pallas_call(
        _mmres_body,
        grid=(m // tm, n // tn),
        in_specs=[
            pl.BlockSpec((tm, k), lambda i, j: (i, 0)),
            pl.BlockSpec((None, k, tn), lambda i, j: (layer, 0, j)),
            pl.BlockSpec((tm, tn), lambda i, j: (i, j)),
        ],
        out_specs=pl.BlockSpec((tm, tn), lambda i, j: (i, j)),
        out_shape=jax.ShapeDtypeStruct((m, n), f32),
        compiler_params=_params("parallel", "arbitrary"),
        name="matmul_residual",
    )(a, w, x)


def _gate_body(u_ref, v_ref, vg_ref, wmix_ref, bias_ref, o_ref, *vn_out, tm, groups):
    vn = _rms(v_ref[...], vg_ref[...])
    if vn_out:
        vn_out[0][...] = vn
    vnb = vn.astype(bf16)
    gd = v_ref.shape[1] // groups
    for c in range(tm // LANES):
        rows = slice(c * LANES, (c + 1) * LANES)
        for g in range(groups):
            cols = slice(g * gd, (g + 1) * gd)
            s = _dot(wmix_ref[g], vnb[rows, cols]) + bias_ref[:, cols]
            o_ref[rows, cols] = (u_ref[rows, cols] * s).astype(bf16)


def _spatial_gate(z, v_gain, layer, wmix, bias, *, tm, want_vn):
    m, w2 = z.shape
    w = w2 // 2
    groups = wmix.shape[0]
    out_shape = [jax.ShapeDtypeStruct((m, w), bf16)]
    out_specs = [pl.BlockSpec((tm, w), lambda i: (i, 0))]
    if want_vn:
        out_shape.append(jax.ShapeDtypeStruct((m, w), f32))
        out_specs.append(pl.BlockSpec((tm, w), lambda i: (i, 0)))
    res = pl.pallas_call(
        functools.partial(_gate_body, tm=tm, groups=groups),
        grid=(m // tm,),
        in_specs=[
            pl.BlockSpec((tm, w), lambda i: (i, 0)),
            pl.BlockSpec((tm, w), lambda i: (i, 1)),
            pl.BlockSpec((None, 1, w), lambda i: (layer, 0, 0)),
            pl.BlockSpec(wmix.shape, lambda i: (0, 0, 0)),
            pl.BlockSpec(bias.shape, lambda i: (0, 0)),
        ],
        out_specs=out_specs,
        out_shape=out_shape,
        compiler_params=_params("parallel"),
        name="spatial_gate",
    )(z, z, _rows(v_gain), wmix, bias)
    return res if want_vn else (res[0], None)


def _conv3(a, cw, cb, prev1, prev2):
    return cb + cw[0:1] * prev2 + cw[1:2] * prev1 + cw[2:3] * a


def _ffn_body(x_ref, g_ref, wg_ref, wu_ref, cwg_ref, cwu_ref, cbg_ref, cbu_ref, wo_ref, *rest,
              sample, seq_tiles, seq_len, nj):
    if sample:
        sg_ref, su_ref, o_ref, ag_ref, au_ref, h_ref, act0_ref, act1_ref = rest
    else:
        o_ref, tg_ref, tu_ref, h_ref, act0_ref, act1_ref, carry_ref = rest
    acts = (act0_ref, act1_ref)
    i = pl.program_id(0)
    j = pl.program_id(1)

    @pl.when(j == 0)
    def _():
        x = x_ref[...]
        h_ref[...] = _rms(x, g_ref[...]).astype(bf16)
        o_ref[...] = x
        act1_ref[...] = jnp.zeros(act1_ref.shape, bf16)

    def down(src_ref):
        o_ref[...] += _dot(src_ref[...], wo_ref[...])

    def up_conv_gate(dst_ref):
        h = h_ref[...]
        ag = _dot(h, wg_ref[...])
        au = _dot(h, wu_ref[...])
        tm, tn = ag.shape
        row = lax.broadcasted_iota(i32, (tm, tn), 0)
        if sample:
            t = row & (seq_len - 1)
            m1 = t == 0
            m2 = t < 2
            p2g, p2u = sg_ref[...], su_ref[...]
            p1g, p1u = pltpu.roll(p2g, tm - 1, 0), pltpu.roll(p2u, tm - 1, 0)
            for a, a_ref in ((ag, ag_ref), (au, au_ref)):
                last = pltpu.roll(a, tm - (seq_len - 2), 0).reshape(tm // seq_len, seq_len, tn)
                a_ref[...] = last[:, :2, :]
        else:
            m1 = row == 0
            m2 = row < 2
            live = (i % seq_tiles) != 0
            cg = jnp.where(live, carry_ref[j, 0], 0.0)
            cu = jnp.where(live, carry_ref[j, 1], 0.0)
            p1g, p1u = cg[7:8], cu[7:8]
            p2g = jnp.where(m1, cg[6:7], cg[7:8])
            p2u = jnp.where(m1, cu[6:7], cu[7:8])
            carry_ref[j, 0] = ag[tm - 8:]
            carry_ref[j, 1] = au[tm - 8:]
            tg_ref[0] = ag[tm - 8:]
            tu_ref[0] = au[tm - 8:]
        a1g = jnp.where(m1, p1g, pltpu.roll(ag, 1, 0))
        a2g = jnp.where(m2, p2g, pltpu.roll(ag, 2, 0))
        a1u = jnp.where(m1, p1u, pltpu.roll(au, 1, 0))
        a2u = jnp.where(m2, p2u, pltpu.roll(au, 2, 0))
        cg_ = _conv3(ag, cwg_ref[...], cbg_ref[...], a1g, a2g)
        cu_ = _conv3(au, cwu_ref[...], cbu_ref[...], a1u, a2u)
        dst_ref[...] = (cg_ * (1.0 / (1.0 + jnp.exp(-cg_))) * cu_).astype(bf16)

    for parity in (0, 1):
        @pl.when((j < nj) & ((j & 1) == parity))
        def _(parity=parity):
            down(acts[1 - parity])
            up_conv_gate(acts[parity])

    @pl.when(j == nj)
    def _():
        down(acts[(nj - 1) % 2])


def _conv_ffn(x, gain, w_in, conv_w, conv_b, w_out, layer, *, tm, tn, seq_len, state=None):
    m, d = x.shape
    ff = w_out.shape[1]
    nj = ff // tn
    ni = m // tm
    sample = state is not None
    up = lambda j: jnp.minimum(j, nj - 1)
    down = lambda j: jnp.maximum(j - 1, 0)
    in_specs = [
        pl.BlockSpec((tm, d), lambda i, j: (i, 0), pipeline_mode=pl.Buffered(1)),
        pl.BlockSpec((None, 1, d), lambda i, j: (layer, 0, 0)),
        pl.BlockSpec((None, d, tn), lambda i, j: (layer, 0, up(j))),
        pl.BlockSpec((None, d, tn), lambda i, j: (layer, 0, nj + up(j))),
        pl.BlockSpec((None, 3, tn), lambda i, j: (layer, 0, up(j))),
        pl.BlockSpec((None, 3, tn), lambda i, j: (layer, 0, nj + up(j))),
        pl.BlockSpec((None, 1, tn), lambda i, j: (layer, 0, up(j))),
        pl.BlockSpec((None, 1, tn), lambda i, j: (layer, 0, nj + up(j))),
        pl.BlockSpec((None, tn, d), lambda i, j: (layer, down(j), 0)),
    ]
    conv_b = _rows(conv_b)
    args = [x, _rows(gain), w_in, w_in, conv_w, conv_w, conv_b, conv_b, w_out]
    out_specs = [pl.BlockSpec((tm, d), lambda i, j: (i, 0), pipeline_mode=pl.Buffered(1 if ni == 1 else 2))]
    out_shape = [jax.ShapeDtypeStruct((m, d), f32)]
    scratch = [pltpu.VMEM((tm, d), bf16), pltpu.VMEM((tm, tn), bf16), pltpu.VMEM((tm, tn), bf16)]
    if sample:
        in_specs += [
            pl.BlockSpec((tm, tn), lambda i, j: (i, up(j))),
            pl.BlockSpec((tm, tn), lambda i, j: (i, nj + up(j))),
        ]
        args += [state, state]
        out_specs += [pl.BlockSpec((tm // seq_len, 2, tn), lambda i, j: (i, 0, up(j)))] * 2
        out_shape += [jax.ShapeDtypeStruct((m // seq_len, 2, ff), f32)] * 2
        seq_tiles = 1
    else:
        out_specs += [pl.BlockSpec((1, 8, tn), lambda i, j: (i, 0, up(j)))] * 2
        out_shape += [jax.ShapeDtypeStruct((ni, 8, ff), f32)] * 2
        scratch.append(pltpu.VMEM((nj, 2, 8, tn), f32))
        seq_tiles = seq_len // tm
    return pl.pallas_call(
        functools.partial(_ffn_body, sample=sample, seq_tiles=seq_tiles, seq_len=seq_len, nj=nj),
        grid=(ni, nj + 1),
        in_specs=in_specs,
        out_specs=out_specs,
        out_shape=out_shape,
        scratch_shapes=scratch,
        compiler_params=_params("arbitrary", "arbitrary"),
        name="conv_ffn_sample" if sample else "conv_ffn_prompt",
    )(*args)


def _rope_full(x, c, s):
    return x * c + pltpu.roll(x, LANES // 2, 1) * s


def _rope_half(x, c, s, lo):
    partner = jnp.where(lo, pltpu.roll(x, 96, 1), pltpu.roll(x, 32, 1))
    return x * c + partner * s


def _post_body(p_ref, qg_ref, kg_ref, c1_ref, s1_ref, c2_ref, s2_ref,
               q_ref, k_ref, v_ref, kb_ref, vb_ref, iq_ref, ikw_ref, ik2_ref,
               *, n_heads, n_kv, n_ih, idx_scale, q_scale):
    hd = LANES
    c1, s1, c2, s2 = c1_ref[...], s1_ref[...], c2_ref[...], s2_ref[...]
    tm = p_ref.shape[0]
    lane = lax.broadcasted_iota(i32, (tm, LANES), 1)
    lo = (lane & 63) < 32
    off = 0
    for h in range(n_heads):
        x = p_ref[:, off + h * hd: off + (h + 1) * hd]
        q_ref[:, h * hd:(h + 1) * hd] = (_rope_full(_rms(x, qg_ref[...]), c1, s1) * q_scale).astype(bf16)
    off += n_heads * hd
    for h in range(n_kv):
        x = p_ref[:, off + h * hd: off + (h + 1) * hd]
        kr = _rope_full(_rms(x, kg_ref[...]), c1, s1)
        k_ref[pl.ds(h, tm, stride=n_kv), :] = kr
        kb_ref[:, h * hd:(h + 1) * hd] = kr.astype(bf16)
    off += n_kv * hd
    vv = p_ref[:, off: off + n_kv * hd]
    for h in range(n_kv):
        v_ref[pl.ds(h, tm, stride=n_kv), :] = vv[:, h * hd:(h + 1) * hd]
    vb_ref[...] = vv.astype(bf16)
    off += n_kv * hd
    for h in range(n_ih // 2):
        x = p_ref[:, off + h * LANES: off + (h + 1) * LANES]
        iq_ref[:, h * LANES:(h + 1) * LANES] = _rope_half(x, c2, s2, lo).astype(bf16)
    off += (n_ih // 2) * LANES
    tail = p_ref[:, off: off + LANES]
    tr = _rope_half(tail, c2, s2, lo)
    ik = jnp.where(lane < 64, tr, 0.0)
    ik2_ref[...] = (ik + pltpu.roll(ik, 64, 1)).astype(bf16)
    ikw_ref[...] = jnp.where(lane < 64, tr, tail * idx_scale)


def _attn_post(p, q_gain, k_gain, layer, tabs, *, tm, n_heads, n_kv, n_ih):
    m, npad = p.shape
    c1, s1, c2, s2 = tabs
    nt = c1.shape[0] // tm
    hd = LANES
    tab_spec = pl.BlockSpec((tm, LANES), lambda i: (i % nt, 0))
    row = lambda w: pl.BlockSpec((tm, w), lambda i: (i, 0))
    gain_spec = pl.BlockSpec((None, 1, hd), lambda i: (layer, 0, 0))
    outs = [
        (1, n_heads * hd, bf16), (n_kv, hd, f32), (n_kv, hd, f32), (1, n_kv * hd, bf16), (1, n_kv * hd, bf16),
        (1, n_ih * 64, bf16), (1, LANES, f32), (1, LANES, bf16),
    ]
    return pl.pallas_call(
        functools.partial(_post_body, n_heads=n_heads, n_kv=n_kv, n_ih=n_ih,
                          idx_scale=float((n_ih * 64) ** -0.5), q_scale=float(hd ** -0.5)),
        grid=(m // tm,),
        in_specs=[row(npad), gain_spec, gain_spec, tab_spec, tab_spec, tab_spec, tab_spec],
        out_specs=[pl.BlockSpec((tm * r, w), lambda i: (i, 0)) for r, w, _ in outs],
        out_shape=[jax.ShapeDtypeStruct((m * r, w), dt) for r, w, dt in outs],
        compiler_params=_params("parallel"),
        name="attn_post",
    )(p, _rows(q_gain), _rows(k_gain), c1, s1, c2, s2)


def _key_to_float(key):
    bits = key ^ ((key >> 31) & 0x7FFFFFFF)
    return lax.bitcast_convert_type(bits, f32)


def _select_threshold(count_fn, shape, topk, bits_per_round):
    kk = float(topk)
    n_cand = (1 << bits_per_round) - 1

    def round_body(it, res):
        step = lax.shift_left(jnp.int32(1), 32 - bits_per_round * (it + 1))
        cands = [res + step * (c + 1) for c in range(n_cand)]
        floats = [_key_to_float(c) for c in cands]
        cnts = count_fn(tuple((lambda sc, idx, cf=cf: sc >= cf) for cf in floats))
        for c, n in zip(cands, cnts):
            res = jnp.where(n >= kk, c, res)
        return res

    res = lax.fori_loop(0, 32 // bits_per_round, round_body, jnp.full(shape, INT_MIN, i32))
    return _key_to_float(jnp.maximum(res, KEY_NEG_INF))


def _tie_search(count_fn, thr, need, shape, idx_bits):
    def tie_body(it, end):
        cand = end + lax.shift_left(jnp.int32(1), idx_bits - 1 - it)
        (cnt,) = count_fn((lambda sc, idx: (sc == thr) & (idx < cand),))
        return jnp.where(cnt <= need, cand, end)

    return lax.fori_loop(0, idx_bits, tie_body, jnp.zeros(shape, i32))


def _select_topk(count_fn, tie_ref, shape, topk, idx_bits, bits_per_round):
    kk = float(topk)
    thr = _select_threshold(count_fn, shape, topk, bits_per_round)
    n_gt, n_ge = count_fn((lambda sc, idx: sc > thr, lambda sc, idx: sc >= thr))
    tie_ref[...] = jnp.full(shape, TIE_ALL, i32)
    crowded = jnp.where((n_ge > kk) & (thr > -jnp.inf), 1.0, 0.0)

    @pl.when(jnp.max(crowded) > 0.0)
    def _():
        tie_ref[...] = _tie_search(count_fn, thr, kk - n_gt, shape, idx_bits)

    return thr, tie_ref[...]


def _dsa_prompt_body(q_ref, iq_ref, iwt_ref, ik2_ref, kb_ref, vb_ref, bound_ref, o_ref,
                     sc_ref, iqm_ref, qg_ref, tie_ref, m_ref, acc_ref,
                     *, topk, n_kv, group, n_ih, idx_bits, tq):
    T = LANES
    nch = tq // T
    strips = tq // 8
    qb = pl.program_id(1)
    nkt = qb + 1
    lane = lax.broadcasted_iota(i32, (tq, T), 1)

    for h in range(n_ih):
        pair = iq_ref[0, :, (h // 2) * T:(h // 2 + 1) * T].astype(f32)
        keep = (lane < 64) if h % 2 == 0 else (lane >= 64)
        iqm_ref[h] = jnp.where(keep, pair, 0.0).astype(bf16)
    for h in range(n_kv * group):
        qg_ref[h // group, (h % group) * tq:(h % group + 1) * tq, :] = q_ref[0, :, h * T:(h + 1) * T]

    def key_rows(kt):
        return pl.ds(pl.multiple_of(kt * tq, tq), tq)

    key_in_tile = (lax.broadcasted_iota(i32, (strips, 8, tq), 0) * 8
                   + lax.broadcasted_iota(i32, (strips, 8, tq), 1))
    qidx = qb * tq + lax.broadcasted_iota(i32, (strips, 8, tq), 2)

    def score_tile(kt, carry):
        ik_t = ik2_ref[0, key_rows(kt), :]
        acc = jnp.zeros((tq, tq), f32)
        for h in range(n_ih):
            acc = acc + jnp.maximum(_dot_nt(ik_t, iqm_ref[h]), 0.0) * iwt_ref[0, h:h + 1, :]
        admissible = kt * tq + key_in_tile <= qidx
        sc_ref[kt] = jnp.where(admissible, acc.reshape(strips, 8, tq), -jnp.inf).reshape(tq, tq)
        return carry

    lax.fori_loop(0, nkt, score_tile, 0)

    def count_fn(preds):
        def body(kt, accs):
            sc = sc_ref[kt].reshape(strips, 8, tq)
            kidx = kt * tq + key_in_tile
            return tuple(a + jnp.sum(jnp.where(pred(sc, kidx), 1.0, 0.0), axis=0)
                         for a, pred in zip(accs, preds))

        accs = lax.fori_loop(0, nkt, body, tuple(jnp.zeros((8, tq), f32) for _ in preds))
        return tuple(jnp.broadcast_to(jnp.sum(a, axis=0, keepdims=True), (8, tq)) for a in accs)

    thr, tie_end = _select_topk(count_fn, tie_ref, (8, tq), topk, idx_bits, 1)

    def bias_tile(kt, carry):
        sc = sc_ref[kt].reshape(strips, 8, tq)
        kidx = kt * tq + key_in_tile
        sel = ((sc > thr) | ((sc == thr) & (kidx < tie_end))) & (kidx <= qidx)
        sc_ref[kt] = jnp.where(sel, 0.0, NEG_BIG).reshape(tq, tq).T
        return carry

    lax.fori_loop(0, nkt, bias_tile, 0)

    gq = group * tq

    def scores(kt, kvh):
        k_t = kb_ref[0, key_rows(kt), kvh * T:(kvh + 1) * T]
        s = _dot_nt(qg_ref[kvh], k_t)
        return (s.reshape(group, tq, tq) + sc_ref[kt][None]).reshape(gq, tq)

    bound = bound_ref[...]
    m_ref[...] = jnp.broadcast_to(bound[None], m_ref.shape)

    @pl.when(jnp.max(bound) > SAFE_SHIFT_BOUND)
    def _():
        m_ref[...] = jnp.full(m_ref.shape, NEG_BIG, f32)

        def max_tile(kt, carry):
            for kvh in range(n_kv):
                s = scores(kt, kvh)
                mx = m_ref[kvh]
                for c in range(nch):
                    mx = jnp.maximum(mx, s[:, c * T:(c + 1) * T])
                m_ref[kvh] = mx
            return carry

        lax.fori_loop(0, nkt, max_tile, 0)
        for kvh in range(n_kv):
            m_ref[kvh] = jnp.broadcast_to(jnp.max(m_ref[kvh], axis=1, keepdims=True), (gq, T))

    acc_ref[...] = jnp.zeros(acc_ref.shape, f32)
    ones = jnp.ones((tq, T), bf16)

    def pv_tile(kt, carry):
        for kvh in range(n_kv):
            s = scores(kt, kvh)
            mx = m_ref[kvh]
            p = jnp.concatenate([jnp.exp(s[:, c * T:(c + 1) * T] - mx) for c in range(nch)], axis=1)
            v1 = jnp.concatenate([vb_ref[0, key_rows(kt), kvh * T:(kvh + 1) * T], ones], axis=1)
            acc_ref[kvh] += _dot(p.astype(bf16), v1)
        return carry

    lax.fori_loop(0, nkt, pv_tile, 0)
    for kvh in range(n_kv):
        o = acc_ref[kvh, :, :T] / acc_ref[kvh, :, T:]
        for g in range(group):
            h = kvh * group + g
            o_ref[0, :, h * T:(h + 1) * T] = o[g * tq:(g + 1) * tq].astype(bf16)


def _dsa_prompt(q, iq, iwt, ik2, kb, vb, bound, *, topk, n_kv, n_ih, tq):
    b, s, qd = q.shape
    T = LANES
    n_heads = qd // T
    group = n_heads // n_kv
    nq = s // tq
    blk = lambda w: pl.BlockSpec((1, tq, w), lambda bi, qi: (bi, qi, 0))
    full = lambda w: pl.BlockSpec((1, s, w), lambda bi, qi: (bi, 0, 0))
    att = pltpu.VMEM((n_kv, group * tq, T), f32)
    return pl.pallas_call(
        functools.partial(_dsa_prompt_body, topk=topk, n_kv=n_kv, group=group, n_ih=n_ih,
                          idx_bits=int(s).bit_length(), tq=tq),
        grid=(b, nq),
        in_specs=[blk(qd), blk(iq.shape[2]), pl.BlockSpec((1, n_ih, tq), lambda bi, qi: (bi, 0, qi)),
                  full(T), full(n_kv * T), full(n_kv * T), pl.BlockSpec((1, T), lambda bi, qi: (0, 0))],
        out_specs=blk(qd),
        out_shape=jax.ShapeDtypeStruct((b, s, qd), bf16),
        scratch_shapes=[
            pltpu.VMEM((nq, tq, tq), f32),
            pltpu.VMEM((n_ih, tq, T), bf16),
            pltpu.VMEM((n_kv, group * tq, T), bf16), pltpu.VMEM((8, tq), i32),
            att, pltpu.VMEM((n_kv, group * tq, 2 * T), f32),
        ],
        compiler_params=_params("parallel", "arbitrary"),
        name="dsa_prompt",
    )(q, iq, iwt, ik2, kb, vb, bound)


def _dsa_sample_body(pt_ref, q_ref, iq_ref, iwb_ref, ikn_ref, kn_ref, vn_ref, *rest,
                     n_seq, n_pages, n_kv, topk, n_tok, idx_bits):
    del pt_ref
    T = LANES
    pages = [rest[i * n_pages:(i + 1) * n_pages] for i in range(3 * n_seq)]
    idx_pages, k_pages, v_pages = pages[:n_seq], pages[n_seq:2 * n_seq], pages[2 * n_seq:]
    o_ref, sc_ref, s_ref, tie_ref, thr_ref = rest[3 * n_seq * n_pages:]
    nt = n_pages + 1
    n_rows = q_ref.shape[1]
    sel_rows = n_seq * n_tok
    lane = lax.broadcasted_iota(i32, (sel_rows, T), 1)
    tok = lax.broadcasted_iota(i32, (sel_rows, T), 0) & (n_tok - 1)
    fresh = (lane <= tok) & (lane < n_tok)

    for s in range(n_seq):
        iq = iq_ref[s]
        iwb = iwb_ref[s]
        for p in range(nt):
            ikt = (idx_pages[s][p][...] if p < n_pages else ikn_ref[s]).astype(bf16)
            w = jnp.maximum(_dot(iq, ikt), 0.0) * iwb
            sc_ref[p, s * n_tok:(s + 1) * n_tok, :] = jnp.sum(w.reshape(n_rows // n_tok, n_tok, T), axis=0)
    sc_ref[n_pages] = jnp.where(fresh, sc_ref[n_pages], -jnp.inf)

    def count_fn(preds):
        accs = [jnp.zeros((sel_rows, T), f32) for _ in preds]
        for p in range(nt):
            sc = sc_ref[p]
            for n, pred in enumerate(preds):
                accs[n] = accs[n] + jnp.where(pred(sc, p * T + lane), 1.0, 0.0)
        return tuple(jnp.broadcast_to(jnp.sum(a, axis=1, keepdims=True), (sel_rows, T)) for a in accs)

    thr, _ = _select_topk(count_fn, tie_ref, (sel_rows, T), topk, idx_bits, 4)
    thr_ref[...] = thr

    lane_s = lax.broadcasted_iota(i32, (n_tok, T), 1)
    fresh_s = (lane_s <= lax.broadcasted_iota(i32, (n_tok, T), 0)) & (lane_s < n_tok)
    rows_kv = n_rows // n_kv

    def head_rows(ref, kvh):
        n_keys = ref.shape[0] // n_kv
        rows = ref[pl.ds(kvh, n_keys, stride=n_kv), :]
        if n_keys < T:
            rows = jnp.concatenate([rows, jnp.zeros((T - n_keys, T), f32)], axis=0)
        return rows.astype(bf16)

    for s in range(n_seq):
        q = q_ref[s]
        toks = slice(s * n_tok, (s + 1) * n_tok)
        thr_s, tie_s = thr_ref[toks, :], tie_ref[toks, :]
        mx = jnp.full((n_rows, T), NEG_BIG, f32)
        for p in range(nt):
            sc = sc_ref[p, toks, :]
            sel = (sc > thr_s) | ((sc == thr_s) & (p * T + lane_s < tie_s))
            if p == n_pages:
                sel = sel & fresh_s
            bias = jnp.concatenate([jnp.where(sel, 0.0, NEG_BIG)] * (rows_kv // n_tok), axis=0)
            k_ref = k_pages[s][p] if p < n_pages else kn_ref.at[s]
            sm = jnp.concatenate(
                [_dot_nt(q[kvh * rows_kv:(kvh + 1) * rows_kv], head_rows(k_ref, kvh)) + bias
                 for kvh in range(n_kv)], axis=0)
            s_ref[s, p] = sm
            mx = jnp.maximum(mx, sm)
        m = jnp.max(mx, axis=1, keepdims=True)
        l = jnp.zeros((n_rows, 1), f32)
        accs = [jnp.zeros((rows_kv, T), f32) for _ in range(n_kv)]
        for p in range(nt):
            e = jnp.exp(s_ref[s, p] - m)
            l = l + jnp.sum(e, axis=1, keepdims=True)
            eb = e.astype(bf16)
            v_ref = v_pages[s][p] if p < n_pages else vn_ref.at[s]
            for kvh in range(n_kv):
                accs[kvh] = accs[kvh] + _dot(eb[kvh * rows_kv:(kvh + 1) * rows_kv], head_rows(v_ref, kvh))
        o_ref[s] = (jnp.concatenate(accs, axis=0) / l).astype(bf16)


def _dsa_sample(page_table, layer, q, iq, iwb, ikn, kn, vn, cache_idx_t, cache_k, cache_v,
                *, topk, n_tok, n_kv):
    nb, n_rows, T = q.shape
    n_pages = page_table.shape[1]
    kv_rows = cache_k.shape[2]
    idw = cache_idx_t.shape[2]
    ns = SEQ_PER_STEP if nb % SEQ_PER_STEP == 0 else 1
    seq = lambda shape: pl.BlockSpec((ns,) + shape, lambda b, pt: (b, 0, 0))

    def pages(shape):
        return [pl.BlockSpec((None, None) + shape, lambda b, pt, s=s, p=p: (layer, pt[ns * b + s, p], 0, 0))
                for s in range(ns) for p in range(n_pages)]

    in_specs = [seq((n_rows, T)), seq((n_rows, idw)), seq((n_rows, T)), seq((idw, T)),
                seq(kn.shape[1:]), seq(vn.shape[1:])]
    in_specs += pages((idw, T)) + pages((kv_rows, T)) + pages((kv_rows, T))
    nt = n_pages + 1
    grid_spec = pltpu.PrefetchScalarGridSpec(
        num_scalar_prefetch=1,
        grid=(nb // ns,),
        in_specs=in_specs,
        out_specs=seq((n_rows, T)),
        scratch_shapes=[pltpu.VMEM((nt, ns * n_tok, T), f32), pltpu.VMEM((ns, nt, n_rows, T), f32),
                        pltpu.VMEM((ns * n_tok, T), i32), pltpu.VMEM((ns * n_tok, T), f32)],
    )
    return pl.pallas_call(
        functools.partial(_dsa_sample_body, n_seq=ns, n_pages=n_pages, n_kv=n_kv, topk=topk, n_tok=n_tok,
                          idx_bits=int(nt * T).bit_length()),
        grid_spec=grid_spec,
        out_shape=jax.ShapeDtypeStruct((nb, n_rows, T), bf16),
        compiler_params=_params("parallel"),
        name="dsa_sample",
    )(page_table, q, iq, iwb, ikn, kn, vn,
      *([cache_idx_t] * (ns * n_pages)), *([cache_k] * (ns * n_pages)), *([cache_v] * (ns * n_pages)))


def _rope_tables(pos, dim):
    half = dim // 2
    inv = ROPE_THETA ** (-jnp.arange(half, dtype=f32) * (2.0 / dim))
    ang = pos.astype(f32)[:, None] * inv[None, :]
    cos = jnp.concatenate([jnp.cos(ang), jnp.cos(ang)], axis=1)
    sin = jnp.concatenate([-jnp.sin(ang), jnp.sin(ang)], axis=1)
    reps = LANES // dim
    return jnp.tile(cos, (1, reps)), jnp.tile(sin, (1, reps))


def _mix_tables(w_s, b_s, c, group_dim):
    causal = jnp.tril(jnp.ones((c, c), dtype=bool))
    ws = jnp.where(causal[None], w_s[:, :c, :c], 0)
    reps = LANES // c
    ws = jnp.tile(ws, (1, reps, reps))
    blk = jnp.arange(LANES) // c
    ws = jnp.where((blk[:, None] == blk[None, :])[None], ws, 0).astype(bf16)
    bias = jnp.tile(jnp.repeat(b_s[:, :c].T, group_dim, axis=1), (reps, 1)).astype(f32)
    return ws, bias


def kernel(x_prompt, x_sample, cache_k, cache_v, cache_idx_k, state_ffn_conv, page_table, a_norm, a_w_in, a_v_norm, a_w_s, a_b_s, a_w_out, b_norm, b_w_in, b_q_norm, b_k_norm, b_w_o, f_norm, f_w_in, f_conv_w, f_conv_b, f_w_out):
    bsz, seq, d = x_prompt.shape
    nb, n_tok, _ = x_sample.shape
    depth = f_norm.shape[0]
    n_layers_b, n_pool, page, n_kv, hd = cache_k.shape
    idw = cache_idx_k.shape[3]
    past = page_table.shape[1] * page
    n_heads = b_w_o.shape[1] // hd
    group = n_heads // n_kv
    n_ih = (b_w_in.shape[2] - (n_heads + 2 * n_kv) * hd - idw) // (idw + 1)
    ff = f_w_out.shape[1]
    a_groups, chunk = a_w_s.shape[1], a_w_s.shape[2]
    a_width = a_w_in.shape[2] // 2
    assert hd == LANES and idw == 64 and page == LANES and chunk == LANES and n_ih % 2 == 0
    assert f_conv_w.shape[1] == 3 and n_tok >= 2 and LANES % n_tok == 0 and n_tok & (n_tok - 1) == 0

    mp, ms = bsz * seq, nb * n_tok
    xp = x_prompt.reshape(mp, d)
    xs = x_sample.reshape(ms, d)
    tm_p = min(TM_ROWS, seq)
    tm_s = min(TM_ROWS, ms)
    tq = min(TQ_PROMPT, seq)

    aw_in, aw_out = a_w_in.astype(bf16), a_w_out.astype(bf16)
    n_real = b_w_in.shape[2]
    n_pad = -(-n_real // LANES) * LANES
    bw_in = jnp.pad(b_w_in, ((0, 0), (0, 0), (0, n_pad - n_real))).astype(bf16)
    bw_o = b_w_o.astype(bf16)
    fw_in, fw_out = f_w_in.astype(bf16), f_w_out.astype(bf16)
    tn_b = LANES * 11 if n_pad % (LANES * 11) == 0 else LANES
    tn_f = 512 if ff % 512 == 0 else LANES

    ck = cache_k.reshape(n_layers_b, n_pool, page * n_kv, hd)
    cv = cache_v.reshape(n_layers_b, n_pool, page * n_kv, hd)
    cidx_t = jnp.swapaxes(cache_idx_k, 2, 3)
    outs = dict(k_p=[], v_p=[], ik_p=[], k_s=[], v_s=[], ik_s=[], chunk_v=[], conv_p=[], conv_s=[])
    n_mixers = 2
    for layer in range(depth):
        j = layer // n_mixers
        if layer % n_mixers == 0:
            new = []
            for x, tm, c, want in ((xp, tm_p, min(seq, chunk), False), (xs, tm_s, min(n_tok, chunk), True)):
                wmix, bias = _mix_tables(a_w_s[j], a_b_s[j], c, a_width // a_groups)
                z = _norm_matmul(x, a_norm, aw_in, j, act="gelu", tm=min(TM_FFN, x.shape[0]), tn=512)
                gated, vn = _spatial_gate(z, a_v_norm, j, wmix, bias, tm=min(tm, 512), want_vn=want)
                new.append(_matmul_residual(gated, aw_out, j, x, tm=tm, tn=d))
                if want:
                    outs["chunk_v"].append(vn.reshape(nb, n_tok, a_width))
            xp, xs = new
        else:
            pos_p = jnp.arange(seq, dtype=i32)
            pos_s = jnp.tile(past + jnp.arange(n_tok, dtype=i32), nb)
            res = []
            for x, tm, pos in ((xp, tm_p, pos_p), (xs, tm_s, pos_s)):
                p = _norm_matmul(x, b_norm, bw_in, j, tm=min(TM_FFN, x.shape[0]), tn=tn_b)
                tabs = _rope_tables(pos, hd) + _rope_tables(pos, idw)
                res.append(_attn_post(p, b_q_norm, b_k_norm, j, tabs, tm=tm,
                                      n_heads=n_heads, n_kv=n_kv, n_ih=n_ih))
            q, k, v, kb, vb, iq, ikw, ik2 = res[0]
            r3 = lambda a: a.reshape(bsz, seq, a.shape[1])
            iwt = jnp.swapaxes(ikw[:, idw:idw + n_ih].reshape(bsz, seq, n_ih), 1, 2)
            bound = 1.01 * math.sqrt(hd) * jnp.max(jnp.abs(b_q_norm[j])) * jnp.max(jnp.abs(b_k_norm[j]))
            bound = jnp.full((1, LANES), bound, f32)
            o = _dsa_prompt(r3(q), r3(iq), iwt, r3(ik2), r3(kb), r3(vb), bound,
                            topk=min(TOPK_MAX, seq // 4), n_kv=n_kv, n_ih=n_ih, tq=tq)
            xp = _matmul_residual(o.reshape(mp, n_heads * hd), bw_o, j, xp, tm=tm_p, tn=d)
            n_seq_pages = seq // page
            outs["k_p"].append(k.reshape(bsz, n_seq_pages, page, n_kv, hd))
            outs["v_p"].append(v.reshape(bsz, n_seq_pages, page, n_kv, hd))
            outs["ik_p"].append(ikw[:, :idw].reshape(bsz, n_seq_pages, page, idw))

            q, k, v, kb, vb, iq, ikw, ik2 = res[1]
            hm = lambda a, w: a.reshape(nb, n_tok, a.shape[1] // w, w).transpose(0, 2, 1, 3).reshape(nb, -1, w)
            qs = hm(q, hd)
            iqs = hm(iq, idw)
            iw = ikw[:, idw:idw + n_ih].reshape(nb, n_tok, n_ih).transpose(0, 2, 1).reshape(nb, n_ih * n_tok, 1)
            iwb = jnp.broadcast_to(iw, (nb, n_ih * n_tok, LANES))
            pad_tok = lambda a: jnp.pad(a.reshape(nb, n_tok, -1), ((0, 0), (0, page - n_tok), (0, 0)))
            ikn = jnp.swapaxes(pad_tok(ikw[:, :idw]), 1, 2)
            kn = k.reshape(nb, n_tok * n_kv, hd)
            vn_ = v.reshape(nb, n_tok * n_kv, hd)
            os_ = _dsa_sample(page_table, j, qs, iqs, iwb, ikn, kn, vn_, cidx_t, ck, cv,
                              topk=min(TOPK_MAX, (past + n_tok) // 4), n_tok=n_tok, n_kv=n_kv)
            os_ = os_.reshape(nb, n_heads, n_tok, hd).transpose(0, 2, 1, 3).reshape(ms, n_heads * hd)
            xs = _matmul_residual(os_, bw_o, j, xs, tm=tm_s, tn=d)
            outs["k_s"].append(k.reshape(nb, n_tok, n_kv, hd))
            outs["v_s"].append(v.reshape(nb, n_tok, n_kv, hd))
            outs["ik_s"].append(ikw[:, :idw].reshape(nb, n_tok, idw))

        tm_f = min(TM_FFN, seq)
        xp, tg, tu = _conv_ffn(xp, f_norm, fw_in, f_conv_w, f_conv_b, fw_out, layer,
                               tm=tm_f, tn=tn_f, seq_len=seq)
        tails = jnp.concatenate([tg, tu], axis=2).reshape(bsz, seq // tm_f, 8, 2 * ff)
        outs["conv_p"].append(tails[:, -1, 6:8, :])
        st = state_ffn_conv[layer]
        st = jnp.pad(st, ((0, 0), (0, n_tok - 2), (0, 0))).reshape(ms, 2 * ff)
        xs, ng, nu = _conv_ffn(xs, f_norm, fw_in, f_conv_w, f_conv_b, fw_out, layer,
                               tm=min(TM_FFN, ms), tn=tn_f, seq_len=n_tok, state=st)
        outs["conv_s"].append(jnp.concatenate([ng, nu], axis=2))

    st = lambda key: jnp.stack(outs[key])
    return (xp.reshape(bsz, seq, d), xs.reshape(nb, n_tok, d), st("k_p"), st("v_p"), st("ik_p"),
            st("k_s"), st("v_s"), st("ik_s"), st("chunk_v"), st("conv_p"), st("conv_s"))
```

```python
import functools
import math

import jax
import jax.numpy as jnp
from jax import lax
from jax.experimental import pallas as pl
from jax.experimental.pallas import tpu as pltpu

EPS = 1e-6
ROPE_THETA = 10000.0
TOPK_MAX = 256
LANES = 128
NEG_BIG = -1e30
INT_MIN = -(2 ** 31)
KEY_NEG_INF = -2139095041
TIE_ALL = 2 ** 30
SAFE_SHIFT_BOUND = 40.0
VMEM_LIMIT = 56 * 1024 * 1024
TM_ROWS = 512
TM_FFN = 1024
TQ_PROMPT = 256
SEQ_PER_STEP = 2

f32 = jnp.float32
bf16 = jnp.bfloat16
i32 = jnp.int32


def _params(*sem):
    return pltpu.CompilerParams(dimension_semantics=sem, vmem_limit_bytes=VMEM_LIMIT)


def _dot(a, b):
    return jnp.dot(a, b, preferred_element_type=f32)


def _dot_nt(a, b):
    return lax.dot_general(a, b, (((1,), (1,)), ((), ())), preferred_element_type=f32)


def _rms(x, g):
    r = lax.rsqrt(jnp.mean(x * x, axis=-1, keepdims=True) + EPS)
    return x * r * g


def _rows(a):
    return a.reshape(a.shape[0], 1, a.shape[1])


def _gelu_tanh(x):
    c = math.sqrt(2.0 / math.pi)
    return 0.5 * x * (1.0 + jnp.tanh(c * (x + 0.044715 * (x * x * x))))


def _nmm_body(x_ref, g_ref, w_ref, o_ref, h_ref, *, act):
    @pl.when(pl.program_id(1) == 0)
    def _():
        h_ref[...] = _rms(x_ref[...], g_ref[...]).astype(bf16)

    z = _dot(h_ref[...], w_ref[...])
    if act == "gelu":
        z = _gelu_tanh(z)
    o_ref[...] = z.astype(o_ref.dtype)


def _norm_matmul(x, gain, w, layer, *, act=None, tm, tn, out_dtype=f32):
    m, k = x.shape
    n = w.shape[2]
    return pl.pallas_call(
        functools.partial(_nmm_body, act=act),
        grid=(m // tm, n // tn),
        in_specs=[
            pl.BlockSpec((tm, k), lambda i, j: (i, 0)),
            pl.BlockSpec((None, 1, k), lambda i, j: (layer, 0, 0)),
            pl.BlockSpec((None, k, tn), lambda i, j: (layer, 0, j)),
        ],
        out_specs=pl.BlockSpec((tm, tn), lambda i, j: (i, j)),
        out_shape=jax.ShapeDtypeStruct((m, n), out_dtype),
        scratch_shapes=[pltpu.VMEM((tm, k), bf16)],
        compiler_params=_params("parallel", "arbitrary"),
        name="norm_matmul",
    )(x, _rows(gain), w)


def _mmres_body(a_ref, w_ref, x_ref, o_ref):
    o_ref[...] = x_ref[...] + _dot(a_ref[...], w_ref[...])


def _matmul_residual(a, w, layer, x, *, tm, tn):
    m, k = a.shape
    n = w.shape[2]
    return pl.pallas_call(
        _mmres_body,
        grid=(m // tm, n // tn),
        in_specs=[
            pl.BlockSpec((tm, k), lambda i, j: (i, 0)),
            pl.BlockSpec((None, k, tn), lambda i, j: (layer, 0, j)),
            pl.BlockSpec((tm, tn), lambda i, j: (i, j)),
        ],
        out_specs=pl.BlockSpec((tm, tn), lambda i, j: (i, j)),
        out_shape=jax.ShapeDtypeStruct((m, n), f32),
        compiler_params=_params("parallel", "arbitrary"),
        name="matmul_residual",
    )(a, w, x)


def _gate_body(u_ref, v_ref, vg_ref, wmix_ref, bias_ref, o_ref, *vn_out, tm, groups):
    vn = _rms(v_ref[...], vg_ref[...])
    if vn_out:
        vn_out[0][...] = vn
    vnb = vn.astype(bf16)
    gd = v_ref.shape[1] // groups
    for c in range(tm // LANES):
        rows = slice(c * LANES, (c + 1) * LANES)
        for g in range(groups):
            cols = slice(g * gd, (g + 1) * gd)
            s = _dot(wmix_ref[g], vnb[rows, cols]) + bias_ref[:, cols]
            o_ref[rows, cols] = (u_ref[rows, cols] * s).astype(bf16)


def _spatial_gate(z, v_gain, layer, wmix, bias, *, tm, want_vn):
    m, w2 = z.shape
    w = w2 // 2
    groups = wmix.shape[0]
    out_shape = [jax.ShapeDtypeStruct((m, w), bf16)]
    out_specs = [pl.BlockSpec((tm, w), lambda i: (i, 0))]
    if want_vn:
        out_shape.append(jax.ShapeDtypeStruct((m, w), f32))
        out_specs.append(pl.BlockSpec((tm, w), lambda i: (i, 0)))
    res = pl.pallas_call(
        functools.partial(_gate_body, tm=tm, groups=groups),
        grid=(m // tm,),
        in_specs=[
            pl.BlockSpec((tm, w), lambda i: (i, 0)),
            pl.BlockSpec((tm, w), lambda i: (i, 1)),
            pl.BlockSpec((None, 1, w), lambda i: (layer, 0, 0)),
            pl.BlockSpec(wmix.shape, lambda i: (0, 0, 0)),
            pl.BlockSpec(bias.shape, lambda i: (0, 0)),
        ],
        out_specs=out_specs,
        out_shape=out_shape,
        compiler_params=_params("parallel"),
        name="spatial_gate",
    )(z, z, _rows(v_gain), wmix, bias)
    return res if want_vn else (res[0], None)


def _ffn_body(x_ref, g_ref, w_ref, cp_ref, wo_ref, *rest, sample, seq_tiles, seq_len, nj):
    if sample:
        st_ref, o_ref, new_ref, h_ref, act0_ref, act1_ref = rest
    else:
        o_ref, tail_ref, h_ref, act0_ref, act1_ref, carry_ref = rest
    acts = (act0_ref, act1_ref)
    i = pl.program_id(0)
    j = pl.program_id(1)

    @pl.when(j == 0)
    def _():
        x = x_ref[...]
        h_ref[...] = _rms(x, g_ref[...]).astype(bf16)
        o_ref[...] = x
        act1_ref[...] = jnp.zeros(act1_ref.shape, bf16)

    def down(src_ref):
        o_ref[...] += _dot(src_ref[...], wo_ref[...])

    def up_conv_gate(dst_ref):
        a = _dot(h_ref[...], w_ref[...])
        tm, tn2 = a.shape
        tn = tn2 // 2
        row = lax.broadcasted_iota(i32, (tm, tn2), 0)
        if sample:
            t = row & (seq_len - 1)
            m1 = t == 0
            m2 = t < 2
            p2 = st_ref[...]
            p1 = pltpu.roll(p2, tm - 1, 0)
            last = pltpu.roll(a, tm - (seq_len - 2), 0).reshape(tm // seq_len, seq_len, tn2)
            new_ref[...] = last[:, :2, :]
        else:
            m1 = row == 0
            m2 = row < 2
            live = (i % seq_tiles) != 0
            c = jnp.where(live, carry_ref[j], 0.0)
            p1 = c[7:8]
            p2 = jnp.where(m1, c[6:7], c[7:8])
            carry_ref[j] = a[tm - 8:]
            tail_ref[0] = a[tm - 8:]
        a1 = jnp.where(m1, p1, pltpu.roll(a, 1, 0))
        a2 = jnp.where(m2, p2, pltpu.roll(a, 2, 0))
        cp = cp_ref[...]
        c = cp[3:4] + cp[0:1] * a2 + cp[1:2] * a1 + cp[2:3] * a
        gate, up = c[:, :tn], c[:, tn:]
        dst_ref[...] = (gate * (1.0 / (1.0 + jnp.exp(-gate))) * up).astype(bf16)

    for parity in (0, 1):
        @pl.when((j < nj) & ((j & 1) == parity))
        def _(parity=parity):
            down(acts[1 - parity])
            up_conv_gate(acts[parity])

    @pl.when(j == nj)
    def _():
        down(acts[(nj - 1) % 2])


def _tile_major(a, tn):
    lead = a.shape[:-1]
    return a.reshape(lead + (2, -1, tn)).swapaxes(-3, -2).reshape(a.shape)


def _half_major(a, tn):
    lead = a.shape[:-1]
    return a.reshape(lead + (-1, 2, tn)).swapaxes(-3, -2).reshape(a.shape)


def _conv_ffn(x, gain, w_in, conv_p, w_out, layer, *, tm, tn, seq_len, state=None):
    m, d = x.shape
    ff = w_out.shape[1]
    nj = ff // tn
    ni = m // tm
    sample = state is not None
    up = lambda j: jnp.minimum(j, nj - 1)
    down = lambda j: jnp.maximum(j - 1, 0)
    in_specs = [
        pl.BlockSpec((tm, d), lambda i, j: (i, 0), pipeline_mode=pl.Buffered(1)),
        pl.BlockSpec((None, 1, d), lambda i, j: (layer, 0, 0)),
        pl.BlockSpec((None, d, 2 * tn), lambda i, j: (layer, 0, up(j))),
        pl.BlockSpec((None, 4, 2 * tn), lambda i, j: (layer, 0, up(j))),
        pl.BlockSpec((None, tn, d), lambda i, j: (layer, down(j), 0)),
    ]
    args = [x, _rows(gain), w_in, conv_p, w_out]
    out_specs = [pl.BlockSpec((tm, d), lambda i, j: (i, 0), pipeline_mode=pl.Buffered(1 if ni == 1 else 2))]
    out_shape = [jax.ShapeDtypeStruct((m, d), f32)]
    scratch = [pltpu.VMEM((tm, d), bf16), pltpu.VMEM((tm, tn), bf16), pltpu.VMEM((tm, tn), bf16)]
    if sample:
        in_specs.append(pl.BlockSpec((tm, 2 * tn), lambda i, j: (i, up(j))))
        args.append(state)
        out_specs.append(pl.BlockSpec((tm // seq_len, 2, 2 * tn), lambda i, j: (i, 0, up(j))))
        out_shape.append(jax.ShapeDtypeStruct((m // seq_len, 2, 2 * ff), f32))
        seq_tiles = 1
    else:
        out_specs.append(pl.BlockSpec((1, 8, 2 * tn), lambda i, j: (i, 0, up(j))))
        out_shape.append(jax.ShapeDtypeStruct((ni, 8, 2 * ff), f32))
        scratch.append(pltpu.VMEM((nj, 8, 2 * tn), f32))
        seq_tiles = seq_len // tm
    return pl.pallas_call(
        functools.partial(_ffn_body, sample=sample, seq_tiles=seq_tiles, seq_len=seq_len, nj=nj),
        grid=(ni, nj + 1),
        in_specs=in_specs,
        out_specs=out_specs,
        out_shape=out_shape,
        scratch_shapes=scratch,
        compiler_params=_params("arbitrary", "arbitrary"),
        name="conv_ffn_sample" if sample else "conv_ffn_prompt",
    )(*args)


def _rope_full(x, c, s):
    return x * c + pltpu.roll(x, LANES // 2, 1) * s


def _rope_half(x, c, s, lo):
    partner = jnp.where(lo, pltpu.roll(x, 96, 1), pltpu.roll(x, 32, 1))
    return x * c + partner * s


def _post_body(p_ref, qg_ref, kg_ref, c1_ref, s1_ref, c2_ref, s2_ref,
               q_ref, k_ref, v_ref, kb_ref, vb_ref, iq_ref, ikw_ref, ik2_ref,
               *, n_heads, n_kv, n_ih, idx_scale, q_scale):
    hd = LANES
    c1, s1, c2, s2 = c1_ref[...], s1_ref[...], c2_ref[...], s2_ref[...]
    tm = p_ref.shape[0]
    lane = lax.broadcasted_iota(i32, (tm, LANES), 1)
    lo = (lane & 63) < 32
    off = 0
    for h in range(n_heads):
        x = p_ref[:, off + h * hd: off + (h + 1) * hd]
        q_ref[:, h * hd:(h + 1) * hd] = (_rope_full(_rms(x, qg_ref[...]), c1, s1) * q_scale).astype(bf16)
    off += n_heads * hd
    for h in range(n_kv):
        x = p_ref[:, off + h * hd: off + (h + 1) * hd]
        kr = _rope_full(_rms(x, kg_ref[...]), c1, s1)
        k_ref[pl.ds(h, tm, stride=n_kv), :] = kr
        kb_ref[:, h * hd:(h + 1) * hd] = kr.astype(bf16)
    off += n_kv * hd
    vv = p_ref[:, off: off + n_kv * hd]
    for h in range(n_kv):
        v_ref[pl.ds(h, tm, stride=n_kv), :] = vv[:, h * hd:(h + 1) * hd]
    vb_ref[...] = vv.astype(bf16)
    off += n_kv * hd
    for h in range(n_ih // 2):
        x = p_ref[:, off + h * LANES: off + (h + 1) * LANES]
        iq_ref[:, h * LANES:(h + 1) * LANES] = _rope_half(x, c2, s2, lo).astype(bf16)
    off += (n_ih // 2) * LANES
    tail = p_ref[:, off: off + LANES]
    tr = _rope_half(tail, c2, s2, lo)
    ik = jnp.where(lane < 64, tr, 0.0)
    ik2_ref[...] = (ik + pltpu.roll(ik, 64, 1)).astype(bf16)
    ikw_ref[...] = jnp.where(lane < 64, tr, tail * idx_scale)


def _attn_post(p, q_gain, k_gain, layer, tabs, *, tm, n_heads, n_kv, n_ih):
    m, npad = p.shape
    c1, s1, c2, s2 = tabs
    nt = c1.shape[0] // tm
    hd = LANES
    tab_spec = pl.BlockSpec((tm, LANES), lambda i: (i % nt, 0))
    row = lambda w: pl.BlockSpec((tm, w), lambda i: (i, 0))
    gain_spec = pl.BlockSpec((None, 1, hd), lambda i: (layer, 0, 0))
    outs = [
        (1, n_heads * hd, bf16), (n_kv, hd, f32), (n_kv, hd, f32), (1, n_kv * hd, bf16), (1, n_kv * hd, bf16),
        (1, n_ih * 64, bf16), (1, LANES, f32), (1, LANES, bf16),
    ]
    return pl.pallas_call(
        functools.partial(_post_body, n_heads=n_heads, n_kv=n_kv, n_ih=n_ih,
                          idx_scale=float((n_ih * 64) ** -0.5), q_scale=float(hd ** -0.5)),
        grid=(m // tm,),
        in_specs=[row(npad), gain_spec, gain_spec, tab_spec, tab_spec, tab_spec, tab_spec],
        out_specs=[pl.BlockSpec((tm * r, w), lambda i: (i, 0)) for r, w, _ in outs],
        out_shape=[jax.ShapeDtypeStruct((m * r, w), dt) for r, w, dt in outs],
        compiler_params=_params("parallel"),
        name="attn_post",
    )(p, _rows(q_gain), _rows(k_gain), c1, s1, c2, s2)


def _key_to_float(key):
    bits = key ^ ((key >> 31) & 0x7FFFFFFF)
    return lax.bitcast_convert_type(bits, f32)


def _select_threshold(count_fn, shape, topk, bits_per_round):
    kk = float(topk)
    n_cand = (1 << bits_per_round) - 1

    def round_body(it, res):
        step = lax.shift_left(jnp.int32(1), 32 - bits_per_round * (it + 1))
        cands = [res + step * (c + 1) for c in range(n_cand)]
        floats = [_key_to_float(c) for c in cands]
        cnts = count_fn(tuple((lambda sc, idx, cf=cf: sc >= cf) for cf in floats))
        for c, n in zip(cands, cnts):
            res = jnp.where(n >= kk, c, res)
        return res

    res = lax.fori_loop(0, 32 // bits_per_round, round_body, jnp.full(shape, INT_MIN, i32))
    return _key_to_float(jnp.maximum(res, KEY_NEG_INF))


def _tie_search(count_fn, thr, need, shape, idx_bits):
    def tie_body(it, end):
        cand = end + lax.shift_left(jnp.int32(1), idx_bits - 1 - it)
        (cnt,) = count_fn((lambda sc, idx: (sc == thr) & (idx < cand),))
        return jnp.where(cnt <= need, cand, end)

    return lax.fori_loop(0, idx_bits, tie_body, jnp.zeros(shape, i32))


def _select_topk(count_fn, tie_ref, shape, topk, idx_bits, bits_per_round):
    kk = float(topk)
    thr = _select_threshold(count_fn, shape, topk, bits_per_round)
    n_gt, n_ge = count_fn((lambda sc, idx: sc > thr, lambda sc, idx: sc >= thr))
    tie_ref[...] = jnp.full(shape, TIE_ALL, i32)
    crowded = jnp.where((n_ge > kk) & (thr > -jnp.inf), 1.0, 0.0)

    @pl.when(jnp.max(crowded) > 0.0)
    def _():
        tie_ref[...] = _tie_search(count_fn, thr, kk - n_gt, shape, idx_bits)

    return thr, tie_ref[...]


def _dsa_prompt_body(q_ref, iq_ref, iwt_ref, ik2_ref, kb_ref, vb_ref, bound_ref, o_ref,
                     sc_ref, iqm_ref, qg_ref, tie_ref, m_ref, acc_ref,
                     *, topk, n_kv, group, n_ih, idx_bits, tq):
    T = LANES
    nch = tq // T
    strips = tq // 8
    qb = pl.program_id(1)
    nkt = qb + 1
    lane = lax.broadcasted_iota(i32, (tq, T), 1)

    for h in range(n_ih):
        pair = iq_ref[0, :, (h // 2) * T:(h // 2 + 1) * T].astype(f32)
        keep = (lane < 64) if h % 2 == 0 else (lane >= 64)
        iqm_ref[h] = jnp.where(keep, pair, 0.0).astype(bf16)
    for h in range(n_kv * group):
        qg_ref[h // group, (h % group) * tq:(h % group + 1) * tq, :] = q_ref[0, :, h * T:(h + 1) * T]

    def key_rows(kt):
        return pl.ds(pl.multiple_of(kt * tq, tq), tq)

    key_in_tile = (lax.broadcasted_iota(i32, (strips, 8, tq), 0) * 8
                   + lax.broadcasted_iota(i32, (strips, 8, tq), 1))
    qidx = qb * tq + lax.broadcasted_iota(i32, (strips, 8, tq), 2)

    def score_tile(kt, carry):
        ik_t = ik2_ref[0, key_rows(kt), :]
        acc = jnp.zeros((tq, tq), f32)
        for h in range(n_ih):
            acc = acc + jnp.maximum(_dot_nt(ik_t, iqm_ref[h]), 0.0) * iwt_ref[0, h:h + 1, :]
        admissible = kt * tq + key_in_tile <= qidx
        sc_ref[kt] = jnp.where(admissible, acc.reshape(strips, 8, tq), -jnp.inf).reshape(tq, tq)
        return carry

    lax.fori_loop(0, nkt, score_tile, 0)

    def count_fn(preds):
        def body(kt, accs):
            sc = sc_ref[kt].reshape(strips, 8, tq)
            kidx = kt * tq + key_in_tile
            return tuple(a + jnp.sum(jnp.where(pred(sc, kidx), 1.0, 0.0), axis=0)
                         for a, pred in zip(accs, preds))

        accs = lax.fori_loop(0, nkt, body, tuple(jnp.zeros((8, tq), f32) for _ in preds))
        return tuple(jnp.broadcast_to(jnp.sum(a, axis=0, keepdims=True), (8, tq)) for a in accs)

    thr, tie_end = _select_topk(count_fn, tie_ref, (8, tq), topk, idx_bits, 1)

    def bias_tile(kt, carry):
        sc = sc_ref[kt].reshape(strips, 8, tq)
        kidx = kt * tq + key_in_tile
        sel = ((sc > thr) | ((sc == thr) & (kidx < tie_end))) & (kidx <= qidx)
        sc_ref[kt] = jnp.where(sel, 0.0, NEG_BIG).reshape(tq, tq).T
        return carry

    lax.fori_loop(0, nkt, bias_tile, 0)

    gq = group * tq

    def scores(kt, kvh):
        k_t = kb_ref[0, key_rows(kt), kvh * T:(kvh + 1) * T]
        s = _dot_nt(qg_ref[kvh], k_t)
        return (s.reshape(group, tq, tq) + sc_ref[kt][None]).reshape(gq, tq)

    bound = bound_ref[...]
    m_ref[...] = jnp.broadcast_to(bound[None], m_ref.shape)

    @pl.when(jnp.max(bound) > SAFE_SHIFT_BOUND)
    def _():
        m_ref[...] = jnp.full(m_ref.shape, NEG_BIG, f32)

        def max_tile(kt, carry):
            for kvh in range(n_kv):
                s = scores(kt, kvh)
                mx = m_ref[kvh]
                for c in range(nch):
                    mx = jnp.maximum(mx, s[:, c * T:(c + 1) * T])
                m_ref[kvh] = mx
            return carry

        lax.fori_loop(0, nkt, max_tile, 0)
        for kvh in range(n_kv):
            m_ref[kvh] = jnp.broadcast_to(jnp.max(m_ref[kvh], axis=1, keepdims=True), (gq, T))

    acc_ref[...] = jnp.zeros(acc_ref.shape, f32)
    ones = jnp.ones((tq, T), bf16)

    def pv_tile(kt, carry):
        for kvh in range(n_kv):
            s = scores(kt, kvh)
            mx = m_ref[kvh]
            p = jnp.concatenate([jnp.exp(s[:, c * T:(c + 1) * T] - mx) for c in range(nch)], axis=1)
            v1 = jnp.concatenate([vb_ref[0, key_rows(kt), kvh * T:(kvh + 1) * T], ones], axis=1)
            acc_ref[kvh] += _dot(p.astype(bf16), v1)
        return carry

    lax.fori_loop(0, nkt, pv_tile, 0)
    for kvh in range(n_kv):
        o = acc_ref[kvh, :, :T] / acc_ref[kvh, :, T:]
        for g in range(group):
            h = kvh * group + g
            o_ref[0, :, h * T:(h + 1) * T] = o[g * tq:(g + 1) * tq].astype(bf16)


def _dsa_prompt(q, iq, iwt, ik2, kb, vb, bound, *, topk, n_kv, n_ih, tq):
    b, s, qd = q.shape
    T = LANES
    n_heads = qd // T
    group = n_heads // n_kv
    nq = s // tq
    blk = lambda w: pl.BlockSpec((1, tq, w), lambda bi, qi: (bi, qi, 0))
    full = lambda w: pl.BlockSpec((1, s, w), lambda bi, qi: (bi, 0, 0))
    att = pltpu.VMEM((n_kv, group * tq, T), f32)
    return pl.pallas_call(
        functools.partial(_dsa_prompt_body, topk=topk, n_kv=n_kv, group=group, n_ih=n_ih,
                          idx_bits=int(s).bit_length(), tq=tq),
        grid=(b, nq),
        in_specs=[blk(qd), blk(iq.shape[2]), pl.BlockSpec((1, n_ih, tq), lambda bi, qi: (bi, 0, qi)),
                  full(T), full(n_kv * T), full(n_kv * T), pl.BlockSpec((1, T), lambda bi, qi: (0, 0))],
        out_specs=blk(qd),
        out_shape=jax.ShapeDtypeStruct((b, s, qd), bf16),
        scratch_shapes=[
            pltpu.VMEM((nq, tq, tq), f32),
            pltpu.VMEM((n_ih, tq, T), bf16),
            pltpu.VMEM((n_kv, group * tq, T), bf16), pltpu.VMEM((8, tq), i32),
            att, pltpu.VMEM((n_kv, group * tq, 2 * T), f32),
        ],
        compiler_params=_params("parallel", "arbitrary"),
        name="dsa_prompt",
    )(q, iq, iwt, ik2, kb, vb, bound)


def _dsa_sample_body(pt_ref, q_ref, iq_ref, iwb_ref, ikn_ref, kn_ref, vn_ref, *rest,
                     n_seq, n_pages, n_kv, topk, n_tok, idx_bits):
    del pt_ref
    T = LANES
    pages = [rest[i * n_pages:(i + 1) * n_pages] for i in range(3 * n_seq)]
    idx_pages, k_pages, v_pages = pages[:n_seq], pages[n_seq:2 * n_seq], pages[2 * n_seq:]
    o_ref, sc_ref, s_ref, tie_ref, thr_ref = rest[3 * n_seq * n_pages:]
    nt = n_pages + 1
    n_rows = q_ref.shape[1]
    sel_rows = n_seq * n_tok
    lane = lax.broadcasted_iota(i32, (sel_rows, T), 1)
    tok = lax.broadcasted_iota(i32, (sel_rows, T), 0) & (n_tok - 1)
    fresh = (lane <= tok) & (lane < n_tok)

    for s in range(n_seq):
        iq = iq_ref[s]
        iwb = iwb_ref[s]
        for p in range(nt):
            ikt = (idx_pages[s][p][...] if p < n_pages else ikn_ref[s]).astype(bf16)
            w = jnp.maximum(_dot(iq, ikt), 0.0) * iwb
            sc_ref[p, s * n_tok:(s + 1) * n_tok, :] = jnp.sum(w.reshape(n_rows // n_tok, n_tok, T), axis=0)
    sc_ref[n_pages] = jnp.where(fresh, sc_ref[n_pages], -jnp.inf)

    def count_fn(preds):
        accs = [jnp.zeros((sel_rows, T), f32) for _ in preds]
        for p in range(nt):
            sc = sc_ref[p]
            for n, pred in enumerate(preds):
                accs[n] = accs[n] + jnp.where(pred(sc, p * T + lane), 1.0, 0.0)
        return tuple(jnp.broadcast_to(jnp.sum(a, axis=1, keepdims=True), (sel_rows, T)) for a in accs)

    thr, _ = _select_topk(count_fn, tie_ref, (sel_rows, T), topk, idx_bits, 4)
    thr_ref[...] = thr

    lane_s = lax.broadcasted_iota(i32, (n_tok, T), 1)
    fresh_s = (lane_s <= lax.broadcasted_iota(i32, (n_tok, T), 0)) & (lane_s < n_tok)
    rows_kv = n_rows // n_kv

    def head_rows(ref, kvh):
        n_keys = ref.shape[0] // n_kv
        rows = ref[pl.ds(kvh, n_keys, stride=n_kv), :]
        if n_keys < T:
            rows = jnp.concatenate([rows, jnp.zeros((T - n_keys, T), f32)], axis=0)
        return rows.astype(bf16)

    for s in range(n_seq):
        q = q_ref[s]
        toks = slice(s * n_tok, (s + 1) * n_tok)
        thr_s, tie_s = thr_ref[toks, :], tie_ref[toks, :]
        mx = jnp.full((n_rows, T), NEG_BIG, f32)
        for p in range(nt):
            sc = sc_ref[p, toks, :]
            sel = (sc > thr_s) | ((sc == thr_s) & (p * T + lane_s < tie_s))
            if p == n_pages:
                sel = sel & fresh_s
            bias = jnp.concatenate([jnp.where(sel, 0.0, NEG_BIG)] * (rows_kv // n_tok), axis=0)
            k_ref = k_pages[s][p] if p < n_pages else kn_ref.at[s]
            sm = jnp.concatenate(
                [_dot_nt(q[kvh * rows_kv:(kvh + 1) * rows_kv], head_rows(k_ref, kvh)) + bias
                 for kvh in range(n_kv)], axis=0)
            s_ref[s, p] = sm
            mx = jnp.maximum(mx, sm)
        m = jnp.max(mx, axis=1, keepdims=True)
        l = jnp.zeros((n_rows, 1), f32)
        accs = [jnp.zeros((rows_kv, T), f32) for _ in range(n_kv)]
        for p in range(nt):
            e = jnp.exp(s_ref[s, p] - m)
            l = l + jnp.sum(e, axis=1, keepdims=True)
            eb = e.astype(bf16)
            v_ref = v_pages[s][p] if p < n_pages else vn_ref.at[s]
            for kvh in range(n_kv):
                accs[kvh] = accs[kvh] + _dot(eb[kvh * rows_kv:(kvh + 1) * rows_kv], head_rows(v_ref, kvh))
        o_ref[s] = (jnp.concatenate(accs, axis=0) / l).astype(bf16)


def _dsa_sample(page_table, layer, q, iq, iwb, ikn, kn, vn, cache_idx_t, cache_k, cache_v,
                *, topk, n_tok, n_kv):
    nb, n_rows, T = q.shape
    n_pages = page_table.shape[1]
    kv_rows = cache_k.shape[2]
    idw = cache_idx_t.shape[2]
    ns = SEQ_PER_STEP if nb % SEQ_PER_STEP == 0 else 1
    seq = lambda shape: pl.BlockSpec((ns,) + shape, lambda b, pt: (b, 0, 0))

    def pages(shape):
        return [pl.BlockSpec((None, None) + shape, lambda b, pt, s=s, p=p: (layer, pt[ns * b + s, p], 0, 0))
                for s in range(ns) for p in range(n_pages)]

    in_specs = [seq((n_rows, T)), seq((n_rows, idw)), seq((n_rows, T)), seq((idw, T)),
                seq(kn.shape[1:]), seq(vn.shape[1:])]
    in_specs += pages((idw, T)) + pages((kv_rows, T)) + pages((kv_rows, T))
    nt = n_pages + 1
    grid_spec = pltpu.PrefetchScalarGridSpec(
        num_scalar_prefetch=1,
        grid=(nb // ns,),
        in_specs=in_specs,
        out_specs=seq((n_rows, T)),
        scratch_shapes=[pltpu.VMEM((nt, ns * n_tok, T), f32), pltpu.VMEM((ns, nt, n_rows, T), f32),
                        pltpu.VMEM((ns * n_tok, T), i32), pltpu.VMEM((ns * n_tok, T), f32)],
    )
    return pl.pallas_call(
        functools.partial(_dsa_sample_body, n_seq=ns, n_pages=n_pages, n_kv=n_kv, topk=topk, n_tok=n_tok,
                          idx_bits=int(nt * T).bit_length()),
        grid_spec=grid_spec,
        out_shape=jax.ShapeDtypeStruct((nb, n_rows, T), bf16),
        compiler_params=_params("parallel"),
        name="dsa_sample",
    )(page_table, q, iq, iwb, ikn, kn, vn,
      *([cache_idx_t] * (ns * n_pages)), *([cache_k] * (ns * n_pages)), *([cache_v] * (ns * n_pages)))


def _rope_tables(pos, dim):
    half = dim // 2
    inv = ROPE_THETA ** (-jnp.arange(half, dtype=f32) * (2.0 / dim))
    ang = pos.astype(f32)[:, None] * inv[None, :]
    cos = jnp.concatenate([jnp.cos(ang), jnp.cos(ang)], axis=1)
    sin = jnp.concatenate([-jnp.sin(ang), jnp.sin(ang)], axis=1)
    reps = LANES // dim
    return jnp.tile(cos, (1, reps)), jnp.tile(sin, (1, reps))


def _mix_tables(w_s, b_s, c, group_dim):
    causal = jnp.tril(jnp.ones((c, c), dtype=bool))
    ws = jnp.where(causal[None], w_s[:, :c, :c], 0)
    reps = LANES // c
    ws = jnp.tile(ws, (1, reps, reps))
    blk = jnp.arange(LANES) // c
    ws = jnp.where((blk[:, None] == blk[None, :])[None], ws, 0).astype(bf16)
    bias = jnp.tile(jnp.repeat(b_s[:, :c].T, group_dim, axis=1), (reps, 1)).astype(f32)
    return ws, bias


def kernel(x_prompt, x_sample, cache_k, cache_v, cache_idx_k, state_ffn_conv, page_table, a_norm, a_w_in, a_v_norm, a_w_s, a_b_s, a_w_out, b_norm, b_w_in, b_q_norm, b_k_norm, b_w_o, f_norm, f_w_in, f_conv_w, f_conv_b, f_w_out):
    bsz, seq, d = x_prompt.shape
    nb, n_tok, _ = x_sample.shape
    depth = f_norm.shape[0]
    n_layers_b, n_pool, page, n_kv, hd = cache_k.shape
    idw = cache_idx_k.shape[3]
    past = page_table.shape[1] * page
    n_heads = b_w_o.shape[1] // hd
    group = n_heads // n_kv
    n_ih = (b_w_in.shape[2] - (n_heads + 2 * n_kv) * hd - idw) // (idw + 1)
    ff = f_w_out.shape[1]
    a_groups, chunk = a_w_s.shape[1], a_w_s.shape[2]
    a_width = a_w_in.shape[2] // 2
    assert hd == LANES and idw == 64 and page == LANES and chunk == LANES and n_ih % 2 == 0
    assert f_conv_w.shape[1] == 3 and n_tok >= 2 and LANES % n_tok == 0 and n_tok & (n_tok - 1) == 0

    mp, ms = bsz * seq, nb * n_tok
    xp = x_prompt.reshape(mp, d)
    xs = x_sample.reshape(ms, d)
    tm_p = min(TM_ROWS, seq)
    tm_s = min(TM_ROWS, ms)
    tq = min(TQ_PROMPT, seq)

    aw_in, aw_out = a_w_in.astype(bf16), a_w_out.astype(bf16)
    n_real = b_w_in.shape[2]
    n_pad = -(-n_real // LANES) * LANES
    bw_in = jnp.pad(b_w_in, ((0, 0), (0, 0), (0, n_pad - n_real))).astype(bf16)
    bw_o = b_w_o.astype(bf16)
    tn_b = LANES * 11 if n_pad % (LANES * 11) == 0 else LANES
    tn_f = 512 if ff % 512 == 0 else LANES
    fw_in, fw_out = _tile_major(f_w_in, tn_f).astype(bf16), f_w_out.astype(bf16)
    f_conv_p = _tile_major(jnp.concatenate([f_conv_w, f_conv_b[:, None, :]], axis=1), tn_f)

    ck = cache_k.reshape(n_layers_b, n_pool, page * n_kv, hd)
    cv = cache_v.reshape(n_layers_b, n_pool, page * n_kv, hd)
    cidx_t = jnp.swapaxes(cache_idx_k, 2, 3)
    outs = dict(k_p=[], v_p=[], ik_p=[], k_s=[], v_s=[], ik_s=[], chunk_v=[], conv_p=[], conv_s=[])
    n_mixers = 2
    for layer in range(depth):
        j = layer // n_mixers
        if layer % n_mixers == 0:
            new = []
            for x, tm, c, want in ((xp, tm_p, min(seq, chunk), False), (xs, tm_s, min(n_tok, chunk), True)):
                wmix, bias = _mix_tables(a_w_s[j], a_b_s[j], c, a_width // a_groups)
                z = _norm_matmul(x, a_norm, aw_in, j, act="gelu", tm=min(TM_FFN, x.shape[0]),
                                 tn=1024 if a_width % 1024 == 0 else 512)
                gated, vn = _spatial_gate(z, a_v_norm, j, wmix, bias, tm=min(tm, 512), want_vn=want)
                new.append(_matmul_residual(gated, aw_out, j, x, tm=tm, tn=d))
                if want:
                    outs["chunk_v"].append(vn.reshape(nb, n_tok, a_width))
            xp, xs = new
        else:
            pos_p = jnp.arange(seq, dtype=i32)
            pos_s = jnp.tile(past + jnp.arange(n_tok, dtype=i32), nb)
            res = []
            for x, tm, pos in ((xp, tm_p, pos_p), (xs, tm_s, pos_s)):
                p = _norm_matmul(x, b_norm, bw_in, j, tm=min(TM_FFN, x.shape[0]), tn=tn_b)
                tabs = _rope_tables(pos, hd) + _rope_tables(pos, idw)
                res.append(_attn_post(p, b_q_norm, b_k_norm, j, tabs, tm=tm,
                                      n_heads=n_heads, n_kv=n_kv, n_ih=n_ih))
            q, k, v, kb, vb, iq, ikw, ik2 = res[0]
            r3 = lambda a: a.reshape(bsz, seq, a.shape[1])
            iwt = jnp.swapaxes(ikw[:, idw:idw + n_ih].reshape(bsz, seq, n_ih), 1, 2)
            bound = 1.01 * math.sqrt(hd) * jnp.max(jnp.abs(b_q_norm[j])) * jnp.max(jnp.abs(b_k_norm[j]))
            bound = jnp.full((1, LANES), bound, f32)
            o = _dsa_prompt(r3(q), r3(iq), iwt, r3(ik2), r3(kb), r3(vb), bound,
                            topk=min(TOPK_MAX, seq // 4), n_kv=n_kv, n_ih=n_ih, tq=tq)
            xp = _matmul_residual(o.reshape(mp, n_heads * hd), bw_o, j, xp, tm=tm_p, tn=d)
            n_seq_pages = seq // page
            outs["k_p"].append(k.reshape(bsz, n_seq_pages, page, n_kv, hd))
            outs["v_p"].append(v.reshape(bsz, n_seq_pages, page, n_kv, hd))
            outs["ik_p"].append(ikw[:, :idw].reshape(bsz, n_seq_pages, page, idw))

            q, k, v, kb, vb, iq, ikw, ik2 = res[1]
            hm = lambda a, w: a.reshape(nb, n_tok, a.shape[1] // w, w).transpose(0, 2, 1, 3).reshape(nb, -1, w)
            qs = hm(q, hd)
            iqs = hm(iq, idw)
            iw = ikw[:, idw:idw + n_ih].reshape(nb, n_tok, n_ih).transpose(0, 2, 1).reshape(nb, n_ih * n_tok, 1)
            iwb = jnp.broadcast_to(iw, (nb, n_ih * n_tok, LANES))
            pad_tok = lambda a: jnp.pad(a.reshape(nb, n_tok, -1), ((0, 0), (0, page - n_tok), (0, 0)))
            ikn = jnp.swapaxes(pad_tok(ikw[:, :idw]), 1, 2)
            kn = k.reshape(nb, n_tok * n_kv, hd)
            vn_ = v.reshape(nb, n_tok * n_kv, hd)
            os_ = _dsa_sample(page_table, j, qs, iqs, iwb, ikn, kn, vn_, cidx_t, ck, cv,
                              topk=min(TOPK_MAX, (past + n_tok) // 4), n_tok=n_tok, n_kv=n_kv)
            os_ = os_.reshape(nb, n_heads, n_tok, hd).transpose(0, 2, 1, 3).reshape(ms, n_heads * hd)
            xs = _matmul_residual(os_, bw_o, j, xs, tm=tm_s, tn=d)
            outs["k_s"].append(k.reshape(nb, n_tok, n_kv, hd))
            outs["v_s"].append(v.reshape(nb, n_tok, n_kv, hd))
            outs["ik_s"].append(ikw[:, :idw].reshape(nb, n_tok, idw))

        tm_f = min(TM_FFN, seq)
        xp, tails = _conv_ffn(xp, f_norm, fw_in, f_conv_p, fw_out, layer, tm=tm_f, tn=tn_f, seq_len=seq)
        tails = tails.reshape(bsz, seq // tm_f, 8, 2 * ff)
        outs["conv_p"].append(_half_major(tails[:, -1, 6:8, :], tn_f))
        st = _tile_major(state_ffn_conv[layer], tn_f)
        st = jnp.pad(st, ((0, 0), (0, n_tok - 2), (0, 0))).reshape(ms, 2 * ff)
        xs, new = _conv_ffn(xs, f_norm, fw_in, f_conv_p, fw_out, layer,
                            tm=min(TM_FFN, ms), tn=tn_f, seq_len=n_tok, state=st)
        outs["conv_s"].append(_half_major(new, tn_f))

    st = lambda key: jnp.stack(outs[key])
    return (xp.reshape(bsz, seq, d), xs.reshape(nb, n_tok, d), st("k_p"), st("v_p"), st("ik_p"),
            st("k_s"), st("v_s"), st("ik_s"), st("chunk_v"), st("conv_p"), st("conv_s"))
```

```python
import functools
import math

import jax
import jax.numpy as jnp
from jax import lax
from jax.experimental import pallas as pl
from jax.experimental.pallas import tpu as pltpu

EPS = 1e-6
ROPE_THETA = 10000.0
TOPK_MAX = 256
LANES = 128
NEG_BIG = -1e30
INT_MIN = -(2 ** 31)
KEY_NEG_INF = -2139095041
TIE_ALL = 2 ** 30
SAFE_SHIFT_BOUND = 40.0
VMEM_LIMIT = 56 * 1024 * 1024
TM_ROWS = 512
TM_FFN = 1024
TQ_PROMPT = 256
SEQ_PER_STEP = 2
COUNT_CHAINS = 4
EARLY_EXIT_BITS = 24

f32 = jnp.float32
bf16 = jnp.bfloat16
i32 = jnp.int32


def _params(*sem):
    return pltpu.CompilerParams(dimension_semantics=sem, vmem_limit_bytes=VMEM_LIMIT)


def _dot(a, b):
    return jnp.dot(a, b, preferred_element_type=f32)


def _dot_nt(a, b):
    return lax.dot_general(a, b, (((1,), (1,)), ((), ())), preferred_element_type=f32)


def _rms(x, g):
    r = lax.rsqrt(jnp.mean(x * x, axis=-1, keepdims=True) + EPS)
    return x * r * g


def _rows(a):
    return a.reshape(a.shape[0], 1, a.shape[1])


def _gelu_tanh(x):
    c = math.sqrt(2.0 / math.pi)
    return 0.5 * x * (1.0 + jnp.tanh(c * (x + 0.044715 * (x * x * x))))


def _nmm_body(x_ref, g_ref, w_ref, o_ref, h_ref, *, act):
    @pl.when(pl.program_id(1) == 0)
    def _():
        h_ref[...] = _rms(x_ref[...], g_ref[...]).astype(bf16)

    z = _dot(h_ref[...], w_ref[...])
    if act == "gelu":
        z = _gelu_tanh(z)
    o_ref[...] = z.astype(o_ref.dtype)


def _norm_matmul(x, gain, w, layer, *, act=None, tm, tn, out_dtype=f32):
    m, k = x.shape
    n = w.shape[2]
    return pl.pallas_call(
        functools.partial(_nmm_body, act=act),
        grid=(m // tm, n // tn),
        in_specs=[
            pl.BlockSpec((tm, k), lambda i, j: (i, 0)),
            pl.BlockSpec((None, 1, k), lambda i, j: (layer, 0, 0)),
            pl.BlockSpec((None, k, tn), lambda i, j: (layer, 0, j)),
        ],
        out_specs=pl.BlockSpec((tm, tn), lambda i, j: (i, j)),
        out_shape=jax.ShapeDtypeStruct((m, n), out_dtype),
        scratch_shapes=[pltpu.VMEM((tm, k), bf16)],
        compiler_params=_params("parallel", "arbitrary"),
        name="norm_matmul",
    )(x, _rows(gain), w)


def _mmres_body(a_ref, w_ref, x_ref, o_ref):
    o_ref[...] = x_ref[...] + _dot(a_ref[...], w_ref[...])


def _matmul_residual(a, w, layer, x, *, tm, tn):
    m, k = a.shape
    n = w.shape[2]
    return pl.pallas_call(
        _mmres_body,
        grid=(m // tm, n // tn),
        in_specs=[
            pl.BlockSpec((tm, k), lambda i, j: (i, 0)),
            pl.BlockSpec((None, k, tn), lambda i, j: (layer, 0, j)),
            pl.BlockSpec((tm, tn), lambda i, j: (i, j)),
        ],
        out_specs=pl.BlockSpec((tm, tn), lambda i, j: (i, j)),
        out_shape=jax.ShapeDtypeStruct((m, n), f32),
        compiler_params=_params("parallel", "arbitrary"),
        name="matmul_residual",
    )(a, w, x)


def _gate_body(u_ref, v_ref, vg_ref, wmix_ref, bias_ref, o_ref, *vn_out, tm, groups):
    vn = _rms(v_ref[...], vg_ref[...])
    if vn_out:
        vn_out[0][...] = vn
    vnb = vn.astype(bf16)
    gd = v_ref.shape[1] // groups
    for c in range(tm // LANES):
        rows = slice(c * LANES, (c + 1) * LANES)
        for g in range(groups):
            cols = slice(g * gd, (g + 1) * gd)
            s = _dot(wmix_ref[g], vnb[rows, cols]) + bias_ref[:, cols]
            o_ref[rows, cols] = (u_ref[rows, cols] * s).astype(bf16)


def _spatial_gate(z, v_gain, layer, wmix, bias, *, tm, want_vn):
    m, w2 = z.shape
    w = w2 // 2
    groups = wmix.shape[0]
    out_shape = [jax.ShapeDtypeStruct((m, w), bf16)]
    out_specs = [pl.BlockSpec((tm, w), lambda i: (i, 0))]
    if want_vn:
        out_shape.append(jax.ShapeDtypeStruct((m, w), f32))
        out_specs.append(pl.BlockSpec((tm, w), lambda i: (i, 0)))
    res = pl.pallas_call(
        functools.partial(_gate_body, tm=tm, groups=groups),
        grid=(m // tm,),
        in_specs=[
            pl.BlockSpec((tm, w), lambda i: (i, 0)),
            pl.BlockSpec((tm, w), lambda i: (i, 1)),
            pl.BlockSpec((None, 1, w), lambda i: (layer, 0, 0)),
            pl.BlockSpec(wmix.shape, lambda i: (0, 0, 0)),
            pl.BlockSpec(bias.shape, lambda i: (0, 0)),
        ],
        out_specs=out_specs,
        out_shape=out_shape,
        compiler_params=_params("parallel"),
        name="spatial_gate",
    )(z, z, _rows(v_gain), wmix, bias)
    return res if want_vn else (res[0], None)


def _conv3(a, cw, cb, prev1, prev2):
    return cb + cw[0:1] * prev2 + cw[1:2] * prev1 + cw[2:3] * a


def _ffn_body(x_ref, g_ref, wg_ref, wu_ref, cwg_ref, cwu_ref, cbg_ref, cbu_ref, wo_ref, *rest,
              sample, seq_tiles, seq_len, nj):
    if sample:
        sg_ref, su_ref, o_ref, ag_ref, au_ref, h_ref, act0_ref, act1_ref = rest
    else:
        o_ref, tg_ref, tu_ref, h_ref, act0_ref, act1_ref, carry_ref = rest
    acts = (act0_ref, act1_ref)
    i = pl.program_id(0)
    j = pl.program_id(1)

    @pl.when(j == 0)
    def _():
        x = x_ref[...]
        h_ref[...] = _rms(x, g_ref[...]).astype(bf16)
        o_ref[...] = x
        act1_ref[...] = jnp.zeros(act1_ref.shape, bf16)

    def down(src_ref):
        o_ref[...] += _dot(src_ref[...], wo_ref[...])

    def up_conv_gate(dst_ref):
        h = h_ref[...]
        ag = _dot(h, wg_ref[...])
        au = _dot(h, wu_ref[...])
        tm, tn = ag.shape
        row = lax.broadcasted_iota(i32, (tm, tn), 0)
        if sample:
            t = row & (seq_len - 1)
            m1 = t == 0
            m2 = t < 2
            p2g, p2u = sg_ref[...], su_ref[...]
            p1g, p1u = pltpu.roll(p2g, tm - 1, 0), pltpu.roll(p2u, tm - 1, 0)
            for a, a_ref in ((ag, ag_ref), (au, au_ref)):
                last = pltpu.roll(a, tm - (seq_len - 2), 0).reshape(tm // seq_len, seq_len, tn)
                a_ref[...] = last[:, :2, :]
        else:
            m1 = row == 0
            m2 = row < 2
            live = (i % seq_tiles) != 0
            cg = jnp.where(live, carry_ref[j, 0], 0.0)
            cu = jnp.where(live, carry_ref[j, 1], 0.0)
            p1g, p1u = cg[7:8], cu[7:8]
            p2g = jnp.where(m1, cg[6:7], cg[7:8])
            p2u = jnp.where(m1, cu[6:7], cu[7:8])
            carry_ref[j, 0] = ag[tm - 8:]
            carry_ref[j, 1] = au[tm - 8:]
            tg_ref[0] = ag[tm - 8:]
            tu_ref[0] = au[tm - 8:]
        a1g = jnp.where(m1, p1g, pltpu.roll(ag, 1, 0))
        a2g = jnp.where(m2, p2g, pltpu.roll(ag, 2, 0))
        a1u = jnp.where(m1, p1u, pltpu.roll(au, 1, 0))
        a2u = jnp.where(m2, p2u, pltpu.roll(au, 2, 0))
        cg_ = _conv3(ag, cwg_ref[...], cbg_ref[...], a1g, a2g)
        cu_ = _conv3(au, cwu_ref[...], cbu_ref[...], a1u, a2u)
        dst_ref[...] = (cg_ * (1.0 / (1.0 + jnp.exp(-cg_))) * cu_).astype(bf16)

    for parity in (0, 1):
        @pl.when((j < nj) & ((j & 1) == parity))
        def _(parity=parity):
            down(acts[1 - parity])
            up_conv_gate(acts[parity])

    @pl.when(j == nj)
    def _():
        down(acts[(nj - 1) % 2])


def _conv_ffn(x, gain, w_in, conv_w, conv_b, w_out, layer, *, tm, tn, seq_len, state=None):
    m, d = x.shape
    ff = w_out.shape[1]
    nj = ff // tn
    ni = m // tm
    sample = state is not None
    up = lambda j: jnp.minimum(j, nj - 1)
    down = lambda j: jnp.maximum(j - 1, 0)
    in_specs = [
        pl.BlockSpec((tm, d), lambda i, j: (i, 0), pipeline_mode=pl.Buffered(1)),
        pl.BlockSpec((None, 1, d), lambda i, j: (layer, 0, 0)),
        pl.BlockSpec((None, d, tn), lambda i, j: (layer, 0, up(j))),
        pl.BlockSpec((None, d, tn), lambda i, j: (layer, 0, nj + up(j))),
        pl.BlockSpec((None, 3, tn), lambda i, j: (layer, 0, up(j))),
        pl.BlockSpec((None, 3, tn), lambda i, j: (layer, 0, nj + up(j))),
        pl.BlockSpec((None, 1, tn), lambda i, j: (layer, 0, up(j))),
        pl.BlockSpec((None, 1, tn), lambda i, j: (layer, 0, nj + up(j))),
        pl.BlockSpec((None, tn, d), lambda i, j: (layer, down(j), 0)),
    ]
    conv_b = _rows(conv_b)
    args = [x, _rows(gain), w_in, w_in, conv_w, conv_w, conv_b, conv_b, w_out]
    out_specs = [pl.BlockSpec((tm, d), lambda i, j: (i, 0), pipeline_mode=pl.Buffered(1 if ni == 1 else 2))]
    out_shape = [jax.ShapeDtypeStruct((m, d), f32)]
    scratch = [pltpu.VMEM((tm, d), bf16), pltpu.VMEM((tm, tn), bf16), pltpu.VMEM((tm, tn), bf16)]
    if sample:
        in_specs += [
            pl.BlockSpec((tm, tn), lambda i, j: (i, up(j))),
            pl.BlockSpec((tm, tn), lambda i, j: (i, nj + up(j))),
        ]
        args += [state, state]
        out_specs += [pl.BlockSpec((tm // seq_len, 2, tn), lambda i, j: (i, 0, up(j)))] * 2
        out_shape += [jax.ShapeDtypeStruct((m // seq_len, 2, ff), f32)] * 2
        seq_tiles = 1
    else:
        out_specs += [pl.BlockSpec((1, 8, tn), lambda i, j: (i, 0, up(j)))] * 2
        out_shape += [jax.ShapeDtypeStruct((ni, 8, ff), f32)] * 2
        scratch.append(pltpu.VMEM((nj, 2, 8, tn), f32))
        seq_tiles = seq_len // tm
    return pl.pallas_call(
        functools.partial(_ffn_body, sample=sample, seq_tiles=seq_tiles, seq_len=seq_len, nj=nj),
        grid=(ni, nj + 1),
        in_specs=in_specs,
        out_specs=out_specs,
        out_shape=out_shape,
        scratch_shapes=scratch,
        compiler_params=_params("arbitrary", "arbitrary"),
        name="conv_ffn_sample" if sample else "conv_ffn_prompt",
    )(*args)


def _rope_full(x, c, s):
    return x * c + pltpu.roll(x, LANES // 2, 1) * s


def _rope_half(x, c, s, lo):
    partner = jnp.where(lo, pltpu.roll(x, 96, 1), pltpu.roll(x, 32, 1))
    return x * c + partner * s


def _post_body(p_ref, qg_ref, kg_ref, c1_ref, s1_ref, c2_ref, s2_ref,
               q_ref, k_ref, v_ref, kb_ref, vb_ref, iq_ref, ikw_ref, ik2_ref,
               *, n_heads, n_kv, n_ih, idx_scale, q_scale):
    hd = LANES
    c1, s1, c2, s2 = c1_ref[...], s1_ref[...], c2_ref[...], s2_ref[...]
    tm = p_ref.shape[0]
    lane = lax.broadcasted_iota(i32, (tm, LANES), 1)
    lo = (lane & 63) < 32
    off = 0
    for h in range(n_heads):
        x = p_ref[:, off + h * hd: off + (h + 1) * hd]
        q_ref[:, h * hd:(h + 1) * hd] = (_rope_full(_rms(x, qg_ref[...]), c1, s1) * q_scale).astype(bf16)
    off += n_heads * hd
    for h in range(n_kv):
        x = p_ref[:, off + h * hd: off + (h + 1) * hd]
        kr = _rope_full(_rms(x, kg_ref[...]), c1, s1)
        k_ref[pl.ds(h, tm, stride=n_kv), :] = kr
        kb_ref[:, h * hd:(h + 1) * hd] = kr.astype(bf16)
    off += n_kv * hd
    vv = p_ref[:, off: off + n_kv * hd]
    for h in range(n_kv):
        v_ref[pl.ds(h, tm, stride=n_kv), :] = vv[:, h * hd:(h + 1) * hd]
    vb_ref[...] = vv.astype(bf16)
    off += n_kv * hd
    for h in range(n_ih // 2):
        x = p_ref[:, off + h * LANES: off + (h + 1) * LANES]
        iq_ref[:, h * LANES:(h + 1) * LANES] = _rope_half(x, c2, s2, lo).astype(bf16)
    off += (n_ih // 2) * LANES
    tail = p_ref[:, off: off + LANES]
    tr = _rope_half(tail, c2, s2, lo)
    ik = jnp.where(lane < 64, tr, 0.0)
    ik2_ref[...] = (ik + pltpu.roll(ik, 64, 1)).astype(bf16)
    ikw_ref[...] = jnp.where(lane < 64, tr, tail * idx_scale)


def _attn_post(p, q_gain, k_gain, layer, tabs, *, tm, n_heads, n_kv, n_ih):
    m, npad = p.shape
    c1, s1, c2, s2 = tabs
    nt = c1.shape[0] // tm
    hd = LANES
    tab_spec = pl.BlockSpec((tm, LANES), lambda i: (i % nt, 0))
    row = lambda w: pl.BlockSpec((tm, w), lambda i: (i, 0))
    gain_spec = pl.BlockSpec((None, 1, hd), lambda i: (layer, 0, 0))
    outs = [
        (1, n_heads * hd, bf16), (n_kv, hd, f32), (n_kv, hd, f32), (1, n_kv * hd, bf16), (1, n_kv * hd, bf16),
        (1, n_ih * 64, bf16), (1, LANES, f32), (1, LANES, bf16),
    ]
    return pl.pallas_call(
        functools.partial(_post_body, n_heads=n_heads, n_kv=n_kv, n_ih=n_ih,
                          idx_scale=float((n_ih * 64) ** -0.5), q_scale=float(hd ** -0.5)),
        grid=(m // tm,),
        in_specs=[row(npad), gain_spec, gain_spec, tab_spec, tab_spec, tab_spec, tab_spec],
        out_specs=[pl.BlockSpec((tm * r, w), lambda i: (i, 0)) for r, w, _ in outs],
        out_shape=[jax.ShapeDtypeStruct((m * r, w), dt) for r, w, dt in outs],
        compiler_params=_params("parallel"),
        name="attn_post",
    )(p, _rows(q_gain), _rows(k_gain), c1, s1, c2, s2)


def _key_to_float(key):
    bits = key ^ ((key >> 31) & 0x7FFFFFFF)
    return lax.bitcast_convert_type(bits, f32)


def _select_threshold(count_fn, res_ref, shape, topk, bits_per_round):
    kk = float(topk)
    n_cand = (1 << bits_per_round) - 1

    def round_body(it, carry):
        res, cnt_res = carry
        step = lax.shift_left(jnp.int32(1), 32 - bits_per_round * (it + 1))
        cands = [res + step * (c + 1) for c in range(n_cand)]
        floats = [_key_to_float(c) for c in cands]
        cnts = count_fn(tuple((lambda sc, idx, cf=cf: sc >= cf) for cf in floats))
        for c, n in zip(cands, cnts):
            res = jnp.where(n >= kk, c, res)
            cnt_res = jnp.where(n >= kk, n, cnt_res)
        return res, cnt_res

    n_rounds = 32 // bits_per_round
    early = -(-EARLY_EXIT_BITS // bits_per_round)
    res, cnt = lax.fori_loop(0, early, round_body, (jnp.full(shape, INT_MIN, i32), jnp.zeros(shape, f32)))
    res_ref[...] = res
    unsettled = jnp.where((cnt != kk) & (res != INT_MIN), 1.0, 0.0)

    @pl.when(jnp.max(unsettled) > 0.0)
    def _():
        res_ref[...] = lax.fori_loop(early, n_rounds, round_body, (res, cnt))[0]

    return _key_to_float(jnp.maximum(res_ref[...], KEY_NEG_INF))


def _tie_search(count_fn, thr, need, shape, idx_bits):
    def tie_body(it, end):
        cand = end + lax.shift_left(jnp.int32(1), idx_bits - 1 - it)
        (cnt,) = count_fn((lambda sc, idx: (sc == thr) & (idx < cand),))
        return jnp.where(cnt <= need, cand, end)

    return lax.fori_loop(0, idx_bits, tie_body, jnp.zeros(shape, i32))


def _select_topk(count_fn, tie_ref, shape, topk, idx_bits, bits_per_round):
    kk = float(topk)
    thr = _select_threshold(count_fn, tie_ref, shape, topk, bits_per_round)
    n_gt, n_ge = count_fn((lambda sc, idx: sc > thr, lambda sc, idx: sc >= thr))
    tie_ref[...] = jnp.full(shape, TIE_ALL, i32)
    crowded = jnp.where((n_ge > kk) & (thr > -jnp.inf), 1.0, 0.0)

    @pl.when(jnp.max(crowded) > 0.0)
    def _():
        tie_ref[...] = _tie_search(count_fn, thr, kk - n_gt, shape, idx_bits)

    return thr, tie_ref[...]


def _dsa_prompt_body(q_ref, iq_ref, iwt_ref, ik2_ref, kb_ref, vb_ref, bound_ref, o_ref,
                     sc_ref, iqm_ref, qg_ref, tie_ref, m_ref, acc_ref,
                     *, topk, n_kv, group, n_ih, idx_bits, tq):
    T = LANES
    nch = tq // T
    strips = tq // 8
    qb = pl.program_id(1)
    nkt = qb + 1
    lane = lax.broadcasted_iota(i32, (tq, T), 1)

    for h in range(n_ih):
        pair = iq_ref[0, :, (h // 2) * T:(h // 2 + 1) * T].astype(f32)
        keep = (lane < 64) if h % 2 == 0 else (lane >= 64)
        iqm_ref[h] = jnp.where(keep, pair, 0.0).astype(bf16)
    for h in range(n_kv * group):
        qg_ref[h // group, (h % group) * tq:(h % group + 1) * tq, :] = q_ref[0, :, h * T:(h + 1) * T]

    def key_rows(kt):
        return pl.ds(pl.multiple_of(kt * tq, tq), tq)

    key_in_tile = (lax.broadcasted_iota(i32, (strips, 8, tq), 0) * 8
                   + lax.broadcasted_iota(i32, (strips, 8, tq), 1))
    qidx = qb * tq + lax.broadcasted_iota(i32, (strips, 8, tq), 2)

    def score_tile(kt, carry):
        ik_t = ik2_ref[0, key_rows(kt), :]
        acc = jnp.zeros((tq, tq), f32)
        for h in range(n_ih):
            acc = acc + jnp.maximum(_dot_nt(ik_t, iqm_ref[h]), 0.0) * iwt_ref[0, h:h + 1, :]
        admissible = kt * tq + key_in_tile <= qidx
        sc_ref[kt] = jnp.where(admissible, acc.reshape(strips, 8, tq), -jnp.inf).reshape(tq, tq)
        return carry

    lax.fori_loop(0, nkt, score_tile, 0)

    def count_fn(preds):
        def body(kt, accs):
            sc = sc_ref[kt].reshape(strips, 8, tq)
            kidx = kt * tq + key_in_tile
            return tuple(a + jnp.sum(jnp.where(pred(sc, kidx), 1.0, 0.0).reshape(-1, 8 * COUNT_CHAINS, tq), axis=0)
                         for a, pred in zip(accs, preds))

        accs = lax.fori_loop(0, nkt, body, tuple(jnp.zeros((8 * COUNT_CHAINS, tq), f32) for _ in preds))
        return tuple(jnp.broadcast_to(jnp.sum(a, axis=0, keepdims=True), (8, tq)) for a in accs)

    thr, tie_end = _select_topk(count_fn, tie_ref, (8, tq), topk, idx_bits, 1)

    def bias_tile(kt, carry):
        sc = sc_ref[kt].reshape(strips, 8, tq)
        kidx = kt * tq + key_in_tile
        sel = ((sc > thr) | ((sc == thr) & (kidx < tie_end))) & (kidx <= qidx)
        sc_ref[kt] = jnp.where(sel, 0.0, NEG_BIG).reshape(tq, tq).T
        return carry

    lax.fori_loop(0, nkt, bias_tile, 0)

    gq = group * tq

    def scores(kt, kvh):
        k_t = kb_ref[0, key_rows(kt), kvh * T:(kvh + 1) * T]
        s = _dot_nt(qg_ref[kvh], k_t)
        return (s.reshape(group, tq, tq) + sc_ref[kt][None]).reshape(gq, tq)

    bound = bound_ref[...]
    m_ref[...] = jnp.broadcast_to(bound[None], m_ref.shape)

    @pl.when(jnp.max(bound) > SAFE_SHIFT_BOUND)
    def _():
        m_ref[...] = jnp.full(m_ref.shape, NEG_BIG, f32)

        def max_tile(kt, carry):
            for kvh in range(n_kv):
                s = scores(kt, kvh)
                mx = m_ref[kvh]
                for c in range(nch):
                    mx = jnp.maximum(mx, s[:, c * T:(c + 1) * T])
                m_ref[kvh] = mx
            return carry

        lax.fori_loop(0, nkt, max_tile, 0)
        for kvh in range(n_kv):
            m_ref[kvh] = jnp.broadcast_to(jnp.max(m_ref[kvh], axis=1, keepdims=True), (gq, T))

    acc_ref[...] = jnp.zeros(acc_ref.shape, f32)
    ones = jnp.ones((tq, T), bf16)

    def pv_tile(kt, carry):
        for kvh in range(n_kv):
            s = scores(kt, kvh)
            mx = m_ref[kvh]
            p = jnp.concatenate([jnp.exp(s[:, c * T:(c + 1) * T] - mx) for c in range(nch)], axis=1)
            v1 = jnp.concatenate([vb_ref[0, key_rows(kt), kvh * T:(kvh + 1) * T], ones], axis=1)
            acc_ref[kvh] += _dot(p.astype(bf16), v1)
        return carry

    lax.fori_loop(0, nkt, pv_tile, 0)
    for kvh in range(n_kv):
        o = acc_ref[kvh, :, :T] / acc_ref[kvh, :, T:]
        for g in range(group):
            h = kvh * group + g
            o_ref[0, :, h * T:(h + 1) * T] = o[g * tq:(g + 1) * tq].astype(bf16)


def _dsa_prompt(q, iq, iwt, ik2, kb, vb, bound, *, topk, n_kv, n_ih, tq):
    b, s, qd = q.shape
    T = LANES
    n_heads = qd // T
    group = n_heads // n_kv
    nq = s // tq
    blk = lambda w: pl.BlockSpec((1, tq, w), lambda bi, qi: (bi, qi, 0))
    full = lambda w: pl.BlockSpec((1, s, w), lambda bi, qi: (bi, 0, 0))
    att = pltpu.VMEM((n_kv, group * tq, T), f32)
    return pl.pallas_call(
        functools.partial(_dsa_prompt_body, topk=topk, n_kv=n_kv, group=group, n_ih=n_ih,
                          idx_bits=int(s).bit_length(), tq=tq),
        grid=(b, nq),
        in_specs=[blk(qd), blk(iq.shape[2]), pl.BlockSpec((1, n_ih, tq), lambda bi, qi: (bi, 0, qi)),
                  full(T), full(n_kv * T), full(n_kv * T), pl.BlockSpec((1, T), lambda bi, qi: (0, 0))],
        out_specs=blk(qd),
        out_shape=jax.ShapeDtypeStruct((b, s, qd), bf16),
        scratch_shapes=[
            pltpu.VMEM((nq, tq, tq), f32),
            pltpu.VMEM((n_ih, tq, T), bf16),
            pltpu.VMEM((n_kv, group * tq, T), bf16), pltpu.VMEM((8, tq), i32),
            att, pltpu.VMEM((n_kv, group * tq, 2 * T), f32),
        ],
        compiler_params=_params("parallel", "arbitrary"),
        name="dsa_prompt",
    )(q, iq, iwt, ik2, kb, vb, bound)


def _dsa_sample_body(pt_ref, q_ref, iq_ref, iwb_ref, ikn_ref, kn_ref, vn_ref, *rest,
                     n_seq, n_pages, n_kv, topk, n_tok, idx_bits):
    del pt_ref
    T = LANES
    pages = [rest[i * n_pages:(i + 1) * n_pages] for i in range(3 * n_seq)]
    idx_pages, k_pages, v_pages = pages[:n_seq], pages[n_seq:2 * n_seq], pages[2 * n_seq:]
    o_ref, sc_ref, s_ref, tie_ref, thr_ref = rest[3 * n_seq * n_pages:]
    nt = n_pages + 1
    n_rows = q_ref.shape[1]
    sel_rows = n_seq * n_tok
    lane = lax.broadcasted_iota(i32, (sel_rows, T), 1)
    tok = lax.broadcasted_iota(i32, (sel_rows, T), 0) & (n_tok - 1)
    fresh = (lane <= tok) & (lane < n_tok)

    for s in range(n_seq):
        iq = iq_ref[s]
        iwb = iwb_ref[s]
        for p in range(nt):
            ikt = (idx_pages[s][p][...] if p < n_pages else ikn_ref[s]).astype(bf16)
            w = jnp.maximum(_dot(iq, ikt), 0.0) * iwb
            sc_ref[p, s * n_tok:(s + 1) * n_tok, :] = jnp.sum(w.reshape(n_rows // n_tok, n_tok, T), axis=0)
    sc_ref[n_pages] = jnp.where(fresh, sc_ref[n_pages], -jnp.inf)

    def count_fn(preds):
        accs = [jnp.zeros((sel_rows, T), f32) for _ in preds]
        for p in range(nt):
            sc = sc_ref[p]
            for n, pred in enumerate(preds):
                accs[n] = accs[n] + jnp.where(pred(sc, p * T + lane), 1.0, 0.0)
        return tuple(jnp.broadcast_to(jnp.sum(a, axis=1, keepdims=True), (sel_rows, T)) for a in accs)

    thr, _ = _select_topk(count_fn, tie_ref, (sel_rows, T), topk, idx_bits, 4)
    thr_ref[...] = thr

    lane_s = lax.broadcasted_iota(i32, (n_tok, T), 1)
    fresh_s = (lane_s <= lax.broadcasted_iota(i32, (n_tok, T), 0)) & (lane_s < n_tok)
    rows_kv = n_rows // n_kv

    def head_rows(ref, kvh):
        n_keys = ref.shape[0] // n_kv
        rows = ref[pl.ds(kvh, n_keys, stride=n_kv), :]
        if n_keys < T:
            rows = jnp.concatenate([rows, jnp.zeros((T - n_keys, T), f32)], axis=0)
        return rows.astype(bf16)

    for s in range(n_seq):
        q = q_ref[s]
        toks = slice(s * n_tok, (s + 1) * n_tok)
        thr_s, tie_s = thr_ref[toks, :], tie_ref[toks, :]
        mx = jnp.full((n_rows, T), NEG_BIG, f32)
        for p in range(nt):
            sc = sc_ref[p, toks, :]
            sel = (sc > thr_s) | ((sc == thr_s) & (p * T + lane_s < tie_s))
            if p == n_pages:
                sel = sel & fresh_s
            bias = jnp.concatenate([jnp.where(sel, 0.0, NEG_BIG)] * (rows_kv // n_tok), axis=0)
            k_ref = k_pages[s][p] if p < n_pages else kn_ref.at[s]
            sm = jnp.concatenate(
                [_dot_nt(q[kvh * rows_kv:(kvh + 1) * rows_kv], head_rows(k_ref, kvh)) + bias
                 for kvh in range(n_kv)], axis=0)
            s_ref[s, p] = sm
            mx = jnp.maximum(mx, sm)
        m = jnp.max(mx, axis=1, keepdims=True)
        l = jnp.zeros((n_rows, 1), f32)
        accs = [jnp.zeros((rows_kv, T), f32) for _ in range(n_kv)]
        for p in range(nt):
            e = jnp.exp(s_ref[s, p] - m)
            l = l + jnp.sum(e, axis=1, keepdims=True)
            eb = e.astype(bf16)
            v_ref = v_pages[s][p] if p < n_pages else vn_ref.at[s]
            for kvh in range(n_kv):
                accs[kvh] = accs[kvh] + _dot(eb[kvh * rows_kv:(kvh + 1) * rows_kv], head_rows(v_ref, kvh))
        o_ref[s] = (jnp.concatenate(accs, axis=0) / l).astype(bf16)


def _dsa_sample(page_table, layer, q, iq, iwb, ikn, kn, vn, cache_idx_t, cache_k, cache_v,
                *, topk, n_tok, n_kv):
    nb, n_rows, T = q.shape
    n_pages = page_table.shape[1]
    kv_rows = cache_k.shape[2]
    idw = cache_idx_t.shape[2]
    ns = SEQ_PER_STEP if nb % SEQ_PER_STEP == 0 else 1
    seq = lambda shape: pl.BlockSpec((ns,) + shape, lambda b, pt: (b, 0, 0))

    def pages(shape):
        return [pl.BlockSpec((None, None) + shape, lambda b, pt, s=s, p=p: (layer, pt[ns * b + s, p], 0, 0))
                for s in range(ns) for p in range(n_pages)]

    in_specs = [seq((n_rows, T)), seq((n_rows, idw)), seq((n_rows, T)), seq((idw, T)),
                seq(kn.shape[1:]), seq(vn.shape[1:])]
    in_specs += pages((idw, T)) + pages((kv_rows, T)) + pages((kv_rows, T))
    nt = n_pages + 1
    grid_spec = pltpu.PrefetchScalarGridSpec(
        num_scalar_prefetch=1,
        grid=(nb // ns,),
        in_specs=in_specs,
        out_specs=seq((n_rows, T)),
        scratch_shapes=[pltpu.VMEM((nt, ns * n_tok, T), f32), pltpu.VMEM((ns, nt, n_rows, T), f32),
                        pltpu.VMEM((ns * n_tok, T), i32), pltpu.VMEM((ns * n_tok, T), f32)],
    )
    return pl.pallas_call(
        functools.partial(_dsa_sample_body, n_seq=ns, n_pages=n_pages, n_kv=n_kv, topk=topk, n_tok=n_tok,
                          idx_bits=int(nt * T).bit_length()),
        grid_spec=grid_spec,
        out_shape=jax.ShapeDtypeStruct((nb, n_rows, T), bf16),
        compiler_params=_params("parallel"),
        name="dsa_sample",
    )(page_table, q, iq, iwb, ikn, kn, vn,
      *([cache_idx_t] * (ns * n_pages)), *([cache_k] * (ns * n_pages)), *([cache_v] * (ns * n_pages)))


def _rope_tables(pos, dim):
    half = dim // 2
    inv = ROPE_THETA ** (-jnp.arange(half, dtype=f32) * (2.0 / dim))
    ang = pos.astype(f32)[:, None] * inv[None, :]
    cos = jnp.concatenate([jnp.cos(ang), jnp.cos(ang)], axis=1)
    sin = jnp.concatenate([-jnp.sin(ang), jnp.sin(ang)], axis=1)
    reps = LANES // dim
    return jnp.tile(cos, (1, reps)), jnp.tile(sin, (1, reps))


def _mix_tables(w_s, b_s, c, group_dim):
    causal = jnp.tril(jnp.ones((c, c), dtype=bool))
    ws = jnp.where(causal[None], w_s[:, :c, :c], 0)
    reps = LANES // c
    ws = jnp.tile(ws, (1, reps, reps))
    blk = jnp.arange(LANES) // c
    ws = jnp.where((blk[:, None] == blk[None, :])[None], ws, 0).astype(bf16)
    bias = jnp.tile(jnp.repeat(b_s[:, :c].T, group_dim, axis=1), (reps, 1)).astype(f32)
    return ws, bias


def kernel(x_prompt, x_sample, cache_k, cache_v, cache_idx_k, state_ffn_conv, page_table, a_norm, a_w_in, a_v_norm, a_w_s, a_b_s, a_w_out, b_norm, b_w_in, b_q_norm, b_k_norm, b_w_o, f_norm, f_w_in, f_conv_w, f_conv_b, f_w_out):
    bsz, seq, d = x_prompt.shape
    nb, n_tok, _ = x_sample.shape
    depth = f_norm.shape[0]
    n_layers_b, n_pool, page, n_kv, hd = cache_k.shape
    idw = cache_idx_k.shape[3]
    past = page_table.shape[1] * page
    n_heads = b_w_o.shape[1] // hd
    group = n_heads // n_kv
    n_ih = (b_w_in.shape[2] - (n_heads + 2 * n_kv) * hd - idw) // (idw + 1)
    ff = f_w_out.shape[1]
    a_groups, chunk = a_w_s.shape[1], a_w_s.shape[2]
    a_width = a_w_in.shape[2] // 2
    assert hd == LANES and idw == 64 and page == LANES and chunk == LANES and n_ih % 2 == 0
    assert f_conv_w.shape[1] == 3 and n_tok >= 2 and LANES % n_tok == 0 and n_tok & (n_tok - 1) == 0

    mp, ms = bsz * seq, nb * n_tok
    xp = x_prompt.reshape(mp, d)
    xs = x_sample.reshape(ms, d)
    tm_p = min(TM_ROWS, seq)
    tm_s = min(TM_ROWS, ms)
    tq = min(TQ_PROMPT, seq)

    aw_in, aw_out = a_w_in.astype(bf16), a_w_out.astype(bf16)
    n_real = b_w_in.shape[2]
    n_pad = -(-n_real // LANES) * LANES
    bw_in = jnp.pad(b_w_in, ((0, 0), (0, 0), (0, n_pad - n_real))).astype(bf16)
    bw_o = b_w_o.astype(bf16)
    tn_b = LANES * 11 if n_pad % (LANES * 11) == 0 else LANES
    tn_f = 512 if ff % 512 == 0 else LANES
    fw_in, fw_out = f_w_in.astype(bf16), f_w_out.astype(bf16)

    ck = cache_k.reshape(n_layers_b, n_pool, page * n_kv, hd)
    cv = cache_v.reshape(n_layers_b, n_pool, page * n_kv, hd)
    cidx_t = jnp.swapaxes(cache_idx_k, 2, 3)
    outs = dict(k_p=[], v_p=[], ik_p=[], k_s=[], v_s=[], ik_s=[], chunk_v=[], conv_p=[], conv_s=[])
    n_mixers = 2
    for layer in range(depth):
        j = layer // n_mixers
        if layer % n_mixers == 0:
            new = []
            for x, tm, c, want in ((xp, tm_p, min(seq, chunk), False), (xs, tm_s, min(n_tok, chunk), True)):
                wmix, bias = _mix_tables(a_w_s[j], a_b_s[j], c, a_width // a_groups)
                z = _norm_matmul(x, a_norm, aw_in, j, act="gelu", tm=min(TM_FFN, x.shape[0]),
                                 tn=1024 if a_width % 1024 == 0 else 512)
                gated, vn = _spatial_gate(z, a_v_norm, j, wmix, bias, tm=min(tm, 512), want_vn=want)
                new.append(_matmul_residual(gated, aw_out, j, x, tm=tm, tn=d))
                if want:
                    outs["chunk_v"].append(vn.reshape(nb, n_tok, a_width))
            xp, xs = new
        else:
            pos_p = jnp.arange(seq, dtype=i32)
            pos_s = jnp.tile(past + jnp.arange(n_tok, dtype=i32), nb)
            res = []
            for x, tm, pos in ((xp, tm_p, pos_p), (xs, tm_s, pos_s)):
                p = _norm_matmul(x, b_norm, bw_in, j, tm=min(TM_FFN, x.shape[0]), tn=tn_b)
                tabs = _rope_tables(pos, hd) + _rope_tables(pos, idw)
                res.append(_attn_post(p, b_q_norm, b_k_norm, j, tabs, tm=tm,
                                      n_heads=n_heads, n_kv=n_kv, n_ih=n_ih))
            q, k, v, kb, vb, iq, ikw, ik2 = res[0]
            r3 = lambda a: a.reshape(bsz, seq, a.shape[1])
            iwt = jnp.swapaxes(ikw[:, idw:idw + n_ih].reshape(bsz, seq, n_ih), 1, 2)
            bound = 1.01 * math.sqrt(hd) * jnp.max(jnp.abs(b_q_norm[j])) * jnp.max(jnp.abs(b_k_norm[j]))
            bound = jnp.full((1, LANES), bound, f32)
            o = _dsa_prompt(r3(q), r3(iq), iwt, r3(ik2), r3(kb), r3(vb), bound,
                            topk=min(TOPK_MAX, seq // 4), n_kv=n_kv, n_ih=n_ih, tq=tq)
            xp = _matmul_residual(o.reshape(mp, n_heads * hd), bw_o, j, xp, tm=tm_p, tn=d)
            n_seq_pages = seq // page
            outs["k_p"].append(k.reshape(bsz, n_seq_pages, page, n_kv, hd))
            outs["v_p"].append(v.reshape(bsz, n_seq_pages, page, n_kv, hd))
            outs["ik_p"].append(ikw[:, :idw].reshape(bsz, n_seq_pages, page, idw))

            q, k, v, kb, vb, iq, ikw, ik2 = res[1]
            hm = lambda a, w: a.reshape(nb, n_tok, a.shape[1] // w, w).transpose(0, 2, 1, 3).reshape(nb, -1, w)
            qs = hm(q, hd)
            iqs = hm(iq, idw)
            iw = ikw[:, idw:idw + n_ih].reshape(nb, n_tok, n_ih).transpose(0, 2, 1).reshape(nb, n_ih * n_tok, 1)
            iwb = jnp.broadcast_to(iw, (nb, n_ih * n_tok, LANES))
            pad_tok = lambda a: jnp.pad(a.reshape(nb, n_tok, -1), ((0, 0), (0, page - n_tok), (0, 0)))
            ikn = jnp.swapaxes(pad_tok(ikw[:, :idw]), 1, 2)
            kn = k.reshape(nb, n_tok * n_kv, hd)
            vn_ = v.reshape(nb, n_tok * n_kv, hd)
            os_ = _dsa_sample(page_table, j, qs, iqs, iwb, ikn, kn, vn_, cidx_t, ck, cv,
                              topk=min(TOPK_MAX, (past + n_tok) // 4), n_tok=n_tok, n_kv=n_kv)
            os_ = os_.reshape(nb, n_heads, n_tok, hd).transpose(0, 2, 1, 3).reshape(ms, n_heads * hd)
            xs = _matmul_residual(os_, bw_o, j, xs, tm=tm_s, tn=d)
            outs["k_s"].append(k.reshape(nb, n_tok, n_kv, hd))
            outs["v_s"].append(v.reshape(nb, n_tok, n_kv, hd))
            outs["ik_s"].append(ikw[:, :idw].reshape(nb, n_tok, idw))

        tm_f = min(TM_FFN, seq)
        xp, tg, tu = _conv_ffn(xp, f_norm, fw_in, f_conv_w, f_conv_b, fw_out, layer,
                               tm=tm_f, tn=tn_f, seq_len=seq)
        tails = jnp.concatenate([tg, tu], axis=2).reshape(bsz, seq // tm_f, 8, 2 * ff)
        outs["conv_p"].append(tails[:, -1, 6:8, :])
        st = state_ffn_conv[layer]
        st = jnp.pad(st, ((0, 0), (0, n_tok - 2), (0, 0))).reshape(ms, 2 * ff)
        xs, ng, nu = _conv_ffn(xs, f_norm, fw_in, f_conv_w, f_conv_b, fw_out, layer,
                               tm=min(TM_FFN, ms), tn=tn_f, seq_len=n_tok, state=st)
        outs["conv_s"].append(jnp.concatenate([ng, nu], axis=2))

    st = lambda key: jnp.stack(outs[key])
    return (xp.reshape(bsz, seq, d), xs.reshape(nb, n_tok, d), st("k_p"), st("v_p"), st("ik_p"),
            st("k_s"), st("v_s"), st("ik_s"), st("chunk_v"), st("conv_p"), st("conv_s"))
```

```python
import functools
import math

import jax
import jax.numpy as jnp
from jax import lax
from jax.experimental import pallas as pl
from jax.experimental.pallas import tpu as pltpu

EPS = 1e-6
ROPE_THETA = 10000.0
TOPK_MAX = 256
LANES = 128
NEG_BIG = -1e30
INT_MIN = -(2 ** 31)
KEY_NEG_INF = -2139095041
TIE_ALL = 2 ** 30
SAFE_SHIFT_BOUND = 40.0
VMEM_LIMIT = 56 * 1024 * 1024
TM_ROWS = 512
TM_FFN = 1024
TQ_PROMPT = 256
SEQ_PER_STEP = 2
COUNT_CHAINS = 4
EXIT_CHECK_BITS = (24, 28)

f32 = jnp.float32
bf16 = jnp.bfloat16
i32 = jnp.int32


def _params(*sem):
    return pltpu.CompilerParams(dimension_semantics=sem, vmem_limit_bytes=VMEM_LIMIT)


def _dot(a, b):
    return jnp.dot(a, b, preferred_element_type=f32)


def _dot_nt(a, b):
    return lax.dot_general(a, b, (((1,), (1,)), ((), ())), preferred_element_type=f32)


def _rms(x, g):
    r = lax.rsqrt(jnp.mean(x * x, axis=-1, keepdims=True) + EPS)
    return x * r * g


def _rows(a):
    return a.reshape(a.shape[0], 1, a.shape[1])


def _gelu_tanh(x):
    c = math.sqrt(2.0 / math.pi)
    return 0.5 * x * (1.0 + jnp.tanh(c * (x + 0.044715 * (x * x * x))))


def _nmm_body(x_ref, g_ref, w_ref, o_ref, h_ref, *, act):
    @pl.when(pl.program_id(1) == 0)
    def _():
        h_ref[...] = _rms(x_ref[...], g_ref[...]).astype(bf16)

    z = _dot(h_ref[...], w_ref[...])
    if act == "gelu":
        z = _gelu_tanh(z)
    o_ref[...] = z.astype(o_ref.dtype)


def _norm_matmul(x, gain, w, layer, *, act=None, tm, tn, out_dtype=f32):
    m, k = x.shape
    n = w.shape[2]
    return pl.pallas_call(
        functools.partial(_nmm_body, act=act),
        grid=(m // tm, n // tn),
        in_specs=[
            pl.BlockSpec((tm, k), lambda i, j: (i, 0)),
            pl.BlockSpec((None, 1, k), lambda i, j: (layer, 0, 0)),
            pl.BlockSpec((None, k, tn), lambda i, j: (layer, 0, j)),
        ],
        out_specs=pl.BlockSpec((tm, tn), lambda i, j: (i, j)),
        out_shape=jax.ShapeDtypeStruct((m, n), out_dtype),
        scratch_shapes=[pltpu.VMEM((tm, k), bf16)],
        compiler_params=_params("parallel", "arbitrary"),
        name="norm_matmul",
    )(x, _rows(gain), w)


def _mmres_body(a_ref, w_ref, x_ref, o_ref):
    o_ref[...] = x_ref[...] + _dot(a_ref[...], w_ref[...])


def _matmul_residual(a, w, layer, x, *, tm, tn):
    m, k = a.shape
    n = w.shape[2]
    return pl.pallas_call(
        _mmres_body,
        grid=(m // tm, n // tn),
        in_specs=[
            pl.BlockSpec((tm, k), lambda i, j: (i, 0)),
            pl.BlockSpec((None, k, tn), lambda i, j: (layer, 0, j)),
            pl.BlockSpec((tm, tn), lambda i, j: (i, j)),
        ],
        out_specs=pl.BlockSpec((tm, tn), lambda i, j: (i, j)),
        out_shape=jax.ShapeDtypeStruct((m, n), f32),
        compiler_params=_params("parallel", "arbitrary"),
        name="matmul_residual",
    )(a, w, x)


def _gate_body(u_ref, v_ref, vg_ref, wmix_ref, bias_ref, o_ref, *vn_out, tm, groups):
    vn = _rms(v_ref[...], vg_ref[...])
    if vn_out:
        vn_out[0][...] = vn
    vnb = vn.astype(bf16)
    gd = v_ref.shape[1] // groups
    for c in range(tm // LANES):
        rows = slice(c * LANES, (c + 1) * LANES)
        for g in range(groups):
            cols = slice(g * gd, (g + 1) * gd)
            s = _dot(wmix_ref[g], vnb[rows, cols]) + bias_ref[:, cols]
            o_ref[rows, cols] = (u_ref[rows, cols] * s).astype(bf16)


def _spatial_gate(z, v_gain, layer, wmix, bias, *, tm, want_vn):
    m, w2 = z.shape
    w = w2 // 2
    groups = wmix.shape[0]
    out_shape = [jax.ShapeDtypeStruct((m, w), bf16)]
    out_specs = [pl.BlockSpec((tm, w), lambda i: (i, 0))]
    if want_vn:
        out_shape.append(jax.ShapeDtypeStruct((m, w), f32))
        out_specs.append(pl.BlockSpec((tm, w), lambda i: (i, 0)))
    res = pl.pallas_call(
        functools.partial(_gate_body, tm=tm, groups=groups),
        grid=(m // tm,),
        in_specs=[
            pl.BlockSpec((tm, w), lambda i: (i, 0)),
            pl.BlockSpec((tm, w), lambda i: (i, 1)),
            pl.BlockSpec((None, 1, w), lambda i: (layer, 0, 0)),
            pl.BlockSpec(wmix.shape, lambda i: (0, 0, 0)),
            pl.BlockSpec(bias.shape, lambda i: (0, 0)),
        ],
        out_specs=out_specs,
        out_shape=out_shape,
        compiler_params=_params("parallel"),
        name="spatial_gate",
    )(z, z, _rows(v_gain), wmix, bias)
    return res if want_vn else (res[0], None)


def _conv3(a, cw, cb, prev1, prev2):
    return cb + cw[0:1] * prev2 + cw[1:2] * prev1 + cw[2:3] * a


def _ffn_body(x_ref, g_ref, wg_ref, wu_ref, cwg_ref, cwu_ref, cbg_ref, cbu_ref, wo_ref, *rest,
              sample, seq_tiles, seq_len, nj):
    if sample:
        sg_ref, su_ref, o_ref, ag_ref, au_ref, h_ref, act0_ref, act1_ref = rest
    else:
        o_ref, tg_ref, tu_ref, h_ref, act0_ref, act1_ref, carry_ref = rest
    acts = (act0_ref, act1_ref)
    i = pl.program_id(0)
    j = pl.program_id(1)

    @pl.when(j == 0)
    def _():
        x = x_ref[...]
        h_ref[...] = _rms(x, g_ref[...]).astype(bf16)
        o_ref[...] = x
        act1_ref[...] = jnp.zeros(act1_ref.shape, bf16)

    def down(src_ref):
        o_ref[...] += _dot(src_ref[...], wo_ref[...])

    def up_conv_gate(dst_ref):
        h = h_ref[...]
        ag = _dot(h, wg_ref[...])
        au = _dot(h, wu_ref[...])
        tm, tn = ag.shape
        row = lax.broadcasted_iota(i32, (tm, tn), 0)
        if sample:
            t = row & (seq_len - 1)
            m1 = t == 0
            m2 = t < 2
            p2g, p2u = sg_ref[...], su_ref[...]
            p1g, p1u = pltpu.roll(p2g, tm - 1, 0), pltpu.roll(p2u, tm - 1, 0)
            for a, a_ref in ((ag, ag_ref), (au, au_ref)):
                last = pltpu.roll(a, tm - (seq_len - 2), 0).reshape(tm // seq_len, seq_len, tn)
                a_ref[...] = last[:, :2, :]
        else:
            m1 = row == 0
            m2 = row < 2
            live = (i % seq_tiles) != 0
            cg = jnp.where(live, carry_ref[j, 0], 0.0)
            cu = jnp.where(live, carry_ref[j, 1], 0.0)
            p1g, p1u = cg[7:8], cu[7:8]
            p2g = jnp.where(m1, cg[6:7], cg[7:8])
            p2u = jnp.where(m1, cu[6:7], cu[7:8])
            carry_ref[j, 0] = ag[tm - 8:]
            carry_ref[j, 1] = au[tm - 8:]
            tg_ref[0] = ag[tm - 8:]
            tu_ref[0] = au[tm - 8:]
        a1g = jnp.where(m1, p1g, pltpu.roll(ag, 1, 0))
        a2g = jnp.where(m2, p2g, pltpu.roll(ag, 2, 0))
        a1u = jnp.where(m1, p1u, pltpu.roll(au, 1, 0))
        a2u = jnp.where(m2, p2u, pltpu.roll(au, 2, 0))
        cg_ = _conv3(ag, cwg_ref[...], cbg_ref[...], a1g, a2g)
        cu_ = _conv3(au, cwu_ref[...], cbu_ref[...], a1u, a2u)
        dst_ref[...] = (cg_ * (1.0 / (1.0 + jnp.exp(-cg_))) * cu_).astype(bf16)

    for parity in (0, 1):
        @pl.when((j < nj) & ((j & 1) == parity))
        def _(parity=parity):
            down(acts[1 - parity])
            up_conv_gate(acts[parity])

    @pl.when(j == nj)
    def _():
        down(acts[(nj - 1) % 2])


def _conv_ffn(x, gain, w_in, conv_w, conv_b, w_out, layer, *, tm, tn, seq_len, state=None):
    m, d = x.shape
    ff = w_out.shape[1]
    nj = ff // tn
    ni = m // tm
    sample = state is not None
    up = lambda j: jnp.minimum(j, nj - 1)
    down = lambda j: jnp.maximum(j - 1, 0)
    in_specs = [
        pl.BlockSpec((tm, d), lambda i, j: (i, 0), pipeline_mode=pl.Buffered(1)),
        pl.BlockSpec((None, 1, d), lambda i, j: (layer, 0, 0)),
        pl.BlockSpec((None, d, tn), lambda i, j: (layer, 0, up(j))),
        pl.BlockSpec((None, d, tn), lambda i, j: (layer, 0, nj + up(j))),
        pl.BlockSpec((None, 3, tn), lambda i, j: (layer, 0, up(j))),
        pl.BlockSpec((None, 3, tn), lambda i, j: (layer, 0, nj + up(j))),
        pl.BlockSpec((None, 1, tn), lambda i, j: (layer, 0, up(j))),
        pl.BlockSpec((None, 1, tn), lambda i, j: (layer, 0, nj + up(j))),
        pl.BlockSpec((None, tn, d), lambda i, j: (layer, down(j), 0)),
    ]
    conv_b = _rows(conv_b)
    args = [x, _rows(gain), w_in, w_in, conv_w, conv_w, conv_b, conv_b, w_out]
    out_specs = [pl.BlockSpec((tm, d), lambda i, j: (i, 0), pipeline_mode=pl.Buffered(1 if ni == 1 else 2))]
    out_shape = [jax.ShapeDtypeStruct((m, d), f32)]
    scratch = [pltpu.VMEM((tm, d), bf16), pltpu.VMEM((tm, tn), bf16), pltpu.VMEM((tm, tn), bf16)]
    if sample:
        in_specs += [
            pl.BlockSpec((tm, tn), lambda i, j: (i, up(j))),
            pl.BlockSpec((tm, tn), lambda i, j: (i, nj + up(j))),
        ]
        args += [state, state]
        out_specs += [pl.BlockSpec((tm // seq_len, 2, tn), lambda i, j: (i, 0, up(j)))] * 2
        out_shape += [jax.ShapeDtypeStruct((m // seq_len, 2, ff), f32)] * 2
        seq_tiles = 1
    else:
        out_specs += [pl.BlockSpec((1, 8, tn), lambda i, j: (i, 0, up(j)))] * 2
        out_shape += [jax.ShapeDtypeStruct((ni, 8, ff), f32)] * 2
        scratch.append(pltpu.VMEM((nj, 2, 8, tn), f32))
        seq_tiles = seq_len // tm
    return pl.pallas_call(
        functools.partial(_ffn_body, sample=sample, seq_tiles=seq_tiles, seq_len=seq_len, nj=nj),
        grid=(ni, nj + 1),
        in_specs=in_specs,
        out_specs=out_specs,
        out_shape=out_shape,
        scratch_shapes=scratch,
        compiler_params=_params("arbitrary", "arbitrary"),
        name="conv_ffn_sample" if sample else "conv_ffn_prompt",
    )(*args)


def _rope_full(x, c, s):
    return x * c + pltpu.roll(x, LANES // 2, 1) * s


def _rope_half(x, c, s, lo):
    partner = jnp.where(lo, pltpu.roll(x, 96, 1), pltpu.roll(x, 32, 1))
    return x * c + partner * s


def _post_body(p_ref, qg_ref, kg_ref, c1_ref, s1_ref, c2_ref, s2_ref,
               q_ref, k_ref, v_ref, kb_ref, vb_ref, iq_ref, ikw_ref, ik2_ref,
               *, n_heads, n_kv, n_ih, idx_scale, q_scale):
    hd = LANES
    c1, s1, c2, s2 = c1_ref[...], s1_ref[...], c2_ref[...], s2_ref[...]
    tm = p_ref.shape[0]
    lane = lax.broadcasted_iota(i32, (tm, LANES), 1)
    lo = (lane & 63) < 32
    off = 0
    for h in range(n_heads):
        x = p_ref[:, off + h * hd: off + (h + 1) * hd]
        q_ref[:, h * hd:(h + 1) * hd] = (_rope_full(_rms(x, qg_ref[...]), c1, s1) * q_scale).astype(bf16)
    off += n_heads * hd
    for h in range(n_kv):
        x = p_ref[:, off + h * hd: off + (h + 1) * hd]
        kr = _rope_full(_rms(x, kg_ref[...]), c1, s1)
        k_ref[pl.ds(h, tm, stride=n_kv), :] = kr
        kb_ref[:, h * hd:(h + 1) * hd] = kr.astype(bf16)
    off += n_kv * hd
    vv = p_ref[:, off: off + n_kv * hd]
    for h in range(n_kv):
        v_ref[pl.ds(h, tm, stride=n_kv), :] = vv[:, h * hd:(h + 1) * hd]
    vb_ref[...] = vv.astype(bf16)
    off += n_kv * hd
    for h in range(n_ih // 2):
        x = p_ref[:, off + h * LANES: off + (h + 1) * LANES]
        iq_ref[:, h * LANES:(h + 1) * LANES] = _rope_half(x, c2, s2, lo).astype(bf16)
    off += (n_ih // 2) * LANES
    tail = p_ref[:, off: off + LANES]
    tr = _rope_half(tail, c2, s2, lo)
    ik = jnp.where(lane < 64, tr, 0.0)
    ik2_ref[...] = (ik + pltpu.roll(ik, 64, 1)).astype(bf16)
    ikw_ref[...] = jnp.where(lane < 64, tr, tail * idx_scale)


def _attn_post(p, q_gain, k_gain, layer, tabs, *, tm, n_heads, n_kv, n_ih):
    m, npad = p.shape
    c1, s1, c2, s2 = tabs
    nt = c1.shape[0] // tm
    hd = LANES
    tab_spec = pl.BlockSpec((tm, LANES), lambda i: (i % nt, 0))
    row = lambda w: pl.BlockSpec((tm, w), lambda i: (i, 0))
    gain_spec = pl.BlockSpec((None, 1, hd), lambda i: (layer, 0, 0))
    outs = [
        (1, n_heads * hd, bf16), (n_kv, hd, f32), (n_kv, hd, f32), (1, n_kv * hd, bf16), (1, n_kv * hd, bf16),
        (1, n_ih * 64, bf16), (1, LANES, f32), (1, LANES, bf16),
    ]
    return pl.pallas_call(
        functools.partial(_post_body, n_heads=n_heads, n_kv=n_kv, n_ih=n_ih,
                          idx_scale=float((n_ih * 64) ** -0.5), q_scale=float(hd ** -0.5)),
        grid=(m // tm,),
        in_specs=[row(npad), gain_spec, gain_spec, tab_spec, tab_spec, tab_spec, tab_spec],
        out_specs=[pl.BlockSpec((tm * r, w), lambda i: (i, 0)) for r, w, _ in outs],
        out_shape=[jax.ShapeDtypeStruct((m * r, w), dt) for r, w, dt in outs],
        compiler_params=_params("parallel"),
        name="attn_post",
    )(p, _rows(q_gain), _rows(k_gain), c1, s1, c2, s2)


def _key_to_float(key):
    bits = key ^ ((key >> 31) & 0x7FFFFFFF)
    return lax.bitcast_convert_type(bits, f32)


def _select_threshold(count_fn, res_ref, shape, topk, bits_per_round):
    kk = float(topk)
    n_cand = (1 << bits_per_round) - 1

    def round_body(it, carry):
        res, cnt_res = carry
        step = lax.shift_left(jnp.int32(1), 32 - bits_per_round * (it + 1))
        cands = [res + step * (c + 1) for c in range(n_cand)]
        floats = [_key_to_float(c) for c in cands]
        cnts = count_fn(tuple((lambda sc, idx, cf=cf: sc >= cf) for cf in floats))
        for c, n in zip(cands, cnts):
            res = jnp.where(n >= kk, c, res)
            cnt_res = jnp.where(n >= kk, n, cnt_res)
        return res, cnt_res

    stops = [-(-b // bits_per_round) for b in EXIT_CHECK_BITS] + [32 // bits_per_round]
    res, cnt = lax.fori_loop(0, stops[0], round_body, (jnp.full(shape, INT_MIN, i32), jnp.zeros(shape, f32)))
    res_ref[...] = res

    def continue_from(k, res, cnt):
        unsettled = jnp.where((cnt != kk) & (res != INT_MIN), 1.0, 0.0)

        @pl.when(jnp.max(unsettled) > 0.0)
        def _():
            r, c = lax.fori_loop(stops[k], stops[k + 1], round_body, (res, cnt))
            res_ref[...] = r
            if k + 2 < len(stops):
                continue_from(k + 1, r, c)

    continue_from(0, res, cnt)
    return _key_to_float(jnp.maximum(res_ref[...], KEY_NEG_INF))


def _tie_search(count_fn, thr, need, shape, idx_bits):
    def tie_body(it, end):
        cand = end + lax.shift_left(jnp.int32(1), idx_bits - 1 - it)
        (cnt,) = count_fn((lambda sc, idx: (sc == thr) & (idx < cand),))
        return jnp.where(cnt <= need, cand, end)

    return lax.fori_loop(0, idx_bits, tie_body, jnp.zeros(shape, i32))


def _select_topk(count_fn, tie_ref, shape, topk, idx_bits, bits_per_round):
    kk = float(topk)
    thr = _select_threshold(count_fn, tie_ref, shape, topk, bits_per_round)
    n_gt, n_ge = count_fn((lambda sc, idx: sc > thr, lambda sc, idx: sc >= thr))
    tie_ref[...] = jnp.full(shape, TIE_ALL, i32)
    crowded = jnp.where((n_ge > kk) & (thr > -jnp.inf), 1.0, 0.0)

    @pl.when(jnp.max(crowded) > 0.0)
    def _():
        tie_ref[...] = _tie_search(count_fn, thr, kk - n_gt, shape, idx_bits)

    return thr, tie_ref[...]


def _dsa_prompt_body(q_ref, iq_ref, iwt_ref, ik2_ref, kb_ref, vb_ref, bound_ref, o_ref,
                     sc_ref, iqm_ref, qg_ref, tie_ref, m_ref, acc_ref,
                     *, topk, n_kv, group, n_ih, idx_bits, tq):
    T = LANES
    nch = tq // T
    strips = tq // 8
    qb = pl.program_id(1)
    nkt = qb + 1
    lane = lax.broadcasted_iota(i32, (tq, T), 1)

    for h in range(n_ih):
        pair = iq_ref[0, :, (h // 2) * T:(h // 2 + 1) * T].astype(f32)
        keep = (lane < 64) if h % 2 == 0 else (lane >= 64)
        iqm_ref[h] = jnp.where(keep, pair, 0.0).astype(bf16)
    for h in range(n_kv * group):
        qg_ref[h // group, (h % group) * tq:(h % group + 1) * tq, :] = q_ref[0, :, h * T:(h + 1) * T]

    def key_rows(kt):
        return pl.ds(pl.multiple_of(kt * tq, tq), tq)

    key_in_tile = (lax.broadcasted_iota(i32, (strips, 8, tq), 0) * 8
                   + lax.broadcasted_iota(i32, (strips, 8, tq), 1))
    qidx = qb * tq + lax.broadcasted_iota(i32, (strips, 8, tq), 2)

    def score_tile(kt, carry):
        ik_t = ik2_ref[0, key_rows(kt), :]
        acc = jnp.zeros((tq, tq), f32)
        for h in range(n_ih):
            acc = acc + jnp.maximum(_dot_nt(ik_t, iqm_ref[h]), 0.0) * iwt_ref[0, h:h + 1, :]
        admissible = kt * tq + key_in_tile <= qidx
        sc_ref[kt] = jnp.where(admissible, acc.reshape(strips, 8, tq), -jnp.inf).reshape(tq, tq)
        return carry

    lax.fori_loop(0, nkt, score_tile, 0)

    def count_fn(preds):
        def body(kt, accs):
            sc = sc_ref[kt].reshape(strips, 8, tq)
            kidx = kt * tq + key_in_tile
            return tuple(a + jnp.sum(jnp.where(pred(sc, kidx), 1.0, 0.0).reshape(-1, 8 * COUNT_CHAINS, tq), axis=0)
                         for a, pred in zip(accs, preds))

        accs = lax.fori_loop(0, nkt, body, tuple(jnp.zeros((8 * COUNT_CHAINS, tq), f32) for _ in preds))
        return tuple(jnp.broadcast_to(jnp.sum(a, axis=0, keepdims=True), (8, tq)) for a in accs)

    thr, tie_end = _select_topk(count_fn, tie_ref, (8, tq), topk, idx_bits, 1)

    def bias_tile(kt, carry):
        sc = sc_ref[kt].reshape(strips, 8, tq)
        kidx = kt * tq + key_in_tile
        sel = ((sc > thr) | ((sc == thr) & (kidx < tie_end))) & (kidx <= qidx)
        sc_ref[kt] = jnp.where(sel, 0.0, NEG_BIG).reshape(tq, tq).T
        return carry

    lax.fori_loop(0, nkt, bias_tile, 0)

    gq = group * tq

    def scores(kt, kvh):
        k_t = kb_ref[0, key_rows(kt), kvh * T:(kvh + 1) * T]
        s = _dot_nt(qg_ref[kvh], k_t)
        return (s.reshape(group, tq, tq) + sc_ref[kt][None]).reshape(gq, tq)

    bound = bound_ref[...]
    m_ref[...] = jnp.broadcast_to(bound[None], m_ref.shape)

    @pl.when(jnp.max(bound) > SAFE_SHIFT_BOUND)
    def _():
        m_ref[...] = jnp.full(m_ref.shape, NEG_BIG, f32)

        def max_tile(kt, carry):
            for kvh in range(n_kv):
                s = scores(kt, kvh)
                mx = m_ref[kvh]
                for c in range(nch):
                    mx = jnp.maximum(mx, s[:, c * T:(c + 1) * T])
                m_ref[kvh] = mx
            return carry

        lax.fori_loop(0, nkt, max_tile, 0)
        for kvh in range(n_kv):
            m_ref[kvh] = jnp.broadcast_to(jnp.max(m_ref[kvh], axis=1, keepdims=True), (gq, T))

    acc_ref[...] = jnp.zeros(acc_ref.shape, f32)
    ones = jnp.ones((tq, T), bf16)

    def pv_tile(kt, carry):
        for kvh in range(n_kv):
            s = scores(kt, kvh)
            mx = m_ref[kvh]
            p = jnp.concatenate([jnp.exp(s[:, c * T:(c + 1) * T] - mx) for c in range(nch)], axis=1)
            v1 = jnp.concatenate([vb_ref[0, key_rows(kt), kvh * T:(kvh + 1) * T], ones], axis=1)
            acc_ref[kvh] += _dot(p.astype(bf16), v1)
        return carry

    lax.fori_loop(0, nkt, pv_tile, 0)
    for kvh in range(n_kv):
        o = acc_ref[kvh, :, :T] / acc_ref[kvh, :, T:]
        for g in range(group):
            h = kvh * group + g
            o_ref[0, :, h * T:(h + 1) * T] = o[g * tq:(g + 1) * tq].astype(bf16)


def _dsa_prompt(q, iq, iwt, ik2, kb, vb, bound, *, topk, n_kv, n_ih, tq):
    b, s, qd = q.shape
    T = LANES
    n_heads = qd // T
    group = n_heads // n_kv
    nq = s // tq
    blk = lambda w: pl.BlockSpec((1, tq, w), lambda bi, qi: (bi, qi, 0))
    full = lambda w: pl.BlockSpec((1, s, w), lambda bi, qi: (bi, 0, 0))
    att = pltpu.VMEM((n_kv, group * tq, T), f32)
    return pl.pallas_call(
        functools.partial(_dsa_prompt_body, topk=topk, n_kv=n_kv, group=group, n_ih=n_ih,
                          idx_bits=int(s).bit_length(), tq=tq),
        grid=(b, nq),
        in_specs=[blk(qd), blk(iq.shape[2]), pl.BlockSpec((1, n_ih, tq), lambda bi, qi: (bi, 0, qi)),
                  full(T), full(n_kv * T), full(n_kv * T), pl.BlockSpec((1, T), lambda bi, qi: (0, 0))],
        out_specs=blk(qd),
        out_shape=jax.ShapeDtypeStruct((b, s, qd), bf16),
        scratch_shapes=[
            pltpu.VMEM((nq, tq, tq), f32),
            pltpu.VMEM((n_ih, tq, T), bf16),
            pltpu.VMEM((n_kv, group * tq, T), bf16), pltpu.VMEM((8, tq), i32),
            att, pltpu.VMEM((n_kv, group * tq, 2 * T), f32),
        ],
        compiler_params=_params("parallel", "arbitrary"),
        name="dsa_prompt",
    )(q, iq, iwt, ik2, kb, vb, bound)


def _dsa_sample_body(pt_ref, q_ref, iq_ref, iwb_ref, ikn_ref, kn_ref, vn_ref, *rest,
                     n_seq, n_pages, n_kv, topk, n_tok, idx_bits):
    del pt_ref
    T = LANES
    pages = [rest[i * n_pages:(i + 1) * n_pages] for i in range(3 * n_seq)]
    idx_pages, k_pages, v_pages = pages[:n_seq], pages[n_seq:2 * n_seq], pages[2 * n_seq:]
    o_ref, sc_ref, s_ref, tie_ref, thr_ref = rest[3 * n_seq * n_pages:]
    nt = n_pages + 1
    n_rows = q_ref.shape[1]
    sel_rows = n_seq * n_tok
    lane = lax.broadcasted_iota(i32, (sel_rows, T), 1)
    tok = lax.broadcasted_iota(i32, (sel_rows, T), 0) & (n_tok - 1)
    fresh = (lane <= tok) & (lane < n_tok)

    for s in range(n_seq):
        iq = iq_ref[s]
        iwb = iwb_ref[s]
        for p in range(nt):
            ikt = (idx_pages[s][p][...] if p < n_pages else ikn_ref[s]).astype(bf16)
            w = jnp.maximum(_dot(iq, ikt), 0.0) * iwb
            sc_ref[p, s * n_tok:(s + 1) * n_tok, :] = jnp.sum(w.reshape(n_rows // n_tok, n_tok, T), axis=0)
    sc_ref[n_pages] = jnp.where(fresh, sc_ref[n_pages], -jnp.inf)

    def count_fn(preds):
        accs = [jnp.zeros((sel_rows, T), f32) for _ in preds]
        for p in range(nt):
            sc = sc_ref[p]
            for n, pred in enumerate(preds):
                accs[n] = accs[n] + jnp.where(pred(sc, p * T + lane), 1.0, 0.0)
        return tuple(jnp.broadcast_to(jnp.sum(a, axis=1, keepdims=True), (sel_rows, T)) for a in accs)

    thr, _ = _select_topk(count_fn, tie_ref, (sel_rows, T), topk, idx_bits, 4)
    thr_ref[...] = thr

    lane_s = lax.broadcasted_iota(i32, (n_tok, T), 1)
    fresh_s = (lane_s <= lax.broadcasted_iota(i32, (n_tok, T), 0)) & (lane_s < n_tok)
    rows_kv = n_rows // n_kv

    def head_rows(ref, kvh):
        n_keys = ref.shape[0] // n_kv
        rows = ref[pl.ds(kvh, n_keys, stride=n_kv), :]
        if n_keys < T:
            rows = jnp.concatenate([rows, jnp.zeros((T - n_keys, T), f32)], axis=0)
        return rows.astype(bf16)

    for s in range(n_seq):
        q = q_ref[s]
        toks = slice(s * n_tok, (s + 1) * n_tok)
        thr_s, tie_s = thr_ref[toks, :], tie_ref[toks, :]
        mx = jnp.full((n_rows, T), NEG_BIG, f32)
        for p in range(nt):
            sc = sc_ref[p, toks, :]
            sel = (sc > thr_s) | ((sc == thr_s) & (p * T + lane_s < tie_s))
            if p == n_pages:
                sel = sel & fresh_s
            bias = jnp.concatenate([jnp.where(sel, 0.0, NEG_BIG)] * (rows_kv // n_tok), axis=0)
            k_ref = k_pages[s][p] if p < n_pages else kn_ref.at[s]
            sm = jnp.concatenate(
                [_dot_nt(q[kvh * rows_kv:(kvh + 1) * rows_kv], head_rows(k_ref, kvh)) + bias
                 for kvh in range(n_kv)], axis=0)
            s_ref[s, p] = sm
            mx = jnp.maximum(mx, sm)
        m = jnp.max(mx, axis=1, keepdims=True)
        l = jnp.zeros((n_rows, 1), f32)
        accs = [jnp.zeros((rows_kv, T), f32) for _ in range(n_kv)]
        for p in range(nt):
            e = jnp.exp(s_ref[s, p] - m)
            l = l + jnp.sum(e, axis=1, keepdims=True)
            eb = e.astype(bf16)
            v_ref = v_pages[s][p] if p < n_pages else vn_ref.at[s]
            for kvh in range(n_kv):
                accs[kvh] = accs[kvh] + _dot(eb[kvh * rows_kv:(kvh + 1) * rows_kv], head_rows(v_ref, kvh))
        o_ref[s] = (jnp.concatenate(accs, axis=0) / l).astype(bf16)


def _dsa_sample(page_table, layer, q, iq, iwb, ikn, kn, vn, cache_idx_t, cache_k, cache_v,
                *, topk, n_tok, n_kv):
    nb, n_rows, T = q.shape
    n_pages = page_table.shape[1]
    kv_rows = cache_k.shape[2]
    idw = cache_idx_t.shape[2]
    ns = SEQ_PER_STEP if nb % SEQ_PER_STEP == 0 else 1
    seq = lambda shape: pl.BlockSpec((ns,) + shape, lambda b, pt: (b, 0, 0))

    def pages(shape):
        return [pl.BlockSpec((None, None) + shape, lambda b, pt, s=s, p=p: (layer, pt[ns * b + s, p], 0, 0))
                for s in range(ns) for p in range(n_pages)]

    in_specs = [seq((n_rows, T)), seq((n_rows, idw)), seq((n_rows, T)), seq((idw, T)),
                seq(kn.shape[1:]), seq(vn.shape[1:])]
    in_specs += pages((idw, T)) + pages((kv_rows, T)) + pages((kv_rows, T))
    nt = n_pages + 1
    grid_spec = pltpu.PrefetchScalarGridSpec(
        num_scalar_prefetch=1,
        grid=(nb // ns,),
        in_specs=in_specs,
        out_specs=seq((n_rows, T)),
        scratch_shapes=[pltpu.VMEM((nt, ns * n_tok, T), f32), pltpu.VMEM((ns, nt, n_rows, T), f32),
                        pltpu.VMEM((ns * n_tok, T), i32), pltpu.VMEM((ns * n_tok, T), f32)],
    )
    return pl.pallas_call(
        functools.partial(_dsa_sample_body, n_seq=ns, n_pages=n_pages, n_kv=n_kv, topk=topk, n_tok=n_tok,
                          idx_bits=int(nt * T).bit_length()),
        grid_spec=grid_spec,
        out_shape=jax.ShapeDtypeStruct((nb, n_rows, T), bf16),
        compiler_params=_params("parallel"),
        name="dsa_sample",
    )(page_table, q, iq, iwb, ikn, kn, vn,
      *([cache_idx_t] * (ns * n_pages)), *([cache_k] * (ns * n_pages)), *([cache_v] * (ns * n_pages)))


def _rope_tables(pos, dim):
    half = dim // 2
    inv = ROPE_THETA ** (-jnp.arange(half, dtype=f32) * (2.0 / dim))
    ang = pos.astype(f32)[:, None] * inv[None, :]
    cos = jnp.concatenate([jnp.cos(ang), jnp.cos(ang)], axis=1)
    sin = jnp.concatenate([-jnp.sin(ang), jnp.sin(ang)], axis=1)
    reps = LANES // dim
    return jnp.tile(cos, (1, reps)), jnp.tile(sin, (1, reps))


def _mix_tables(w_s, b_s, c, group_dim):
    causal = jnp.tril(jnp.ones((c, c), dtype=bool))
    ws = jnp.where(causal[None], w_s[:, :c, :c], 0)
    reps = LANES // c
    ws = jnp.tile(ws, (1, reps, reps))
    blk = jnp.arange(LANES) // c
    ws = jnp.where((blk[:, None] == blk[None, :])[None], ws, 0).astype(bf16)
    bias = jnp.tile(jnp.repeat(b_s[:, :c].T, group_dim, axis=1), (reps, 1)).astype(f32)
    return ws, bias


def kernel(x_prompt, x_sample, cache_k, cache_v, cache_idx_k, state_ffn_conv, page_table, a_norm, a_w_in, a_v_norm, a_w_s, a_b_s, a_w_out, b_norm, b_w_in, b_q_norm, b_k_norm, b_w_o, f_norm, f_w_in, f_conv_w, f_conv_b, f_w_out):
    bsz, seq, d = x_prompt.shape
    nb, n_tok, _ = x_sample.shape
    depth = f_norm.shape[0]
    n_layers_b, n_pool, page, n_kv, hd = cache_k.shape
    idw = cache_idx_k.shape[3]
    past = page_table.shape[1] * page
    n_heads = b_w_o.shape[1] // hd
    group = n_heads // n_kv
    n_ih = (b_w_in.shape[2] - (n_heads + 2 * n_kv) * hd - idw) // (idw + 1)
    ff = f_w_out.shape[1]
    a_groups, chunk = a_w_s.shape[1], a_w_s.shape[2]
    a_width = a_w_in.shape[2] // 2
    assert hd == LANES and idw == 64 and page == LANES and chunk == LANES and n_ih % 2 == 0
    assert f_conv_w.shape[1] == 3 and n_tok >= 2 and LANES % n_tok == 0 and n_tok & (n_tok - 1) == 0

    mp, ms = bsz * seq, nb * n_tok
    xp = x_prompt.reshape(mp, d)
    xs = x_sample.reshape(ms, d)
    tm_p = min(TM_ROWS, seq)
    tm_s = min(TM_ROWS, ms)
    tq = min(TQ_PROMPT, seq)

    aw_in, aw_out = a_w_in.astype(bf16), a_w_out.astype(bf16)
    n_real = b_w_in.shape[2]
    n_pad = -(-n_real // LANES) * LANES
    bw_in = jnp.pad(b_w_in, ((0, 0), (0, 0), (0, n_pad - n_real))).astype(bf16)
    bw_o = b_w_o.astype(bf16)
    tn_b = LANES * 11 if n_pad % (LANES * 11) == 0 else LANES
    tn_f = 512 if ff % 512 == 0 else LANES
    fw_in, fw_out = f_w_in.astype(bf16), f_w_out.astype(bf16)

    ck = cache_k.reshape(n_layers_b, n_pool, page * n_kv, hd)
    cv = cache_v.reshape(n_layers_b, n_pool, page * n_kv, hd)
    cidx_t = jnp.swapaxes(cache_idx_k, 2, 3)
    outs = dict(k_p=[], v_p=[], ik_p=[], k_s=[], v_s=[], ik_s=[], chunk_v=[], conv_p=[], conv_s=[])
    n_mixers = 2
    for layer in range(depth):
        j = layer // n_mixers
        if layer % n_mixers == 0:
            new = []
            for x, tm, c, want in ((xp, tm_p, min(seq, chunk), False), (xs, tm_s, min(n_tok, chunk), True)):
                wmix, bias = _mix_tables(a_w_s[j], a_b_s[j], c, a_width // a_groups)
                z = _norm_matmul(x, a_norm, aw_in, j, act="gelu", tm=min(TM_FFN, x.shape[0]),
                                 tn=1024 if a_width % 1024 == 0 else 512)
                gated, vn = _spatial_gate(z, a_v_norm, j, wmix, bias, tm=min(tm, 512), want_vn=want)
                new.append(_matmul_residual(gated, aw_out, j, x, tm=tm, tn=d))
                if want:
                    outs["chunk_v"].append(vn.reshape(nb, n_tok, a_width))
            xp, xs = new
        else:
            pos_p = jnp.arange(seq, dtype=i32)
            pos_s = jnp.tile(past + jnp.arange(n_tok, dtype=i32), nb)
            res = []
            for x, tm, pos in ((xp, tm_p, pos_p), (xs, tm_s, pos_s)):
                p = _norm_matmul(x, b_norm, bw_in, j, tm=min(TM_FFN, x.shape[0]), tn=tn_b)
                tabs = _rope_tables(pos, hd) + _rope_tables(pos, idw)
                res.append(_attn_post(p, b_q_norm, b_k_norm, j, tabs, tm=tm,
                                      n_heads=n_heads, n_kv=n_kv, n_ih=n_ih))
            q, k, v, kb, vb, iq, ikw, ik2 = res[0]
            r3 = lambda a: a.reshape(bsz, seq, a.shape[1])
            iwt = jnp.swapaxes(ikw[:, idw:idw + n_ih].reshape(bsz, seq, n_ih), 1, 2)
            bound = 1.01 * math.sqrt(hd) * jnp.max(jnp.abs(b_q_norm[j])) * jnp.max(jnp.abs(b_k_norm[j]))
            bound = jnp.full((1, LANES), bound, f32)
            o = _dsa_prompt(r3(q), r3(iq), iwt, r3(ik2), r3(kb), r3(vb), bound,
                            topk=min(TOPK_MAX, seq // 4), n_kv=n_kv, n_ih=n_ih, tq=tq)
            xp = _matmul_residual(o.reshape(mp, n_heads * hd), bw_o, j, xp, tm=tm_p, tn=d)
            n_seq_pages = seq // page
            outs["k_p"].append(k.reshape(bsz, n_seq_pages, page, n_kv, hd))
            outs["v_p"].append(v.reshape(bsz, n_seq_pages, page, n_kv, hd))
            outs["ik_p"].append(ikw[:, :idw].reshape(bsz, n_seq_pages, page, idw))

            q, k, v, kb, vb, iq, ikw, ik2 = res[1]
            hm = lambda a, w: a.reshape(nb, n_tok, a.shape[1] // w, w).transpose(0, 2, 1, 3).reshape(nb, -1, w)
            qs = hm(q, hd)
            iqs = hm(iq, idw)
            iw = ikw[:, idw:idw + n_ih].reshape(nb, n_tok, n_ih).transpose(0, 2, 1).reshape(nb, n_ih * n_tok, 1)
            iwb = jnp.broadcast_to(iw, (nb, n_ih * n_tok, LANES))
            pad_tok = lambda a: jnp.pad(a.reshape(nb, n_tok, -1), ((0, 0), (0, page - n_tok), (0, 0)))
            ikn = jnp.swapaxes(pad_tok(ikw[:, :idw]), 1, 2)
            kn = k.reshape(nb, n_tok * n_kv, hd)
            vn_ = v.reshape(nb, n_tok * n_kv, hd)
            os_ = _dsa_sample(page_table, j, qs, iqs, iwb, ikn, kn, vn_, cidx_t, ck, cv,
                              topk=min(TOPK_MAX, (past + n_tok) // 4), n_tok=n_tok, n_kv=n_kv)
            os_ = os_.reshape(nb, n_heads, n_tok, hd).transpose(0, 2, 1, 3).reshape(ms, n_heads * hd)
            xs = _matmul_residual(os_, bw_o, j, xs, tm=tm_s, tn=d)
            outs["k_s"].append(k.reshape(nb, n_tok, n_kv, hd))
            outs["v_s"].append(v.reshape(nb, n_tok, n_kv, hd))
            outs["ik_s"].append(ikw[:, :idw].reshape(nb, n_tok, idw))

        tm_f = min(TM_FFN, seq)
        xp, tg, tu = _conv_ffn(xp, f_norm, fw_in, f_conv_w, f_conv_b, fw_out, layer,
                               tm=tm_f, tn=tn_f, seq_len=seq)
        tails = jnp.concatenate([tg, tu], axis=2).reshape(bsz, seq // tm_f, 8, 2 * ff)
        outs["conv_p"].append(tails[:, -1, 6:8, :])
        st = state_ffn_conv[layer]
        st = jnp.pad(st, ((0, 0), (0, n_tok - 2), (0, 0))).reshape(ms, 2 * ff)
        xs, ng, nu = _conv_ffn(xs, f_norm, fw_in, f_conv_w, f_conv_b, fw_out, layer,
                               tm=min(TM_FFN, ms), tn=tn_f, seq_len=n_tok, state=st)
        outs["conv_s"].append(jnp.concatenate([ng, nu], axis=2))

    st = lambda key: jnp.stack(outs[key])
    return (xp.reshape(bsz, seq, d), xs.reshape(nb, n_tok, d), st("k_p"), st("v_p"), st("ik_p"),
            st("k_s"), st("v_s"), st("ik_s"), st("chunk_v"), st("conv_p"), st("conv_s"))
```

```python
import functools
import math

import jax
import jax.numpy as jnp
from jax import lax
from jax.experimental import pallas as pl
from jax.experimental.pallas import tpu as pltpu

EPS = 1e-6
ROPE_THETA = 10000.0
TOPK_MAX = 256
LANES = 128
NEG_BIG = -1e30
INT_MIN = -(2 ** 31)
KEY_NEG_INF = -2139095041
TIE_ALL = 2 ** 30
SAFE_SHIFT_BOUND = 40.0
VMEM_LIMIT = 56 * 1024 * 1024
TM_ROWS = 512
TM_FFN = 1024
TQ_PROMPT = 256
SEQ_PER_STEP = 2
COUNT_CHAINS = 2
EXIT_CHECK_BITS = (24, 28)

f32 = jnp.float32
bf16 = jnp.bfloat16
i32 = jnp.int32


def _params(*sem):
    return pltpu.CompilerParams(dimension_semantics=sem, vmem_limit_bytes=VMEM_LIMIT)


def _dot(a, b):
    return jnp.dot(a, b, preferred_element_type=f32)


def _dot_nt(a, b):
    return lax.dot_general(a, b, (((1,), (1,)), ((), ())), preferred_element_type=f32)


def _rms(x, g):
    r = lax.rsqrt(jnp.mean(x * x, axis=-1, keepdims=True) + EPS)
    return x * r * g


def _rows(a):
    return a.reshape(a.shape[0], 1, a.shape[1])


def _gelu_tanh(x):
    c = math.sqrt(2.0 / math.pi)
    return 0.5 * x * (1.0 + jnp.tanh(c * (x + 0.044715 * (x * x * x))))


def _nmm_body(x_ref, g_ref, w_ref, o_ref, h_ref, *, act):
    @pl.when(pl.program_id(1) == 0)
    def _():
        h_ref[...] = _rms(x_ref[...], g_ref[...]).astype(bf16)

    z = _dot(h_ref[...], w_ref[...])
    if act == "gelu":
        z = _gelu_tanh(z)
    o_ref[...] = z.astype(o_ref.dtype)


def _norm_matmul(x, gain, w, layer, *, act=None, tm, tn, out_dtype=f32):
    m, k = x.shape
    n = w.shape[2]
    return pl.pallas_call(
        functools.partial(_nmm_body, act=act),
        grid=(m // tm, n // tn),
        in_specs=[
            pl.BlockSpec((tm, k), lambda i, j: (i, 0)),
            pl.BlockSpec((None, 1, k), lambda i, j: (layer, 0, 0)),
            pl.BlockSpec((None, k, tn), lambda i, j: (layer, 0, j)),
        ],
        out_specs=pl.BlockSpec((tm, tn), lambda i, j: (i, j)),
        out_shape=jax.ShapeDtypeStruct((m, n), out_dtype),
        scratch_shapes=[pltpu.VMEM((tm, k), bf16)],
        compiler_params=_params("parallel", "arbitrary"),
        name="norm_matmul",
    )(x, _rows(gain), w)


def _mmres_body(a_ref, w_ref, x_ref, o_ref):
    o_ref[...] = x_ref[...] + _dot(a_ref[...], w_ref[...])


def _matmul_residual(a, w, layer, x, *, tm, tn):
    m, k = a.shape
    n = w.shape[2]
    return pl.pallas_call(
        _mmres_body,
        grid=(m // tm, n // tn),
        in_specs=[
            pl.BlockSpec((tm, k), lambda i, j: (i, 0)),
            pl.BlockSpec((None, k, tn), lambda i, j: (layer, 0, j)),
            pl.BlockSpec((tm, tn), lambda i, j: (i, j)),
        ],
        out_specs=pl.BlockSpec((tm, tn), lambda i, j: (i, j)),
        out_shape=jax.ShapeDtypeStruct((m, n), f32),
        compiler_params=_params("parallel", "arbitrary"),
        name="matmul_residual",
    )(a, w, x)


def _gate_body(u_ref, v_ref, vg_ref, wmix_ref, bias_ref, o_ref, *vn_out, tm, groups):
    vn = _rms(v_ref[...], vg_ref[...])
    if vn_out:
        vn_out[0][...] = vn
    vnb = vn.astype(bf16)
    gd = v_ref.shape[1] // groups
    for c in range(tm // LANES):
        rows = slice(c * LANES, (c + 1) * LANES)
        for g in range(groups):
            cols = slice(g * gd, (g + 1) * gd)
            s = _dot(wmix_ref[g], vnb[rows, cols]) + bias_ref[:, cols]
            o_ref[rows, cols] = (u_ref[rows, cols] * s).astype(bf16)


def _spatial_gate(z, v_gain, layer, wmix, bias, *, tm, want_vn):
    m, w2 = z.shape
    w = w2 // 2
    groups = wmix.shape[0]
    out_shape = [jax.ShapeDtypeStruct((m, w), bf16)]
    out_specs = [pl.BlockSpec((tm, w), lambda i: (i, 0))]
    if want_vn:
        out_shape.append(jax.ShapeDtypeStruct((m, w), f32))
        out_specs.append(pl.BlockSpec((tm, w), lambda i: (i, 0)))
    res = pl.pallas_call(
        functools.partial(_gate_body, tm=tm, groups=groups),
        grid=(m // tm,),
        in_specs=[
            pl.BlockSpec((tm, w), lambda i: (i, 0)),
            pl.BlockSpec((tm, w), lambda i: (i, 1)),
            pl.BlockSpec((None, 1, w), lambda i: (layer, 0, 0)),
            pl.BlockSpec(wmix.shape, lambda i: (0, 0, 0)),
            pl.BlockSpec(bias.shape, lambda i: (0, 0)),
        ],
        out_specs=out_specs,
        out_shape=out_shape,
        compiler_params=_params("parallel"),
        name="spatial_gate",
    )(z, z, _rows(v_gain), wmix, bias)
    return res if want_vn else (res[0], None)


def _conv3(a, cw, cb, prev1, prev2):
    return cb + cw[0:1] * prev2 + cw[1:2] * prev1 + cw[2:3] * a


def _ffn_body(x_ref, g_ref, wg_ref, wu_ref, cwg_ref, cwu_ref, cbg_ref, cbu_ref, wo_ref, *rest,
              sample, seq_tiles, seq_len, nj):
    if sample:
        sg_ref, su_ref, o_ref, ag_ref, au_ref, h_ref, act0_ref, act1_ref = rest
    else:
        o_ref, tg_ref, tu_ref, h_ref, act0_ref, act1_ref, carry_ref = rest
    acts = (act0_ref, act1_ref)
    i = pl.program_id(0)
    j = pl.program_id(1)

    @pl.when(j == 0)
    def _():
        x = x_ref[...]
        h_ref[...] = _rms(x, g_ref[...]).astype(bf16)
        o_ref[...] = x
        act1_ref[...] = jnp.zeros(act1_ref.shape, bf16)

    def down(src_ref):
        o_ref[...] += _dot(src_ref[...], wo_ref[...])

    def up_conv_gate(dst_ref):
        h = h_ref[...]
        ag = _dot(h, wg_ref[...])
        au = _dot(h, wu_ref[...])
        tm, tn = ag.shape
        row = lax.broadcasted_iota(i32, (tm, tn), 0)
        if sample:
            t = row & (seq_len - 1)
            m1 = t == 0
            m2 = t < 2
            p2g, p2u = sg_ref[...], su_ref[...]
            p1g, p1u = pltpu.roll(p2g, tm - 1, 0), pltpu.roll(p2u, tm - 1, 0)
            for a, a_ref in ((ag, ag_ref), (au, au_ref)):
                last = pltpu.roll(a, tm - (seq_len - 2), 0).reshape(tm // seq_len, seq_len, tn)
                a_ref[...] = last[:, :2, :]
        else:
            m1 = row == 0
            m2 = row < 2
            live = (i % seq_tiles) != 0
            cg = jnp.where(live, carry_ref[j, 0], 0.0)
            cu = jnp.where(live, carry_ref[j, 1], 0.0)
            p1g, p1u = cg[7:8], cu[7:8]
            p2g = jnp.where(m1, cg[6:7], cg[7:8])
            p2u = jnp.where(m1, cu[6:7], cu[7:8])
            carry_ref[j, 0] = ag[tm - 8:]
            carry_ref[j, 1] = au[tm - 8:]
            tg_ref[0] = ag[tm - 8:]
            tu_ref[0] = au[tm - 8:]
        a1g = jnp.where(m1, p1g, pltpu.roll(ag, 1, 0))
        a2g = jnp.where(m2, p2g, pltpu.roll(ag, 2, 0))
        a1u = jnp.where(m1, p1u, pltpu.roll(au, 1, 0))
        a2u = jnp.where(m2, p2u, pltpu.roll(au, 2, 0))
        cg_ = _conv3(ag, cwg_ref[...], cbg_ref[...], a1g, a2g)
        cu_ = _conv3(au, cwu_ref[...], cbu_ref[...], a1u, a2u)
        dst_ref[...] = (cg_ * (1.0 / (1.0 + jnp.exp(-cg_))) * cu_).astype(bf16)

    for parity in (0, 1):
        @pl.when((j < nj) & ((j & 1) == parity))
        def _(parity=parity):
            down(acts[1 - parity])
            up_conv_gate(acts[parity])

    @pl.when(j == nj)
    def _():
        down(acts[(nj - 1) % 2])


def _conv_ffn(x, gain, w_in, conv_w, conv_b, w_out, layer, *, tm, tn, seq_len, state=None):
    m, d = x.shape
    ff = w_out.shape[1]
    nj = ff // tn
    ni = m // tm
    sample = state is not None
    up = lambda j: jnp.minimum(j, nj - 1)
    down = lambda j: jnp.maximum(j - 1, 0)
    in_specs = [
        pl.BlockSpec((tm, d), lambda i, j: (i, 0), pipeline_mode=pl.Buffered(1)),
        pl.BlockSpec((None, 1, d), lambda i, j: (layer, 0, 0)),
        pl.BlockSpec((None, d, tn), lambda i, j: (layer, 0, up(j))),
        pl.BlockSpec((None, d, tn), lambda i, j: (layer, 0, nj + up(j))),
        pl.BlockSpec((None, 3, tn), lambda i, j: (layer, 0, up(j))),
        pl.BlockSpec((None, 3, tn), lambda i, j: (layer, 0, nj + up(j))),
        pl.BlockSpec((None, 1, tn), lambda i, j: (layer, 0, up(j))),
        pl.BlockSpec((None, 1, tn), lambda i, j: (layer, 0, nj + up(j))),
        pl.BlockSpec((None, tn, d), lambda i, j: (layer, down(j), 0)),
    ]
    conv_b = _rows(conv_b)
    args = [x, _rows(gain), w_in, w_in, conv_w, conv_w, conv_b, conv_b, w_out]
    out_specs = [pl.BlockSpec((tm, d), lambda i, j: (i, 0), pipeline_mode=pl.Buffered(1 if ni == 1 else 2))]
    out_shape = [jax.ShapeDtypeStruct((m, d), f32)]
    scratch = [pltpu.VMEM((tm, d), bf16), pltpu.VMEM((tm, tn), bf16), pltpu.VMEM((tm, tn), bf16)]
    if sample:
        in_specs += [
            pl.BlockSpec((tm, tn), lambda i, j: (i, up(j))),
            pl.BlockSpec((tm, tn), lambda i, j: (i, nj + up(j))),
        ]
        args += [state, state]
        out_specs += [pl.BlockSpec((tm // seq_len, 2, tn), lambda i, j: (i, 0, up(j)))] * 2
        out_shape += [jax.ShapeDtypeStruct((m // seq_len, 2, ff), f32)] * 2
        seq_tiles = 1
    else:
        out_specs += [pl.BlockSpec((1, 8, tn), lambda i, j: (i, 0, up(j)))] * 2
        out_shape += [jax.ShapeDtypeStruct((ni, 8, ff), f32)] * 2
        scratch.append(pltpu.VMEM((nj, 2, 8, tn), f32))
        seq_tiles = seq_len // tm
    return pl.pallas_call(
        functools.partial(_ffn_body, sample=sample, seq_tiles=seq_tiles, seq_len=seq_len, nj=nj),
        grid=(ni, nj + 1),
        in_specs=in_specs,
        out_specs=out_specs,
        out_shape=out_shape,
        scratch_shapes=scratch,
        compiler_params=_params("arbitrary", "arbitrary"),
        name="conv_ffn_sample" if sample else "conv_ffn_prompt",
    )(*args)


def _rope_full(x, c, s):
    return x * c + pltpu.roll(x, LANES // 2, 1) * s


def _rope_half(x, c, s, lo):
    partner = jnp.where(lo, pltpu.roll(x, 96, 1), pltpu.roll(x, 32, 1))
    return x * c + partner * s


def _post_body(p_ref, qg_ref, kg_ref, c1_ref, s1_ref, c2_ref, s2_ref,
               q_ref, k_ref, v_ref, kb_ref, vb_ref, iq_ref, ikw_ref, ik2_ref,
               *, n_heads, n_kv, n_ih, idx_scale, q_scale):
    hd = LANES
    c1, s1, c2, s2 = c1_ref[...], s1_ref[...], c2_ref[...], s2_ref[...]
    tm = p_ref.shape[0]
    lane = lax.broadcasted_iota(i32, (tm, LANES), 1)
    lo = (lane & 63) < 32
    off = 0
    for h in range(n_heads):
        x = p_ref[:, off + h * hd: off + (h + 1) * hd]
        q_ref[:, h * hd:(h + 1) * hd] = (_rope_full(_rms(x, qg_ref[...]), c1, s1) * q_scale).astype(bf16)
    off += n_heads * hd
    for h in range(n_kv):
        x = p_ref[:, off + h * hd: off + (h + 1) * hd]
        kr = _rope_full(_rms(x, kg_ref[...]), c1, s1)
        k_ref[pl.ds(h, tm, stride=n_kv), :] = kr
        kb_ref[:, h * hd:(h + 1) * hd] = kr.astype(bf16)
    off += n_kv * hd
    vv = p_ref[:, off: off + n_kv * hd]
    for h in range(n_kv):
        v_ref[pl.ds(h, tm, stride=n_kv), :] = vv[:, h * hd:(h + 1) * hd]
    vb_ref[...] = vv.astype(bf16)
    off += n_kv * hd
    for h in range(n_ih // 2):
        x = p_ref[:, off + h * LANES: off + (h + 1) * LANES]
        iq_ref[:, h * LANES:(h + 1) * LANES] = _rope_half(x, c2, s2, lo).astype(bf16)
    off += (n_ih // 2) * LANES
    tail = p_ref[:, off: off + LANES]
    tr = _rope_half(tail, c2, s2, lo)
    ik = jnp.where(lane < 64, tr, 0.0)
    ik2_ref[...] = (ik + pltpu.roll(ik, 64, 1)).astype(bf16)
    ikw_ref[...] = jnp.where(lane < 64, tr, tail * idx_scale)


def _attn_post(p, q_gain, k_gain, layer, tabs, *, tm, n_heads, n_kv, n_ih):
    m, npad = p.shape
    c1, s1, c2, s2 = tabs
    nt = c1.shape[0] // tm
    hd = LANES
    tab_spec = pl.BlockSpec((tm, LANES), lambda i: (i % nt, 0))
    row = lambda w: pl.BlockSpec((tm, w), lambda i: (i, 0))
    gain_spec = pl.BlockSpec((None, 1, hd), lambda i: (layer, 0, 0))
    outs = [
        (1, n_heads * hd, bf16), (n_kv, hd, f32), (n_kv, hd, f32), (1, n_kv * hd, bf16), (1, n_kv * hd, bf16),
        (1, n_ih * 64, bf16), (1, LANES, f32), (1, LANES, bf16),
    ]
    return pl.pallas_call(
        functools.partial(_post_body, n_heads=n_heads, n_kv=n_kv, n_ih=n_ih,
                          idx_scale=float((n_ih * 64) ** -0.5), q_scale=float(hd ** -0.5)),
        grid=(m // tm,),
        in_specs=[row(npad), gain_spec, gain_spec, tab_spec, tab_spec, tab_spec, tab_spec],
        out_specs=[pl.BlockSpec((tm * r, w), lambda i: (i, 0)) for r, w, _ in outs],
        out_shape=[jax.ShapeDtypeStruct((m * r, w), dt) for r, w, dt in outs],
        compiler_params=_params("parallel"),
        name="attn_post",
    )(p, _rows(q_gain), _rows(k_gain), c1, s1, c2, s2)


def _key_to_float(key):
    bits = key ^ ((key >> 31) & 0x7FFFFFFF)
    return lax.bitcast_convert_type(bits, f32)


def _select_threshold(count_fn, res_ref, shape, topk, bits_per_round):
    kk = float(topk)
    n_cand = (1 << bits_per_round) - 1

    def round_body(it, carry):
        res, cnt_res = carry
        step = lax.shift_left(jnp.int32(1), 32 - bits_per_round * (it + 1))
        cands = [res + step * (c + 1) for c in range(n_cand)]
        floats = [_key_to_float(c) for c in cands]
        cnts = count_fn(tuple((lambda sc, idx, cf=cf: sc >= cf) for cf in floats))
        for c, n in zip(cands, cnts):
            res = jnp.where(n >= kk, c, res)
            cnt_res = jnp.where(n >= kk, n, cnt_res)
        return res, cnt_res

    stops = [-(-b // bits_per_round) for b in EXIT_CHECK_BITS] + [32 // bits_per_round]
    res, cnt = lax.fori_loop(0, stops[0], round_body, (jnp.full(shape, INT_MIN, i32), jnp.zeros(shape, f32)))
    res_ref[...] = res

    def continue_from(k, res, cnt):
        unsettled = jnp.where((cnt != kk) & (res != INT_MIN), 1.0, 0.0)

        @pl.when(jnp.max(unsettled) > 0.0)
        def _():
            r, c = lax.fori_loop(stops[k], stops[k + 1], round_body, (res, cnt))
            res_ref[...] = r
            if k + 2 < len(stops):
                continue_from(k + 1, r, c)

    continue_from(0, res, cnt)
    return _key_to_float(jnp.maximum(res_ref[...], KEY_NEG_INF))


def _tie_search(count_fn, thr, need, shape, idx_bits):
    def tie_body(it, end):
        cand = end + lax.shift_left(jnp.int32(1), idx_bits - 1 - it)
        (cnt,) = count_fn((lambda sc, idx: (sc == thr) & (idx < cand),))
        return jnp.where(cnt <= need, cand, end)

    return lax.fori_loop(0, idx_bits, tie_body, jnp.zeros(shape, i32))


def _select_topk(count_fn, tie_ref, shape, topk, idx_bits, bits_per_round):
    kk = float(topk)
    thr = _select_threshold(count_fn, tie_ref, shape, topk, bits_per_round)
    n_gt, n_ge = count_fn((lambda sc, idx: sc > thr, lambda sc, idx: sc >= thr))
    tie_ref[...] = jnp.full(shape, TIE_ALL, i32)
    crowded = jnp.where((n_ge > kk) & (thr > -jnp.inf), 1.0, 0.0)

    @pl.when(jnp.max(crowded) > 0.0)
    def _():
        tie_ref[...] = _tie_search(count_fn, thr, kk - n_gt, shape, idx_bits)

    return thr, tie_ref[...]


def _dsa_prompt_body(q_ref, iq_ref, iwt_ref, ik2_ref, kb_ref, vb_ref, bound_ref, o_ref,
                     sc_ref, iqm_ref, qg_ref, tie_ref, m_ref, acc_ref,
                     *, topk, n_kv, group, n_ih, idx_bits, tq):
    T = LANES
    nch = tq // T
    strips = tq // 8
    qb = pl.program_id(1)
    nkt = qb + 1
    lane = lax.broadcasted_iota(i32, (tq, T), 1)

    for h in range(n_ih):
        pair = iq_ref[0, :, (h // 2) * T:(h // 2 + 1) * T].astype(f32)
        keep = (lane < 64) if h % 2 == 0 else (lane >= 64)
        iqm_ref[h] = jnp.where(keep, pair, 0.0).astype(bf16)
    for h in range(n_kv * group):
        qg_ref[h // group, (h % group) * tq:(h % group + 1) * tq, :] = q_ref[0, :, h * T:(h + 1) * T]

    def key_rows(kt):
        return pl.ds(pl.multiple_of(kt * tq, tq), tq)

    key_in_tile = (lax.broadcasted_iota(i32, (strips, 8, tq), 0) * 8
                   + lax.broadcasted_iota(i32, (strips, 8, tq), 1))
    qidx = qb * tq + lax.broadcasted_iota(i32, (strips, 8, tq), 2)

    def score_tile(kt, carry):
        ik_t = ik2_ref[0, key_rows(kt), :]
        acc = jnp.zeros((tq, tq), f32)
        for h in range(n_ih):
            acc = acc + jnp.maximum(_dot_nt(ik_t, iqm_ref[h]), 0.0) * iwt_ref[0, h:h + 1, :]
        admissible = kt * tq + key_in_tile <= qidx
        sc_ref[kt] = jnp.where(admissible, acc.reshape(strips, 8, tq), -jnp.inf).reshape(tq, tq)
        return carry

    lax.fori_loop(0, nkt, score_tile, 0)

    def count_fn(preds):
        def body(kt, accs):
            sc = sc_ref[kt].reshape(strips, 8, tq)
            kidx = kt * tq + key_in_tile
            return tuple(a + jnp.sum(jnp.where(pred(sc, kidx), 1.0, 0.0).reshape(-1, 8 * COUNT_CHAINS, tq), axis=0)
                         for a, pred in zip(accs, preds))

        accs = lax.fori_loop(0, nkt, body, tuple(jnp.zeros((8 * COUNT_CHAINS, tq), f32) for _ in preds))
        return tuple(jnp.broadcast_to(jnp.sum(a, axis=0, keepdims=True), (8, tq)) for a in accs)

    thr, tie_end = _select_topk(count_fn, tie_ref, (8, tq), topk, idx_bits, 1)

    def bias_tile(kt, carry):
        sc = sc_ref[kt].reshape(strips, 8, tq)
        kidx = kt * tq + key_in_tile
        sel = ((sc > thr) | ((sc == thr) & (kidx < tie_end))) & (kidx <= qidx)
        sc_ref[kt] = jnp.where(sel, 0.0, NEG_BIG).reshape(tq, tq).T
        return carry

    lax.fori_loop(0, nkt, bias_tile, 0)

    gq = group * tq

    def scores(kt, kvh):
        k_t = kb_ref[0, key_rows(kt), kvh * T:(kvh + 1) * T]
        s = _dot_nt(qg_ref[kvh], k_t)
        return (s.reshape(group, tq, tq) + sc_ref[kt][None]).reshape(gq, tq)

    bound = bound_ref[...]
    m_ref[...] = jnp.broadcast_to(bound[None], m_ref.shape)

    @pl.when(jnp.max(bound) > SAFE_SHIFT_BOUND)
    def _():
        m_ref[...] = jnp.full(m_ref.shape, NEG_BIG, f32)

        def max_tile(kt, carry):
            for kvh in range(n_kv):
                s = scores(kt, kvh)
                mx = m_ref[kvh]
                for c in range(nch):
                    mx = jnp.maximum(mx, s[:, c * T:(c + 1) * T])
                m_ref[kvh] = mx
            return carry

        lax.fori_loop(0, nkt, max_tile, 0)
        for kvh in range(n_kv):
            m_ref[kvh] = jnp.broadcast_to(jnp.max(m_ref[kvh], axis=1, keepdims=True), (gq, T))

    acc_ref[...] = jnp.zeros(acc_ref.shape, f32)
    ones = jnp.ones((tq, T), bf16)

    def pv_tile(kt, carry):
        for kvh in range(n_kv):
            s = scores(kt, kvh)
            mx = m_ref[kvh]
            p = jnp.concatenate([jnp.exp(s[:, c * T:(c + 1) * T] - mx) for c in range(nch)], axis=1)
            v1 = jnp.concatenate([vb_ref[0, key_rows(kt), kvh * T:(kvh + 1) * T], ones], axis=1)
            acc_ref[kvh] += _dot(p.astype(bf16), v1)
        return carry

    lax.fori_loop(0, nkt, pv_tile, 0)
    for kvh in range(n_kv):
        o = acc_ref[kvh, :, :T] / acc_ref[kvh, :, T:]
        for g in range(group):
            h = kvh * group + g
            o_ref[0, :, h * T:(h + 1) * T] = o[g * tq:(g + 1) * tq].astype(bf16)


def _dsa_prompt(q, iq, iwt, ik2, kb, vb, bound, *, topk, n_kv, n_ih, tq):
    b, s, qd = q.shape
    T = LANES
    n_heads = qd // T
    group = n_heads // n_kv
    nq = s // tq
    blk = lambda w: pl.BlockSpec((1, tq, w), lambda bi, qi: (bi, qi, 0))
    full = lambda w: pl.BlockSpec((1, s, w), lambda bi, qi: (bi, 0, 0))
    att = pltpu.VMEM((n_kv, group * tq, T), f32)
    return pl.pallas_call(
        functools.partial(_dsa_prompt_body, topk=topk, n_kv=n_kv, group=group, n_ih=n_ih,
                          idx_bits=int(s).bit_length(), tq=tq),
        grid=(b, nq),
        in_specs=[blk(qd), blk(iq.shape[2]), pl.BlockSpec((1, n_ih, tq), lambda bi, qi: (bi, 0, qi)),
                  full(T), full(n_kv * T), full(n_kv * T), pl.BlockSpec((1, T), lambda bi, qi: (0, 0))],
        out_specs=blk(qd),
        out_shape=jax.ShapeDtypeStruct((b, s, qd), bf16),
        scratch_shapes=[
            pltpu.VMEM((nq, tq, tq), f32),
            pltpu.VMEM((n_ih, tq, T), bf16),
            pltpu.VMEM((n_kv, group * tq, T), bf16), pltpu.VMEM((8, tq), i32),
            att, pltpu.VMEM((n_kv, group * tq, 2 * T), f32),
        ],
        compiler_params=_params("parallel", "arbitrary"),
        name="dsa_prompt",
    )(q, iq, iwt, ik2, kb, vb, bound)


def _dsa_sample_body(pt_ref, q_ref, iq_ref, iwb_ref, ikn_ref, kn_ref, vn_ref, *rest,
                     n_seq, n_pages, n_kv, topk, n_tok, idx_bits):
    del pt_ref
    T = LANES
    pages = [rest[i * n_pages:(i + 1) * n_pages] for i in range(3 * n_seq)]
    idx_pages, k_pages, v_pages = pages[:n_seq], pages[n_seq:2 * n_seq], pages[2 * n_seq:]
    o_ref, sc_ref, s_ref, tie_ref, thr_ref = rest[3 * n_seq * n_pages:]
    nt = n_pages + 1
    n_rows = q_ref.shape[1]
    sel_rows = n_seq * n_tok
    lane = lax.broadcasted_iota(i32, (sel_rows, T), 1)
    tok = lax.broadcasted_iota(i32, (sel_rows, T), 0) & (n_tok - 1)
    fresh = (lane <= tok) & (lane < n_tok)

    for s in range(n_seq):
        iq = iq_ref[s]
        iwb = iwb_ref[s]
        for p in range(nt):
            ikt = (idx_pages[s][p][...] if p < n_pages else ikn_ref[s]).astype(bf16)
            w = jnp.maximum(_dot(iq, ikt), 0.0) * iwb
            sc_ref[p, s * n_tok:(s + 1) * n_tok, :] = jnp.sum(w.reshape(n_rows // n_tok, n_tok, T), axis=0)
    sc_ref[n_pages] = jnp.where(fresh, sc_ref[n_pages], -jnp.inf)

    def count_fn(preds):
        accs = [jnp.zeros((sel_rows, T), f32) for _ in preds]
        for p in range(nt):
            sc = sc_ref[p]
            for n, pred in enumerate(preds):
                accs[n] = accs[n] + jnp.where(pred(sc, p * T + lane), 1.0, 0.0)
        return tuple(jnp.broadcast_to(jnp.sum(a, axis=1, keepdims=True), (sel_rows, T)) for a in accs)

    thr, _ = _select_topk(count_fn, tie_ref, (sel_rows, T), topk, idx_bits, 4)
    thr_ref[...] = thr

    lane_s = lax.broadcasted_iota(i32, (n_tok, T), 1)
    fresh_s = (lane_s <= lax.broadcasted_iota(i32, (n_tok, T), 0)) & (lane_s < n_tok)
    rows_kv = n_rows // n_kv

    def head_rows(ref, kvh):
        n_keys = ref.shape[0] // n_kv
        rows = ref[pl.ds(kvh, n_keys, stride=n_kv), :]
        if n_keys < T:
            rows = jnp.concatenate([rows, jnp.zeros((T - n_keys, T), f32)], axis=0)
        return rows.astype(bf16)

    for s in range(n_seq):
        q = q_ref[s]
        toks = slice(s * n_tok, (s + 1) * n_tok)
        thr_s, tie_s = thr_ref[toks, :], tie_ref[toks, :]
        mx = jnp.full((n_rows, T), NEG_BIG, f32)
        for p in range(nt):
            sc = sc_ref[p, toks, :]
            sel = (sc > thr_s) | ((sc == thr_s) & (p * T + lane_s < tie_s))
            if p == n_pages:
                sel = sel & fresh_s
            bias = jnp.concatenate([jnp.where(sel, 0.0, NEG_BIG)] * (rows_kv // n_tok), axis=0)
            k_ref = k_pages[s][p] if p < n_pages else kn_ref.at[s]
            sm = jnp.concatenate(
                [_dot_nt(q[kvh * rows_kv:(kvh + 1) * rows_kv], head_rows(k_ref, kvh)) + bias
                 for kvh in range(n_kv)], axis=0)
            s_ref[s, p] = sm
            mx = jnp.maximum(mx, sm)
        m = jnp.max(mx, axis=1, keepdims=True)
        l = jnp.zeros((n_rows, 1), f32)
        accs = [jnp.zeros((rows_kv, T), f32) for _ in range(n_kv)]
        for p in range(nt):
            e = jnp.exp(s_ref[s, p] - m)
            l = l + jnp.sum(e, axis=1, keepdims=True)
            eb = e.astype(bf16)
            v_ref = v_pages[s][p] if p < n_pages else vn_ref.at[s]
            for kvh in range(n_kv):
                accs[kvh] = accs[kvh] + _dot(eb[kvh * rows_kv:(kvh + 1) * rows_kv], head_rows(v_ref, kvh))
        o_ref[s] = (jnp.concatenate(accs, axis=0) / l).astype(bf16)


def _dsa_sample(page_table, layer, q, iq, iwb, ikn, kn, vn, cache_idx_t, cache_k, cache_v,
                *, topk, n_tok, n_kv):
    nb, n_rows, T = q.shape
    n_pages = page_table.shape[1]
    kv_rows = cache_k.shape[2]
    idw = cache_idx_t.shape[2]
    ns = SEQ_PER_STEP if nb % SEQ_PER_STEP == 0 else 1
    seq = lambda shape: pl.BlockSpec((ns,) + shape, lambda b, pt: (b, 0, 0))

    def pages(shape):
        return [pl.BlockSpec((None, None) + shape, lambda b, pt, s=s, p=p: (layer, pt[ns * b + s, p], 0, 0))
                for s in range(ns) for p in range(n_pages)]

    in_specs = [seq((n_rows, T)), seq((n_rows, idw)), seq((n_rows, T)), seq((idw, T)),
                seq(kn.shape[1:]), seq(vn.shape[1:])]
    in_specs += pages((idw, T)) + pages((kv_rows, T)) + pages((kv_rows, T))
    nt = n_pages + 1
    grid_spec = pltpu.PrefetchScalarGridSpec(
        num_scalar_prefetch=1,
        grid=(nb // ns,),
        in_specs=in_specs,
        out_specs=seq((n_rows, T)),
        scratch_shapes=[pltpu.VMEM((nt, ns * n_tok, T), f32), pltpu.VMEM((ns, nt, n_rows, T), f32),
                        pltpu.VMEM((ns * n_tok, T), i32), pltpu.VMEM((ns * n_tok, T), f32)],
    )
    return pl.pallas_call(
        functools.partial(_dsa_sample_body, n_seq=ns, n_pages=n_pages, n_kv=n_kv, topk=topk, n_tok=n_tok,
                          idx_bits=int(nt * T).bit_length()),
        grid_spec=grid_spec,
        out_shape=jax.ShapeDtypeStruct((nb, n_rows, T), bf16),
        compiler_params=_params("parallel"),
        name="dsa_sample",
    )(page_table, q, iq, iwb, ikn, kn, vn,
      *([cache_idx_t] * (ns * n_pages)), *([cache_k] * (ns * n_pages)), *([cache_v] * (ns * n_pages)))


def _rope_tables(pos, dim):
    half = dim // 2
    inv = ROPE_THETA ** (-jnp.arange(half, dtype=f32) * (2.0 / dim))
    ang = pos.astype(f32)[:, None] * inv[None, :]
    cos = jnp.concatenate([jnp.cos(ang), jnp.cos(ang)], axis=1)
    sin = jnp.concatenate([-jnp.sin(ang), jnp.sin(ang)], axis=1)
    reps = LANES // dim
    return jnp.tile(cos, (1, reps)), jnp.tile(sin, (1, reps))


def _mix_tables(w_s, b_s, c, group_dim):
    causal = jnp.tril(jnp.ones((c, c), dtype=bool))
    ws = jnp.where(causal[None], w_s[:, :c, :c], 0)
    reps = LANES // c
    ws = jnp.tile(ws, (1, reps, reps))
    blk = jnp.arange(LANES) // c
    ws = jnp.where((blk[:, None] == blk[None, :])[None], ws, 0).astype(bf16)
    bias = jnp.tile(jnp.repeat(b_s[:, :c].T, group_dim, axis=1), (reps, 1)).astype(f32)
    return ws, bias


def kernel(x_prompt, x_sample, cache_k, cache_v, cache_idx_k, state_ffn_conv, page_table, a_norm, a_w_in, a_v_norm, a_w_s, a_b_s, a_w_out, b_norm, b_w_in, b_q_norm, b_k_norm, b_w_o, f_norm, f_w_in, f_conv_w, f_conv_b, f_w_out):
    bsz, seq, d = x_prompt.shape
    nb, n_tok, _ = x_sample.shape
    depth = f_norm.shape[0]
    n_layers_b, n_pool, page, n_kv, hd = cache_k.shape
    idw = cache_idx_k.shape[3]
    past = page_table.shape[1] * page
    n_heads = b_w_o.shape[1] // hd
    group = n_heads // n_kv
    n_ih = (b_w_in.shape[2] - (n_heads + 2 * n_kv) * hd - idw) // (idw + 1)
    ff = f_w_out.shape[1]
    a_groups, chunk = a_w_s.shape[1], a_w_s.shape[2]
    a_width = a_w_in.shape[2] // 2
    assert hd == LANES and idw == 64 and page == LANES and chunk == LANES and n_ih % 2 == 0
    assert f_conv_w.shape[1] == 3 and n_tok >= 2 and LANES % n_tok == 0 and n_tok & (n_tok - 1) == 0

    mp, ms = bsz * seq, nb * n_tok
    xp = x_prompt.reshape(mp, d)
    xs = x_sample.reshape(ms, d)
    tm_p = min(TM_ROWS, seq)
    tm_s = min(TM_ROWS, ms)
    tq = min(TQ_PROMPT, seq)

    aw_in, aw_out = a_w_in.astype(bf16), a_w_out.astype(bf16)
    n_real = b_w_in.shape[2]
    n_pad = -(-n_real // LANES) * LANES
    bw_in = jnp.pad(b_w_in, ((0, 0), (0, 0), (0, n_pad - n_real))).astype(bf16)
    bw_o = b_w_o.astype(bf16)
    tn_b = LANES * 11 if n_pad % (LANES * 11) == 0 else LANES
    tn_f = 512 if ff % 512 == 0 else LANES
    fw_in, fw_out = f_w_in.astype(bf16), f_w_out.astype(bf16)

    ck = cache_k.reshape(n_layers_b, n_pool, page * n_kv, hd)
    cv = cache_v.reshape(n_layers_b, n_pool, page * n_kv, hd)
    cidx_t = jnp.swapaxes(cache_idx_k, 2, 3)
    outs = dict(k_p=[], v_p=[], ik_p=[], k_s=[], v_s=[], ik_s=[], chunk_v=[], conv_p=[], conv_s=[])
    n_mixers = 2
    for layer in range(depth):
        j = layer // n_mixers
        if layer % n_mixers == 0:
            new = []
            for x, tm, c, want in ((xp, tm_p, min(seq, chunk), False), (xs, tm_s, min(n_tok, chunk), True)):
                wmix, bias = _mix_tables(a_w_s[j], a_b_s[j], c, a_width // a_groups)
                z = _norm_matmul(x, a_norm, aw_in, j, act="gelu", tm=min(TM_FFN, x.shape[0]),
                                 tn=1024 if a_width % 1024 == 0 else 512)
                gated, vn = _spatial_gate(z, a_v_norm, j, wmix, bias, tm=min(tm, 512), want_vn=want)
                new.append(_matmul_residual(gated, aw_out, j, x, tm=tm, tn=d))
                if want:
                    outs["chunk_v"].append(vn.reshape(nb, n_tok, a_width))
            xp, xs = new
        else:
            pos_p = jnp.arange(seq, dtype=i32)
            pos_s = jnp.tile(past + jnp.arange(n_tok, dtype=i32), nb)
            res = []
            for x, tm, pos in ((xp, tm_p, pos_p), (xs, tm_s, pos_s)):
                p = _norm_matmul(x, b_norm, bw_in, j, tm=min(TM_FFN, x.shape[0]), tn=tn_b)
                tabs = _rope_tables(pos, hd) + _rope_tables(pos, idw)
                res.append(_attn_post(p, b_q_norm, b_k_norm, j, tabs, tm=tm,
                                      n_heads=n_heads, n_kv=n_kv, n_ih=n_ih))
            q, k, v, kb, vb, iq, ikw, ik2 = res[0]
            r3 = lambda a: a.reshape(bsz, seq, a.shape[1])
            iwt = jnp.swapaxes(ikw[:, idw:idw + n_ih].reshape(bsz, seq, n_ih), 1, 2)
            bound = 1.01 * math.sqrt(hd) * jnp.max(jnp.abs(b_q_norm[j])) * jnp.max(jnp.abs(b_k_norm[j]))
            bound = jnp.full((1, LANES), bound, f32)
            o = _dsa_prompt(r3(q), r3(iq), iwt, r3(ik2), r3(kb), r3(vb), bound,
                            topk=min(TOPK_MAX, seq // 4), n_kv=n_kv, n_ih=n_ih, tq=tq)
            xp = _matmul_residual(o.reshape(mp, n_heads * hd), bw_o, j, xp, tm=tm_p, tn=d)
            n_seq_pages = seq // page
            outs["k_p"].append(k.reshape(bsz, n_seq_pages, page, n_kv, hd))
            outs["v_p"].append(v.reshape(bsz, n_seq_pages, page, n_kv, hd))
            outs["ik_p"].append(ikw[:, :idw].reshape(bsz, n_seq_pages, page, idw))

            q, k, v, kb, vb, iq, ikw, ik2 = res[1]
            hm = lambda a, w: a.reshape(nb, n_tok, a.shape[1] // w, w).transpose(0, 2, 1, 3).reshape(nb, -1, w)
            qs = hm(q, hd)
            iqs = hm(iq, idw)
            iw = ikw[:, idw:idw + n_ih].reshape(nb, n_tok, n_ih).transpose(0, 2, 1).reshape(nb, n_ih * n_tok, 1)
            iwb = jnp.broadcast_to(iw, (nb, n_ih * n_tok, LANES))
            pad_tok = lambda a: jnp.pad(a.reshape(nb, n_tok, -1), ((0, 0), (0, page - n_tok), (0, 0)))
            ikn = jnp.swapaxes(pad_tok(ikw[:, :idw]), 1, 2)
            kn = k.reshape(nb, n_tok * n_kv, hd)
            vn_ = v.reshape(nb, n_tok * n_kv, hd)
            os_ = _dsa_sample(page_table, j, qs, iqs, iwb, ikn, kn, vn_, cidx_t, ck, cv,
                              topk=min(TOPK_MAX, (past + n_tok) // 4), n_tok=n_tok, n_kv=n_kv)
            os_ = os_.reshape(nb, n_heads, n_tok, hd).transpose(0, 2, 1, 3).reshape(ms, n_heads * hd)
            xs = _matmul_residual(os_, bw_o, j, xs, tm=tm_s, tn=d)
            outs["k_s"].append(k.reshape(nb, n_tok, n_kv, hd))
            outs["v_s"].append(v.reshape(nb, n_tok, n_kv, hd))
            outs["ik_s"].append(ikw[:, :idw].reshape(nb, n_tok, idw))

        tm_f = min(TM_FFN, seq)
        xp, tg, tu = _conv_ffn(xp, f_norm, fw_in, f_conv_w, f_conv_b, fw_out, layer,
                               tm=tm_f, tn=tn_f, seq_len=seq)
        tails = jnp.concatenate([tg, tu], axis=2).reshape(bsz, seq // tm_f, 8, 2 * ff)
        outs["conv_p"].append(tails[:, -1, 6:8, :])
        st = state_ffn_conv[layer]
        st = jnp.pad(st, ((0, 0), (0, n_tok - 2), (0, 0))).reshape(ms, 2 * ff)
        xs, ng, nu = _conv_ffn(xs, f_norm, fw_in, f_conv_w, f_conv_b, fw_out, layer,
                               tm=min(TM_FFN, ms), tn=tn_f, seq_len=n_tok, state=st)
        outs["conv_s"].append(jnp.concatenate([ng, nu], axis=2))

    st = lambda key: jnp.stack(outs[key])
    return (xp.reshape(bsz, seq, d), xs.reshape(nb, n_tok, d), st("k_p"), st("v_p"), st("ik_p"),
            st("k_s"), st("v_s"), st("ik_s"), st("chunk_v"), st("conv_p"), st("conv_s"))
```

```python
import functools
import math

import jax
import jax.numpy as jnp
from jax import lax
from jax.experimental import pallas as pl
from jax.experimental.pallas import tpu as pltpu

EPS = 1e-6
ROPE_THETA = 10000.0
TOPK_MAX = 256
LANES = 128
NEG_BIG = -1e30
INT_MIN = -(2 ** 31)
KEY_NEG_INF = -2139095041
TIE_ALL = 2 ** 30
SAFE_SHIFT_BOUND = 40.0
VMEM_LIMIT = 56 * 1024 * 1024
TM_ROWS = 512
TM_FFN = 1024
TQ_PROMPT = 256
SEQ_PER_STEP = 2
COUNT_CHAINS = 2
EXIT_CHECK_BITS = (24, 28)

f32 = jnp.float32
bf16 = jnp.bfloat16
i32 = jnp.int32


def _params(*sem):
    return pltpu.CompilerParams(dimension_semantics=sem, vmem_limit_bytes=VMEM_LIMIT)


def _dot(a, b):
    return jnp.dot(a, b, preferred_element_type=f32)


def _dot_nt(a, b):
    return lax.dot_general(a, b, (((1,), (1,)), ((), ())), preferred_element_type=f32)


def _rms(x, g):
    r = lax.rsqrt(jnp.mean(x * x, axis=-1, keepdims=True) + EPS)
    return x * r * g


def _rows(a):
    return a.reshape(a.shape[0], 1, a.shape[1])


def _gelu_tanh(x):
    c = math.sqrt(2.0 / math.pi)
    return 0.5 * x * (1.0 + jnp.tanh(c * (x + 0.044715 * (x * x * x))))


def _nmm_body(x_ref, g_ref, w_ref, o_ref, h_ref, *, act):
    @pl.when(pl.program_id(1) == 0)
    def _():
        h_ref[...] = _rms(x_ref[...], g_ref[...]).astype(bf16)

    z = _dot(h_ref[...], w_ref[...])
    if act == "gelu":
        z = _gelu_tanh(z)
    o_ref[...] = z.astype(o_ref.dtype)


def _norm_matmul(x, gain, w, layer, *, act=None, tm, tn, out_dtype=f32):
    m, k = x.shape
    n = w.shape[2]
    return pl.pallas_call(
        functools.partial(_nmm_body, act=act),
        grid=(m // tm, n // tn),
        in_specs=[
            pl.BlockSpec((tm, k), lambda i, j: (i, 0)),
            pl.BlockSpec((None, 1, k), lambda i, j: (layer, 0, 0)),
            pl.BlockSpec((None, k, tn), lambda i, j: (layer, 0, j)),
        ],
        out_specs=pl.BlockSpec((tm, tn), lambda i, j: (i, j)),
        out_shape=jax.ShapeDtypeStruct((m, n), out_dtype),
        scratch_shapes=[pltpu.VMEM((tm, k), bf16)],
        compiler_params=_params("parallel", "arbitrary"),
        name="norm_matmul",
    )(x, _rows(gain), w)


def _mmres_body(a_ref, w_ref, x_ref, o_ref):
    o_ref[...] = x_ref[...] + _dot(a_ref[...], w_ref[...])


def _matmul_residual(a, w, layer, x, *, tm, tn):
    m, k = a.shape
    n = w.shape[2]
    return pl.pallas_call(
        _mmres_body,
        grid=(m // tm, n // tn),
        in_specs=[
            pl.BlockSpec((tm, k), lambda i, j: (i, 0)),
            pl.BlockSpec((None, k, tn), lambda i, j: (layer, 0, j)),
            pl.BlockSpec((tm, tn), lambda i, j: (i, j)),
        ],
        out_specs=pl.BlockSpec((tm, tn), lambda i, j: (i, j)),
        out_shape=jax.ShapeDtypeStruct((m, n), f32),
        compiler_params=_params("parallel", "arbitrary"),
        name="matmul_residual",
    )(a, w, x)


def _gate_body(u_ref, v_ref, vg_ref, wmix_ref, bias_ref, o_ref, *vn_out, tm, groups):
    vn = _rms(v_ref[...], vg_ref[...])
    if vn_out:
        vn_out[0][...] = vn
    vnb = vn.astype(bf16)
    gd = v_ref.shape[1] // groups
    for c in range(tm // LANES):
        rows = slice(c * LANES, (c + 1) * LANES)
        for g in range(groups):
            cols = slice(g * gd, (g + 1) * gd)
            s = _dot(wmix_ref[g], vnb[rows, cols]) + bias_ref[:, cols]
            o_ref[rows, cols] = (u_ref[rows, cols] * s).astype(bf16)


def _spatial_gate(z, v_gain, layer, wmix, bias, *, tm, want_vn):
    m, w2 = z.shape
    w = w2 // 2
    groups = wmix.shape[0]
    out_shape = [jax.ShapeDtypeStruct((m, w), bf16)]
    out_specs = [pl.BlockSpec((tm, w), lambda i: (i, 0))]
    if want_vn:
        out_shape.append(jax.ShapeDtypeStruct((m, w), f32))
        out_specs.append(pl.BlockSpec((tm, w), lambda i: (i, 0)))
    res = pl.pallas_call(
        functools.partial(_gate_body, tm=tm, groups=groups),
        grid=(m // tm,),
        in_specs=[
            pl.BlockSpec((tm, w), lambda i: (i, 0)),
            pl.BlockSpec((tm, w), lambda i: (i, 1)),
            pl.BlockSpec((None, 1, w), lambda i: (layer, 0, 0)),
            pl.BlockSpec(wmix.shape, lambda i: (0, 0, 0)),
            pl.BlockSpec(bias.shape, lambda i: (0, 0)),
        ],
        out_specs=out_specs,
        out_shape=out_shape,
        compiler_params=_params("parallel"),
        name="spatial_gate",
    )(z, z, _rows(v_gain), wmix, bias)
    return res if want_vn else (res[0], None)


def _conv3(a, cw, cb, prev1, prev2):
    return cb + cw[0:1] * prev2 + cw[1:2] * prev1 + cw[2:3] * a


def _ffn_body(x_ref, g_ref, wg_ref, wu_ref, cwg_ref, cwu_ref, cbg_ref, cbu_ref, wo_ref, *rest,
              sample, seq_tiles, seq_len, nj):
    if sample:
        sg_ref, su_ref, o_ref, ag_ref, au_ref, h_ref, act0_ref, act1_ref = rest
    else:
        o_ref, tg_ref, tu_ref, h_ref, act0_ref, act1_ref, carry_ref = rest
    acts = (act0_ref, act1_ref)
    i = pl.program_id(0)
    j = pl.program_id(1)

    @pl.when(j == 0)
    def _():
        x = x_ref[...]
        h_ref[...] = _rms(x, g_ref[...]).astype(bf16)
        o_ref[...] = x
        act1_ref[...] = jnp.zeros(act1_ref.shape, bf16)

    def down(src_ref):
        o_ref[...] += _dot(src_ref[...], wo_ref[...])

    def up_conv_gate(dst_ref):
        h = h_ref[...]
        ag = _dot(h, wg_ref[...])
        au = _dot(h, wu_ref[...])
        tm, tn = ag.shape
        row = lax.broadcasted_iota(i32, (tm, tn), 0)
        if sample:
            t = row & (seq_len - 1)
            m1 = t == 0
            m2 = t < 2
            p2g, p2u = sg_ref[...], su_ref[...]
            p1g, p1u = pltpu.roll(p2g, tm - 1, 0), pltpu.roll(p2u, tm - 1, 0)
            for a, a_ref in ((ag, ag_ref), (au, au_ref)):
                last = pltpu.roll(a, tm - (seq_len - 2), 0).reshape(tm // seq_len, seq_len, tn)
                a_ref[...] = last[:, :2, :]
        else:
            m1 = row == 0
            m2 = row < 2
            live = (i % seq_tiles) != 0
            cg = jnp.where(live, carry_ref[j, 0], 0.0)
            cu = jnp.where(live, carry_ref[j, 1], 0.0)
            p1g, p1u = cg[7:8], cu[7:8]
            p2g = jnp.where(m1, cg[6:7], cg[7:8])
            p2u = jnp.where(m1, cu[6:7], cu[7:8])
            carry_ref[j, 0] = ag[tm - 8:]
            carry_ref[j, 1] = au[tm - 8:]
            tg_ref[0] = ag[tm - 8:]
            tu_ref[0] = au[tm - 8:]
        a1g = jnp.where(m1, p1g, pltpu.roll(ag, 1, 0))
        a2g = jnp.where(m2, p2g, pltpu.roll(ag, 2, 0))
        a1u = jnp.where(m1, p1u, pltpu.roll(au, 1, 0))
        a2u = jnp.where(m2, p2u, pltpu.roll(au, 2, 0))
        cg_ = _conv3(ag, cwg_ref[...], cbg_ref[...], a1g, a2g)
        cu_ = _conv3(au, cwu_ref[...], cbu_ref[...], a1u, a2u)
        dst_ref[...] = (cg_ * (1.0 / (1.0 + jnp.exp(-cg_))) * cu_).astype(bf16)

    for parity in (0, 1):
        @pl.when((j < nj) & ((j & 1) == parity))
        def _(parity=parity):
            down(acts[1 - parity])
            up_conv_gate(acts[parity])

    @pl.when(j == nj)
    def _():
        down(acts[(nj - 1) % 2])


def _conv_ffn(x, gain, w_in, conv_w, conv_b, w_out, layer, *, tm, tn, seq_len, state=None):
    m, d = x.shape
    ff = w_out.shape[1]
    nj = ff // tn
    ni = m // tm
    sample = state is not None
    up = lambda j: jnp.minimum(j, nj - 1)
    down = lambda j: jnp.maximum(j - 1, 0)
    in_specs = [
        pl.BlockSpec((tm, d), lambda i, j: (i, 0), pipeline_mode=pl.Buffered(1)),
        pl.BlockSpec((None, 1, d), lambda i, j: (layer, 0, 0)),
        pl.BlockSpec((None, d, tn), lambda i, j: (layer, 0, up(j))),
        pl.BlockSpec((None, d, tn), lambda i, j: (layer, 0, nj + up(j))),
        pl.BlockSpec((None, 3, tn), lambda i, j: (layer, 0, up(j))),
        pl.BlockSpec((None, 3, tn), lambda i, j: (layer, 0, nj + up(j))),
        pl.BlockSpec((None, 1, tn), lambda i, j: (layer, 0, up(j))),
        pl.BlockSpec((None, 1, tn), lambda i, j: (layer, 0, nj + up(j))),
        pl.BlockSpec((None, tn, d), lambda i, j: (layer, down(j), 0)),
    ]
    conv_b = _rows(conv_b)
    args = [x, _rows(gain), w_in, w_in, conv_w, conv_w, conv_b, conv_b, w_out]
    out_specs = [pl.BlockSpec((tm, d), lambda i, j: (i, 0), pipeline_mode=pl.Buffered(1 if ni == 1 else 2))]
    out_shape = [jax.ShapeDtypeStruct((m, d), f32)]
    scratch = [pltpu.VMEM((tm, d), bf16), pltpu.VMEM((tm, tn), bf16), pltpu.VMEM((tm, tn), bf16)]
    if sample:
        in_specs += [
            pl.BlockSpec((tm, tn), lambda i, j: (i, up(j))),
            pl.BlockSpec((tm, tn), lambda i, j: (i, nj + up(j))),
        ]
        args += [state, state]
        out_specs += [pl.BlockSpec((tm // seq_len, 2, tn), lambda i, j: (i, 0, up(j)))] * 2
        out_shape += [jax.ShapeDtypeStruct((m // seq_len, 2, ff), f32)] * 2
        seq_tiles = 1
    else:
        out_specs += [pl.BlockSpec((1, 8, tn), lambda i, j: (i, 0, up(j)))] * 2
        out_shape += [jax.ShapeDtypeStruct((ni, 8, ff), f32)] * 2
        scratch.append(pltpu.VMEM((nj, 2, 8, tn), f32))
        seq_tiles = seq_len // tm
    return pl.pallas_call(
        functools.partial(_ffn_body, sample=sample, seq_tiles=seq_tiles, seq_len=seq_len, nj=nj),
        grid=(ni, nj + 1),
        in_specs=in_specs,
        out_specs=out_specs,
        out_shape=out_shape,
        scratch_shapes=scratch,
        compiler_params=_params("arbitrary", "arbitrary"),
        name="conv_ffn_sample" if sample else "conv_ffn_prompt",
    )(*args)


def _rope_full(x, c, s):
    return x * c + pltpu.roll(x, LANES // 2, 1) * s


def _rope_half(x, c, s, lo):
    partner = jnp.where(lo, pltpu.roll(x, 96, 1), pltpu.roll(x, 32, 1))
    return x * c + partner * s


def _post_body(p_ref, qg_ref, kg_ref, c1_ref, s1_ref, c2_ref, s2_ref,
               q_ref, k_ref, v_ref, kb_ref, vb_ref, iq_ref, ikw_ref, ik2_ref,
               *, n_heads, n_kv, n_ih, idx_scale, q_scale):
    hd = LANES
    c1, s1, c2, s2 = c1_ref[...], s1_ref[...], c2_ref[...], s2_ref[...]
    tm = p_ref.shape[0]
    lane = lax.broadcasted_iota(i32, (tm, LANES), 1)
    lo = (lane & 63) < 32
    off = 0
    for h in range(n_heads):
        x = p_ref[:, off + h * hd: off + (h + 1) * hd]
        q_ref[:, h * hd:(h + 1) * hd] = (_rope_full(_rms(x, qg_ref[...]), c1, s1) * q_scale).astype(bf16)
    off += n_heads * hd
    for h in range(n_kv):
        x = p_ref[:, off + h * hd: off + (h + 1) * hd]
        kr = _rope_full(_rms(x, kg_ref[...]), c1, s1)
        k_ref[pl.ds(h, tm, stride=n_kv), :] = kr
        kb_ref[:, h * hd:(h + 1) * hd] = kr.astype(bf16)
    off += n_kv * hd
    vv = p_ref[:, off: off + n_kv * hd]
    for h in range(n_kv):
        v_ref[pl.ds(h, tm, stride=n_kv), :] = vv[:, h * hd:(h + 1) * hd]
    vb_ref[...] = vv.astype(bf16)
    off += n_kv * hd
    for h in range(n_ih // 2):
        x = p_ref[:, off + h * LANES: off + (h + 1) * LANES]
        iq_ref[:, h * LANES:(h + 1) * LANES] = _rope_half(x, c2, s2, lo).astype(bf16)
    off += (n_ih // 2) * LANES
    tail = p_ref[:, off: off + LANES]
    tr = _rope_half(tail, c2, s2, lo)
    ik = jnp.where(lane < 64, tr, 0.0)
    ik2_ref[...] = (ik + pltpu.roll(ik, 64, 1)).astype(bf16)
    ikw_ref[...] = jnp.where(lane < 64, tr, tail * idx_scale)


def _attn_post(p, q_gain, k_gain, layer, tabs, *, tm, n_heads, n_kv, n_ih):
    m, npad = p.shape
    c1, s1, c2, s2 = tabs
    nt = c1.shape[0] // tm
    hd = LANES
    tab_spec = pl.BlockSpec((tm, LANES), lambda i: (i % nt, 0))
    row = lambda w: pl.BlockSpec((tm, w), lambda i: (i, 0))
    gain_spec = pl.BlockSpec((None, 1, hd), lambda i: (layer, 0, 0))
    outs = [
        (1, n_heads * hd, bf16), (n_kv, hd, f32), (n_kv, hd, f32), (1, n_kv * hd, bf16), (1, n_kv * hd, bf16),
        (1, n_ih * 64, bf16), (1, LANES, f32), (1, LANES, bf16),
    ]
    return pl.pallas_call(
        functools.partial(_post_body, n_heads=n_heads, n_kv=n_kv, n_ih=n_ih,
                          idx_scale=float((n_ih * 64) ** -0.5), q_scale=float(hd ** -0.5)),
        grid=(m // tm,),
        in_specs=[row(npad), gain_spec, gain_spec, tab_spec, tab_spec, tab_spec, tab_spec],
        out_specs=[pl.BlockSpec((tm * r, w), lambda i: (i, 0)) for r, w, _ in outs],
        out_shape=[jax.ShapeDtypeStruct((m * r, w), dt) for r, w, dt in outs],
        compiler_params=_params("parallel"),
        name="attn_post",
    )(p, _rows(q_gain), _rows(k_gain), c1, s1, c2, s2)


def _key_to_float(key):
    bits = key ^ ((key >> 31) & 0x7FFFFFFF)
    return lax.bitcast_convert_type(bits, f32)


def _select_threshold(count_fn, res_ref, shape, topk, bits_per_round):
    kk = float(topk)
    n_cand = (1 << bits_per_round) - 1

    def round_body(it, carry):
        res, cnt_res = carry
        step = lax.shift_left(jnp.int32(1), 32 - bits_per_round * (it + 1))
        cands = [res + step * (c + 1) for c in range(n_cand)]
        floats = [_key_to_float(c) for c in cands]
        cnts = count_fn(tuple((lambda sc, idx, cf=cf: sc >= cf) for cf in floats))
        for c, n in zip(cands, cnts):
            res = jnp.where(n >= kk, c, res)
            cnt_res = jnp.where(n >= kk, n, cnt_res)
        return res, cnt_res

    stops = [-(-b // bits_per_round) for b in EXIT_CHECK_BITS] + [32 // bits_per_round]
    res, cnt = lax.fori_loop(0, stops[0], round_body, (jnp.full(shape, INT_MIN, i32), jnp.zeros(shape, f32)))
    res_ref[...] = res

    def continue_from(k, res, cnt):
        unsettled = jnp.where((cnt != kk) & (res != INT_MIN), 1.0, 0.0)

        @pl.when(jnp.max(unsettled) > 0.0)
        def _():
            r, c = lax.fori_loop(stops[k], stops[k + 1], round_body, (res, cnt))
            res_ref[...] = r
            if k + 2 < len(stops):
                continue_from(k + 1, r, c)

    continue_from(0, res, cnt)
    return _key_to_float(jnp.maximum(res_ref[...], KEY_NEG_INF))


def _tie_search(count_fn, thr, need, shape, idx_bits):
    def tie_body(it, end):
        cand = end + lax.shift_left(jnp.int32(1), idx_bits - 1 - it)
        (cnt,) = count_fn((lambda sc, idx: (sc == thr) & (idx < cand),))
        return jnp.where(cnt <= need, cand, end)

    return lax.fori_loop(0, idx_bits, tie_body, jnp.zeros(shape, i32))


def _select_topk(count_fn, tie_ref, shape, topk, idx_bits, bits_per_round):
    kk = float(topk)
    thr = _select_threshold(count_fn, tie_ref, shape, topk, bits_per_round)
    n_gt, n_ge = count_fn((lambda sc, idx: sc > thr, lambda sc, idx: sc >= thr))
    tie_ref[...] = jnp.full(shape, TIE_ALL, i32)
    crowded = jnp.where((n_ge > kk) & (thr > -jnp.inf), 1.0, 0.0)

    @pl.when(jnp.max(crowded) > 0.0)
    def _():
        tie_ref[...] = _tie_search(count_fn, thr, kk - n_gt, shape, idx_bits)

    return thr, tie_ref[...]


def _dsa_prompt_body(q_ref, iq_ref, iwt_ref, ik2_ref, kb_ref, vb_ref, bound_ref, o_ref,
                     sc_ref, iqm_ref, qg_ref, tie_ref, m_ref, acc_ref,
                     *, topk, n_kv, group, n_ih, idx_bits, tq):
    T = LANES
    nch = tq // T
    strips = tq // 8
    qb = pl.program_id(1)
    nkt = qb + 1
    lane = lax.broadcasted_iota(i32, (tq, T), 1)

    for h in range(n_ih):
        pair = iq_ref[0, :, (h // 2) * T:(h // 2 + 1) * T].astype(f32)
        keep = (lane < 64) if h % 2 == 0 else (lane >= 64)
        iqm_ref[h] = jnp.where(keep, pair, 0.0).astype(bf16)
    for h in range(n_kv * group):
        qg_ref[h // group, (h % group) * tq:(h % group + 1) * tq, :] = q_ref[0, :, h * T:(h + 1) * T]

    def key_rows(kt):
        return pl.ds(pl.multiple_of(kt * tq, tq), tq)

    key_in_tile = (lax.broadcasted_iota(i32, (strips, 8, tq), 0) * 8
                   + lax.broadcasted_iota(i32, (strips, 8, tq), 1))
    qidx = qb * tq + lax.broadcasted_iota(i32, (strips, 8, tq), 2)

    def score_tile(kt, carry):
        ik_t = ik2_ref[0, key_rows(kt), :]
        acc = jnp.zeros((tq, tq), f32)
        for h in range(n_ih):
            acc = acc + jnp.maximum(_dot_nt(ik_t, iqm_ref[h]), 0.0) * iwt_ref[0, h:h + 1, :]
        admissible = kt * tq + key_in_tile <= qidx
        sc_ref[kt] = jnp.where(admissible, acc.reshape(strips, 8, tq), -jnp.inf).reshape(tq, tq)
        return carry

    lax.fori_loop(0, nkt, score_tile, 0)

    def count_fn(preds):
        def body(kt, accs):
            sc = sc_ref[kt].reshape(strips, 8, tq)
            kidx = kt * tq + key_in_tile
            return tuple(a + jnp.sum(jnp.where(pred(sc, kidx), 1.0, 0.0).reshape(-1, 8 * COUNT_CHAINS, tq), axis=0)
                         for a, pred in zip(accs, preds))

        accs = lax.fori_loop(0, nkt, body, tuple(jnp.zeros((8 * COUNT_CHAINS, tq), f32) for _ in preds))
        return tuple(jnp.broadcast_to(jnp.sum(a, axis=0, keepdims=True), (8, tq)) for a in accs)

    thr, tie_end = _select_topk(count_fn, tie_ref, (8, tq), topk, idx_bits, 1)

    def bias_tile(kt, carry):
        sc = sc_ref[kt].reshape(strips, 8, tq)
        kidx = kt * tq + key_in_tile
        sel = ((sc > thr) | ((sc == thr) & (kidx < tie_end))) & (kidx <= qidx)
        sc_ref[kt] = jnp.where(sel, 0.0, NEG_BIG).reshape(tq, tq).T
        return carry

    lax.fori_loop(0, nkt, bias_tile, 0)

    gq = group * tq

    def scores(kt, kvh):
        k_t = kb_ref[0, key_rows(kt), kvh * T:(kvh + 1) * T]
        s = _dot_nt(qg_ref[kvh], k_t)
        return (s.reshape(group, tq, tq) + sc_ref[kt][None]).reshape(gq, tq)

    bound = bound_ref[...]
    m_ref[...] = jnp.broadcast_to(bound[None], m_ref.shape)

    @pl.when(jnp.max(bound) > SAFE_SHIFT_BOUND)
    def _():
        m_ref[...] = jnp.full(m_ref.shape, NEG_BIG, f32)

        def max_tile(kt, carry):
            for kvh in range(n_kv):
                s = scores(kt, kvh)
                mx = m_ref[kvh]
                for c in range(nch):
                    mx = jnp.maximum(mx, s[:, c * T:(c + 1) * T])
                m_ref[kvh] = mx
            return carry

        lax.fori_loop(0, nkt, max_tile, 0)
        for kvh in range(n_kv):
            m_ref[kvh] = jnp.broadcast_to(jnp.max(m_ref[kvh], axis=1, keepdims=True), (gq, T))

    acc_ref[...] = jnp.zeros(acc_ref.shape, f32)
    ones = jnp.ones((tq, T), bf16)

    def pv_tile(kt, carry):
        for kvh in range(n_kv):
            s = scores(kt, kvh)
            mx = m_ref[kvh]
            p = jnp.concatenate([jnp.exp(s[:, c * T:(c + 1) * T] - mx) for c in range(nch)], axis=1)
            v1 = jnp.concatenate([vb_ref[0, key_rows(kt), kvh * T:(kvh + 1) * T], ones], axis=1)
            acc_ref[kvh] += _dot(p.astype(bf16), v1)
        return carry

    lax.fori_loop(0, nkt, pv_tile, 0)
    for kvh in range(n_kv):
        o = acc_ref[kvh, :, :T] / acc_ref[kvh, :, T:]
        for g in range(group):
            h = kvh * group + g
            o_ref[0, :, h * T:(h + 1) * T] = o[g * tq:(g + 1) * tq].astype(bf16)


def _dsa_prompt(q, iq, iwt, ik2, kb, vb, bound, *, topk, n_kv, n_ih, tq):
    b, s, qd = q.shape
    T = LANES
    n_heads = qd // T
    group = n_heads // n_kv
    nq = s // tq
    blk = lambda w: pl.BlockSpec((1, tq, w), lambda bi, qi: (bi, qi, 0))
    full = lambda w: pl.BlockSpec((1, s, w), lambda bi, qi: (bi, 0, 0))
    att = pltpu.VMEM((n_kv, group * tq, T), f32)
    return pl.pallas_call(
        functools.partial(_dsa_prompt_body, topk=topk, n_kv=n_kv, group=group, n_ih=n_ih,
                          idx_bits=int(s).bit_length(), tq=tq),
        grid=(b, nq),
        in_specs=[blk(qd), blk(iq.shape[2]), pl.BlockSpec((1, n_ih, tq), lambda bi, qi: (bi, 0, qi)),
                  full(T), full(n_kv * T), full(n_kv * T), pl.BlockSpec((1, T), lambda bi, qi: (0, 0))],
        out_specs=blk(qd),
        out_shape=jax.ShapeDtypeStruct((b, s, qd), bf16),
        scratch_shapes=[
            pltpu.VMEM((nq, tq, tq), f32),
            pltpu.VMEM((n_ih, tq, T), bf16),
            pltpu.VMEM((n_kv, group * tq, T), bf16), pltpu.VMEM((8, tq), i32),
            att, pltpu.VMEM((n_kv, group * tq, 2 * T), f32),
        ],
        compiler_params=_params("parallel", "arbitrary"),
        name="dsa_prompt",
    )(q, iq, iwt, ik2, kb, vb, bound)


def _dsa_sample_body(pt_ref, q_ref, iq_ref, iwb_ref, ikn_ref, kn_ref, vn_ref, *rest,
                     n_seq, n_pages, n_kv, topk, n_tok, idx_bits):
    del pt_ref
    T = LANES
    pages = [rest[i * n_pages:(i + 1) * n_pages] for i in range(3 * n_seq)]
    idx_pages, k_pages, v_pages = pages[:n_seq], pages[n_seq:2 * n_seq], pages[2 * n_seq:]
    o_ref, sc_ref, s_ref, tie_ref, thr_ref = rest[3 * n_seq * n_pages:]
    nt = n_pages + 1
    n_rows = q_ref.shape[1]
    sel_rows = n_seq * n_tok
    lane = lax.broadcasted_iota(i32, (sel_rows, T), 1)
    tok = lax.broadcasted_iota(i32, (sel_rows, T), 0) & (n_tok - 1)
    fresh = (lane <= tok) & (lane < n_tok)

    for s in range(n_seq):
        iq = iq_ref[s]
        iwb = iwb_ref[s]
        for p in range(nt):
            ikt = (idx_pages[s][p][...] if p < n_pages else ikn_ref[s]).astype(bf16)
            w = jnp.maximum(_dot(iq, ikt), 0.0) * iwb
            sc_ref[p, s * n_tok:(s + 1) * n_tok, :] = jnp.sum(w.reshape(n_rows // n_tok, n_tok, T), axis=0)
    sc_ref[n_pages] = jnp.where(fresh, sc_ref[n_pages], -jnp.inf)

    def count_fn(preds):
        accs = [jnp.zeros((sel_rows, T), f32) for _ in preds]
        for p in range(nt):
            sc = sc_ref[p]
            for n, pred in enumerate(preds):
                accs[n] = accs[n] + jnp.where(pred(sc, p * T + lane), 1.0, 0.0)
        return tuple(jnp.broadcast_to(jnp.sum(a, axis=1, keepdims=True), (sel_rows, T)) for a in accs)

    thr, _ = _select_topk(count_fn, tie_ref, (sel_rows, T), topk, idx_bits, 2)
    thr_ref[...] = thr

    lane_s = lax.broadcasted_iota(i32, (n_tok, T), 1)
    fresh_s = (lane_s <= lax.broadcasted_iota(i32, (n_tok, T), 0)) & (lane_s < n_tok)
    rows_kv = n_rows // n_kv

    def head_rows(ref, kvh):
        n_keys = ref.shape[0] // n_kv
        rows = ref[pl.ds(kvh, n_keys, stride=n_kv), :]
        if n_keys < T:
            rows = jnp.concatenate([rows, jnp.zeros((T - n_keys, T), f32)], axis=0)
        return rows.astype(bf16)

    for s in range(n_seq):
        q = q_ref[s]
        toks = slice(s * n_tok, (s + 1) * n_tok)
        thr_s, tie_s = thr_ref[toks, :], tie_ref[toks, :]
        mx = jnp.full((n_rows, T), NEG_BIG, f32)
        for p in range(nt):
            sc = sc_ref[p, toks, :]
            sel = (sc > thr_s) | ((sc == thr_s) & (p * T + lane_s < tie_s))
            if p == n_pages:
                sel = sel & fresh_s
            bias = jnp.concatenate([jnp.where(sel, 0.0, NEG_BIG)] * (rows_kv // n_tok), axis=0)
            k_ref = k_pages[s][p] if p < n_pages else kn_ref.at[s]
            sm = jnp.concatenate(
                [_dot_nt(q[kvh * rows_kv:(kvh + 1) * rows_kv], head_rows(k_ref, kvh)) + bias
                 for kvh in range(n_kv)], axis=0)
            s_ref[s, p] = sm
            mx = jnp.maximum(mx, sm)
        m = jnp.max(mx, axis=1, keepdims=True)
        l = jnp.zeros((n_rows, 1), f32)
        accs = [jnp.zeros((rows_kv, T), f32) for _ in range(n_kv)]
        for p in range(nt):
            e = jnp.exp(s_ref[s, p] - m)
            l = l + jnp.sum(e, axis=1, keepdims=True)
            eb = e.astype(bf16)
            v_ref = v_pages[s][p] if p < n_pages else vn_ref.at[s]
            for kvh in range(n_kv):
                accs[kvh] = accs[kvh] + _dot(eb[kvh * rows_kv:(kvh + 1) * rows_kv], head_rows(v_ref, kvh))
        o_ref[s] = (jnp.concatenate(accs, axis=0) / l).astype(bf16)


def _dsa_sample(page_table, layer, q, iq, iwb, ikn, kn, vn, cache_idx_t, cache_k, cache_v,
                *, topk, n_tok, n_kv):
    nb, n_rows, T = q.shape
    n_pages = page_table.shape[1]
    kv_rows = cache_k.shape[2]
    idw = cache_idx_t.shape[2]
    ns = SEQ_PER_STEP if nb % SEQ_PER_STEP == 0 else 1
    seq = lambda shape: pl.BlockSpec((ns,) + shape, lambda b, pt: (b, 0, 0))

    def pages(shape):
        return [pl.BlockSpec((None, None) + shape, lambda b, pt, s=s, p=p: (layer, pt[ns * b + s, p], 0, 0))
                for s in range(ns) for p in range(n_pages)]

    in_specs = [seq((n_rows, T)), seq((n_rows, idw)), seq((n_rows, T)), seq((idw, T)),
                seq(kn.shape[1:]), seq(vn.shape[1:])]
    in_specs += pages((idw, T)) + pages((kv_rows, T)) + pages((kv_rows, T))
    nt = n_pages + 1
    grid_spec = pltpu.PrefetchScalarGridSpec(
        num_scalar_prefetch=1,
        grid=(nb // ns,),
        in_specs=in_specs,
        out_specs=seq((n_rows, T)),
        scratch_shapes=[pltpu.VMEM((nt, ns * n_tok, T), f32), pltpu.VMEM((ns, nt, n_rows, T), f32),
                        pltpu.VMEM((ns * n_tok, T), i32), pltpu.VMEM((ns * n_tok, T), f32)],
    )
    return pl.pallas_call(
        functools.partial(_dsa_sample_body, n_seq=ns, n_pages=n_pages, n_kv=n_kv, topk=topk, n_tok=n_tok,
                          idx_bits=int(nt * T).bit_length()),
        grid_spec=grid_spec,
        out_shape=jax.ShapeDtypeStruct((nb, n_rows, T), bf16),
        compiler_params=_params("parallel"),
        name="dsa_sample",
    )(page_table, q, iq, iwb, ikn, kn, vn,
      *([cache_idx_t] * (ns * n_pages)), *([cache_k] * (ns * n_pages)), *([cache_v] * (ns * n_pages)))


def _rope_tables(pos, dim):
    half = dim // 2
    inv = ROPE_THETA ** (-jnp.arange(half, dtype=f32) * (2.0 / dim))
    ang = pos.astype(f32)[:, None] * inv[None, :]
    cos = jnp.concatenate([jnp.cos(ang), jnp.cos(ang)], axis=1)
    sin = jnp.concatenate([-jnp.sin(ang), jnp.sin(ang)], axis=1)
    reps = LANES // dim
    return jnp.tile(cos, (1, reps)), jnp.tile(sin, (1, reps))


def _mix_tables(w_s, b_s, c, group_dim):
    causal = jnp.tril(jnp.ones((c, c), dtype=bool))
    ws = jnp.where(causal[None], w_s[:, :c, :c], 0)
    reps = LANES // c
    ws = jnp.tile(ws, (1, reps, reps))
    blk = jnp.arange(LANES) // c
    ws = jnp.where((blk[:, None] == blk[None, :])[None], ws, 0).astype(bf16)
    bias = jnp.tile(jnp.repeat(b_s[:, :c].T, group_dim, axis=1), (reps, 1)).astype(f32)
    return ws, bias


def kernel(x_prompt, x_sample, cache_k, cache_v, cache_idx_k, state_ffn_conv, page_table, a_norm, a_w_in, a_v_norm, a_w_s, a_b_s, a_w_out, b_norm, b_w_in, b_q_norm, b_k_norm, b_w_o, f_norm, f_w_in, f_conv_w, f_conv_b, f_w_out):
    bsz, seq, d = x_prompt.shape
    nb, n_tok, _ = x_sample.shape
    depth = f_norm.shape[0]
    n_layers_b, n_pool, page, n_kv, hd = cache_k.shape
    idw = cache_idx_k.shape[3]
    past = page_table.shape[1] * page
    n_heads = b_w_o.shape[1] // hd
    group = n_heads // n_kv
    n_ih = (b_w_in.shape[2] - (n_heads + 2 * n_kv) * hd - idw) // (idw + 1)
    ff = f_w_out.shape[1]
    a_groups, chunk = a_w_s.shape[1], a_w_s.shape[2]
    a_width = a_w_in.shape[2] // 2
    assert hd == LANES and idw == 64 and page == LANES and chunk == LANES and n_ih % 2 == 0
    assert f_conv_w.shape[1] == 3 and n_tok >= 2 and LANES % n_tok == 0 and n_tok & (n_tok - 1) == 0

    mp, ms = bsz * seq, nb * n_tok
    xp = x_prompt.reshape(mp, d)
    xs = x_sample.reshape(ms, d)
    tm_p = min(TM_ROWS, seq)
    tm_s = min(TM_ROWS, ms)
    tq = min(TQ_PROMPT, seq)

    aw_in, aw_out = a_w_in.astype(bf16), a_w_out.astype(bf16)
    n_real = b_w_in.shape[2]
    n_pad = -(-n_real // LANES) * LANES
    bw_in = jnp.pad(b_w_in, ((0, 0), (0, 0), (0, n_pad - n_real))).astype(bf16)
    bw_o = b_w_o.astype(bf16)
    tn_b = LANES * 11 if n_pad % (LANES * 11) == 0 else LANES
    tn_f = 512 if ff % 512 == 0 else LANES
    fw_in, fw_out = f_w_in.astype(bf16), f_w_out.astype(bf16)

    ck = cache_k.reshape(n_layers_b, n_pool, page * n_kv, hd)
    cv = cache_v.reshape(n_layers_b, n_pool, page * n_kv, hd)
    cidx_t = jnp.swapaxes(cache_idx_k, 2, 3)
    outs = dict(k_p=[], v_p=[], ik_p=[], k_s=[], v_s=[], ik_s=[], chunk_v=[], conv_p=[], conv_s=[])
    n_mixers = 2
    for layer in range(depth):
        j = layer // n_mixers
        if layer % n_mixers == 0:
            new = []
            for x, tm, c, want in ((xp, tm_p, min(seq, chunk), False), (xs, tm_s, min(n_tok, chunk), True)):
                wmix, bias = _mix_tables(a_w_s[j], a_b_s[j], c, a_width // a_groups)
                z = _norm_matmul(x, a_norm, aw_in, j, act="gelu", tm=min(TM_FFN, x.shape[0]),
                                 tn=1024 if a_width % 1024 == 0 else 512)
                gated, vn = _spatial_gate(z, a_v_norm, j, wmix, bias, tm=min(tm, 512), want_vn=want)
                new.append(_matmul_residual(gated, aw_out, j, x, tm=tm, tn=d))
                if want:
                    outs["chunk_v"].append(vn.reshape(nb, n_tok, a_width))
            xp, xs = new
        else:
            pos_p = jnp.arange(seq, dtype=i32)
            pos_s = jnp.tile(past + jnp.arange(n_tok, dtype=i32), nb)
            res = []
            for x, tm, pos in ((xp, tm_p, pos_p), (xs, tm_s, pos_s)):
                p = _norm_matmul(x, b_norm, bw_in, j, tm=min(TM_FFN, x.shape[0]), tn=tn_b)
                tabs = _rope_tables(pos, hd) + _rope_tables(pos, idw)
                res.append(_attn_post(p, b_q_norm, b_k_norm, j, tabs, tm=tm,
                                      n_heads=n_heads, n_kv=n_kv, n_ih=n_ih))
            q, k, v, kb, vb, iq, ikw, ik2 = res[0]
            r3 = lambda a: a.reshape(bsz, seq, a.shape[1])
            iwt = jnp.swapaxes(ikw[:, idw:idw + n_ih].reshape(bsz, seq, n_ih), 1, 2)
            bound = 1.01 * math.sqrt(hd) * jnp.max(jnp.abs(b_q_norm[j])) * jnp.max(jnp.abs(b_k_norm[j]))
            bound = jnp.full((1, LANES), bound, f32)
            o = _dsa_prompt(r3(q), r3(iq), iwt, r3(ik2), r3(kb), r3(vb), bound,
                            topk=min(TOPK_MAX, seq // 4), n_kv=n_kv, n_ih=n_ih, tq=tq)
            xp = _matmul_residual(o.reshape(mp, n_heads * hd), bw_o, j, xp, tm=tm_p, tn=d)
            n_seq_pages = seq // page
            outs["k_p"].append(k.reshape(bsz, n_seq_pages, page, n_kv, hd))
            outs["v_p"].append(v.reshape(bsz, n_seq_pages, page, n_kv, hd))
            outs["ik_p"].append(ikw[:, :idw].reshape(bsz, n_seq_pages, page, idw))

            q, k, v, kb, vb, iq, ikw, ik2 = res[1]
            hm = lambda a, w: a.reshape(nb, n_tok, a.shape[1] // w, w).transpose(0, 2, 1, 3).reshape(nb, -1, w)
            qs = hm(q, hd)
            iqs = hm(iq, idw)
            iw = ikw[:, idw:idw + n_ih].reshape(nb, n_tok, n_ih).transpose(0, 2, 1).reshape(nb, n_ih * n_tok, 1)
            iwb = jnp.broadcast_to(iw, (nb, n_ih * n_tok, LANES))
            pad_tok = lambda a: jnp.pad(a.reshape(nb, n_tok, -1), ((0, 0), (0, page - n_tok), (0, 0)))
            ikn = jnp.swapaxes(pad_tok(ikw[:, :idw]), 1, 2)
            kn = k.reshape(nb, n_tok * n_kv, hd)
            vn_ = v.reshape(nb, n_tok * n_kv, hd)
            os_ = _dsa_sample(page_table, j, qs, iqs, iwb, ikn, kn, vn_, cidx_t, ck, cv,
                              topk=min(TOPK_MAX, (past + n_tok) // 4), n_tok=n_tok, n_kv=n_kv)
            os_ = os_.reshape(nb, n_heads, n_tok, hd).transpose(0, 2, 1, 3).reshape(ms, n_heads * hd)
            xs = _matmul_residual(os_, bw_o, j, xs, tm=tm_s, tn=d)
            outs["k_s"].append(k.reshape(nb, n_tok, n_kv, hd))
            outs["v_s"].append(v.reshape(nb, n_tok, n_kv, hd))
            outs["ik_s"].append(ikw[:, :idw].reshape(nb, n_tok, idw))

        tm_f = min(TM_FFN, seq)
        xp, tg, tu = _conv_ffn(xp, f_norm, fw_in, f_conv_w, f_conv_b, fw_out, layer,
                               tm=tm_f, tn=tn_f, seq_len=seq)
        tails = jnp.concatenate([tg, tu], axis=2).reshape(bsz, seq // tm_f, 8, 2 * ff)
        outs["conv_p"].append(tails[:, -1, 6:8, :])
        st = state_ffn_conv[layer]
        st = jnp.pad(st, ((0, 0), (0, n_tok - 2), (0, 0))).reshape(ms, 2 * ff)
        xs, ng, nu = _conv_ffn(xs, f_norm, fw_in, f_conv_w, f_conv_b, fw_out, layer,
                               tm=min(TM_FFN, ms), tn=tn_f, seq_len=n_tok, state=st)
        outs["conv_s"].append(jnp.concatenate([ng, nu], axis=2))

    st = lambda key: jnp.stack(outs[key])
    return (xp.reshape(bsz, seq, d), xs.reshape(nb, n_tok, d), st("k_p"), st("v_p"), st("ik_p"),
            st("k_s"), st("v_s"), st("ik_s"), st("chunk_v"), st("conv_p"), st("conv_s"))
```
